```python
import math
import jax, jax.numpy as jnp
from jax import lax
import numpy as np

D_MODEL = 1024
BATCH = 2
SEQ = 16384
DEPTH = 4

GRID_W = 64
CTX_LEN = 256
N_MIXERS = 3
N_HY = (DEPTH + 2) // 3
N_GD = (DEPTH + 1) // 3
N_FN = DEPTH // 3

DN_ALPHA = (2 * DEPTH) ** 0.25
DN_BETA = (8 * DEPTH) ** -0.25
LN_EPS = 1e-5
N_MOD = 6

HY_CONV = 3
HY_EMB = 33
HY_BANDS = (HY_EMB - 1) // 2
HY_FFN = 64
HY_SHIFT = 0.05
HY_TARGET = 1e-2
HY_MIN_DECAY = math.log(HY_TARGET) / 1.5
HY_MAX_DECAY = math.log(HY_TARGET) / 0.3
HY_FILTER_INIT = 0.02

GD_HEADS = 8
GD_DK = 128
GD_DV = 128
GD_WIDTH = GD_HEADS * GD_DK
GD_CONV = 3
GD_CHUNK = 64
GD_IN = 4 * GD_WIDTH + 4 * GD_HEADS

FN_GROUPS = 4

PK_HEADS = 8
PK_NKEYS = 128
PK_EXPERTS = PK_NKEYS * PK_NKEYS
PK_DH = 128
PK_TOPK = 16
PK_BLOCK = 128

kernel_name = "hybrid_hyena_gdn_fnet_peer_diffusion_trunk"


def layer_norm(x, g, b):
    xf = x.astype(jnp.float32)
    mu = jnp.mean(xf, -1, keepdims=True)
    var = jnp.mean(jnp.square(xf - mu), -1, keepdims=True)
    return ((xf - mu) * lax.rsqrt(var + LN_EPS) * g + b).astype(x.dtype)


def conv_centred(u, w):
    k_w = w.shape[0]
    r = k_w // 2
    length = u.shape[1]
    up = jnp.pad(u, ((0, 0), (r, r), (0, 0)))
    out = up[:, 0:length] * w[0]
    for k in range(1, k_w):
        out = out + up[:, k:k + length] * w[k]
    return out


def sincos_2d(rows, dim):
    quarter = dim // 4
    omega = 1.0 / (10000.0 ** (jnp.arange(quarter, dtype=jnp.float32) / quarter))
    er = jnp.arange(rows, dtype=jnp.float32)[:, None] * omega
    ec = jnp.arange(GRID_W, dtype=jnp.float32)[:, None] * omega
    emb_r = jnp.concatenate([jnp.sin(er), jnp.cos(er)], -1)
    emb_c = jnp.concatenate([jnp.sin(ec), jnp.cos(ec)], -1)
    pe = jnp.concatenate([
        jnp.broadcast_to(emb_r[:, None, :], (rows, GRID_W, dim // 2)),
        jnp.broadcast_to(emb_c[None, :, :], (rows, GRID_W, dim // 2))], -1)
    return pe.reshape(rows * GRID_W, dim)


def hyena_filter(length, f_w1, f_b1, f_w2, f_b2, f_w3):
    d = f_w3.shape[1] // 2
    t = jnp.arange(length, dtype=jnp.float32) / length
    bands = jnp.linspace(1e-4, HY_BANDS - 1, HY_BANDS, dtype=jnp.float32)
    ang = 2.0 * jnp.pi * t[:, None] * bands
    feat = jnp.concatenate([t[:, None], jnp.cos(ang), -jnp.sin(ang)], -1)
    hdn = jnp.sin(feat @ f_w1 + f_b1)
    hdn = jnp.sin(hdn @ f_w2 + f_b2)
    hf = (hdn @ f_w3).astype(jnp.float32)
    deltas = jnp.abs(jnp.linspace(HY_MIN_DECAY, HY_MAX_DECAY, d, dtype=jnp.float32))
    win = jnp.exp(-t[:, None] * deltas) + HY_SHIFT
    h_fwd = hf[:, :d] * win
    h_bwd = hf[:, d:] * win
    return jnp.concatenate([h_fwd, jnp.zeros((1, d), jnp.float32), h_bwd[:0:-1]], 0)


def hyena_mixer(h, in_w, in_b, conv_w, f_w1, f_b1, f_w2, f_b2, f_w3, skip, out_w, out_b):
    length = h.shape[1]
    u = conv_centred(h @ in_w + in_b, conv_w)
    x0, x1, v = jnp.split(u, 3, axis=-1)
    z = (v * x1).astype(jnp.float32)
    filt = hyena_filter(length, f_w1, f_b1, f_w2, f_b2, f_w3)
    n = 2 * length
    y = jnp.fft.irfft(jnp.fft.rfft(z, n=n, axis=1) * jnp.fft.rfft(filt, n=n, axis=0)[None],
                      n=n, axis=1)[:, :length]
    y = (y + skip * z) * x0
    return y.astype(h.dtype) @ out_w + out_b


def to_heads(t):
    b, length, _ = t.shape
    return jnp.transpose(t.reshape(b, length, GD_HEADS, -1), (0, 2, 1, 3)).astype(jnp.float32)


def l2norm(t):
    return t * lax.rsqrt(jnp.sum(t * t, -1, keepdims=True) + 1e-6)


def gdn_inputs(h, in_w, conv_w, a_log, dt_bias, with_query):
    b, length, _ = h.shape
    w = GD_WIDTH
    u = h @ in_w

    def branch(s):
        cols = slice(s * w, (s + 1) * w)
        return to_heads(jax.nn.silu(conv_centred(u[..., cols], conv_w[:, cols])))

    k = l2norm(branch(1))
    v = branch(2)
    q = l2norm(branch(0)) * GD_DK ** -0.5 if with_query else None
    z = u[..., 3 * w:4 * w] if with_query else None
    ab = u[..., 4 * w:].astype(jnp.float32).reshape(b, length, 2, 2, GD_HEADS)
    ab = jnp.transpose(ab, (2, 3, 0, 4, 1))
    g = -jnp.exp(a_log)[:, None, :, None] * jax.nn.softplus(ab[0] + dt_bias[:, None, :, None])
    beta = jax.nn.sigmoid(ab[1])
    return q, k, v, g, beta, z


def gated_delta_chunked(q, k, v, g, beta, s0):
    b, nh, length, dk = k.shape
    dv = v.shape[-1]
    c = GD_CHUNK
    n = length // c
    k = k.reshape(b, nh, n, c, dk)
    v = v.reshape(b, nh, n, c, dv)
    gc = jnp.cumsum(g.reshape(b, nh, n, c), -1)
    beta = beta.reshape(b, nh, n, c)
    causal = jnp.tril(jnp.ones((c, c), bool))
    strict = jnp.tril(jnp.ones((c, c), bool), -1)
    decay = jnp.exp(jnp.where(causal, gc[..., :, None] - gc[..., None, :], -jnp.inf))
    kb = k * beta[..., None]
    a_mat = jnp.where(strict, jnp.einsum('bhncd,bhnsd->bhncs', kb, k) * decay, 0.0) + jnp.eye(c, dtype=jnp.float32)
    rhs = jnp.concatenate([v * beta[..., None], kb * jnp.exp(gc)[..., None]], -1)
    sol = lax.linalg.triangular_solve(a_mat, rhs, left_side=True, lower=True, unit_diagonal=True)
    u_c, w_c = sol[..., :dv], sol[..., dv:]
    kd = k * jnp.exp(gc[..., -1:] - gc)[..., None]
    glast = jnp.exp(gc[..., -1])
    with_out = q is not None
    xs = [kd, u_c, w_c, glast]
    if with_out:
        q = q.reshape(b, nh, n, c, dk)
        xs += [q * jnp.exp(gc)[..., None], jnp.einsum('bhncd,bhnsd->bhncs', q, k) * decay]
    xs = tuple(jnp.moveaxis(t, 2, 0) for t in xs)

    def step(s, xs_i):
        kd_i, u_i, w_i, gl_i = xs_i[:4]
        v_new = u_i - jnp.einsum('bhcd,bhde->bhce', w_i, s)
        s_new = s * gl_i[..., None, None] + jnp.einsum('bhcd,bhce->bhde', kd_i, v_new)
        if with_out:
            qg_i, attn_i = xs_i[4:]
            o_i = jnp.einsum('bhcd,bhde->bhce', qg_i, s) + jnp.einsum('bhcs,bhse->bhce', attn_i, v_new)
        else:
            o_i = None
        return s_new, o_i

    s_fin, o = lax.scan(step, s0, xs)
    if with_out:
        o = jnp.moveaxis(o, 0, 2).reshape(b, nh, length, dv)
    return o, s_fin


def gdn_out(o, z, norm_g, out_w):
    b, nh, length, dv = o.shape
    o = jnp.transpose(o, (0, 2, 1, 3))
    o = o * lax.rsqrt(jnp.mean(o * o, -1, keepdims=True) + 1e-6) * norm_g
    o = o.reshape(b, length, nh * dv) * jax.nn.silu(z.astype(jnp.float32))
    return o.astype(z.dtype) @ out_w


def gdn_mixer(hc, hl, in_w, conv_w, a_log, dt_bias, norm_g, out_w, ctx_out):
    qc, kc, vc, gcx, bcx, zc = gdn_inputs(hc, in_w, conv_w, a_log, dt_bias, ctx_out)
    ql, kl, vl, glt, blt, zl = gdn_inputs(hl, in_w, conv_w, a_log, dt_bias, True)
    s0 = jnp.zeros((hl.shape[0], GD_HEADS, GD_DK, GD_DV), jnp.float32)

    def rev(t):
        return None if t is None else jnp.flip(t, axis=2)

    oc_f, sc_f = gated_delta_chunked(qc, kc, vc, gcx[0], bcx[0], s0)
    ol_f, _ = gated_delta_chunked(ql, kl, vl, glt[0], blt[0], sc_f)
    oc_b, sc_b = gated_delta_chunked(rev(qc), rev(kc), rev(vc), rev(gcx[1]), rev(bcx[1]), s0)
    ol_b, _ = gated_delta_chunked(rev(ql), rev(kl), rev(vl), rev(glt[1]), rev(blt[1]), sc_b)
    yl = gdn_out(ol_f + rev(ol_b), zl, norm_g, out_w)
    yc = gdn_out(oc_f + rev(oc_b), zc, norm_g, out_w) if ctx_out else None
    return yc, yl


def fnet_mixer(h, out_w, out_b):
    b, length, d = h.shape
    hg = h.reshape(b, length, FN_GROUPS, d // FN_GROUPS).astype(jnp.float32)
    y = jnp.fft.fftn(hg, axes=(1, 3), norm='ortho').real.reshape(b, length, d)
    return y.astype(h.dtype) @ out_w + out_b


def peer_ffn(h, wq, keys, u_tab, v_tab):
    b, length, d = h.shape
    blocks = h.reshape(-1, PK_BLOCK, d)

    def block(hb):
        q = (hb @ wq).reshape(PK_BLOCK, PK_HEADS, 2, PK_DH)
        s = jnp.einsum('thpd,hpkd->thpk', q, keys).astype(jnp.float32)
        sv, si = lax.top_k(s, PK_TOPK)
        cand = (sv[:, :, 0, :, None] + sv[:, :, 1, None, :]).reshape(PK_BLOCK, PK_HEADS, PK_TOPK * PK_TOPK)
        cv, ci = lax.top_k(cand, PK_TOPK)
        i1 = jnp.take_along_axis(si[:, :, 0], ci // PK_TOPK, -1)
        i2 = jnp.take_along_axis(si[:, :, 1], ci % PK_TOPK, -1)
        idx = i1 * PK_NKEYS + i2
        gate = jax.nn.softmax(cv, -1)
        act = jax.nn.gelu(jnp.einsum('td,thkd->thk', hb, u_tab[idx]).astype(jnp.float32))
        return jnp.einsum('thk,thkd->td', (gate * act).astype(v_tab.dtype), v_tab[idx])

    return lax.map(block, blocks).reshape(b, length, d).astype(h.dtype)


def setup_inputs(seed: int = 0) -> dict:
    key = jax.random.key(seed)
    ks = iter(jax.random.split(key, 48))
    f32 = jnp.float32
    d = D_MODEL
    w = GD_WIDTH

    def nrm(shape, scale):
        return jax.random.normal(next(ks), shape, f32) * scale

    dt = jnp.exp(jax.random.uniform(next(ks), (N_GD, 2, GD_HEADS), f32, math.log(1e-3), math.log(1e-1)))
    return {
        'x': nrm((BATCH, SEQ, d), 1.0),
        'c': nrm((BATCH, d), 1.0),
        'ctx': nrm((BATCH, CTX_LEN, d), 1.0),
        'c_ctx': nrm((d,), 1.0),
        'ada_w': nrm((DEPTH, d, N_MOD * d), d ** -0.5),
        'ada_b': nrm((DEPTH, N_MOD * d), 0.02),
        'ln_g': 1.0 + nrm((DEPTH, 2, d), 0.02),
        'ln_b': nrm((DEPTH, 2, d), 0.02),
        'pk_wq': nrm((DEPTH, d, PK_HEADS * 2 * PK_DH), d ** -0.5),
        'pk_keys': nrm((DEPTH, PK_HEADS, 2, PK_NKEYS, PK_DH), PK_DH ** -0.5),
        'pk_u': nrm((DEPTH, PK_EXPERTS, d), d ** -0.5),
        'pk_v': nrm((DEPTH, PK_EXPERTS, d), DN_BETA),
        'hy_in_w': nrm((N_HY, d, 3 * d), d ** -0.5),
        'hy_in_b': nrm((N_HY, 3 * d), 0.02),
        'hy_conv': nrm((N_HY, HY_CONV, 3 * d), HY_CONV ** -0.5),
        'hy_f_w1': nrm((N_HY, HY_EMB, HY_FFN), HY_EMB ** -0.5),
        'hy_f_b1': nrm((N_HY, HY_FFN), 0.1),
        'hy_f_w2': nrm((N_HY, HY_FFN, HY_FFN), HY_FFN ** -0.5),
        'hy_f_b2': nrm((N_HY, HY_FFN), 0.1),
        'hy_f_w3': nrm((N_HY, HY_FFN, 2 * d), HY_FILTER_INIT * HY_FFN ** -0.5),
        'hy_skip': nrm((N_HY, d), 1.0),
        'hy_out_w': nrm((N_HY, d, d), DN_BETA * d ** -0.5),
        'hy_out_b': nrm((N_HY, d), 0.02),
        'gd_in_w': nrm((N_GD, d, GD_IN), d ** -0.5),
        'gd_conv': nrm((N_GD, GD_CONV, 3 * w), GD_CONV ** -0.5),
        'gd_a_log': jnp.log(jax.random.uniform(next(ks), (N_GD, 2, GD_HEADS), f32, 1.0, 16.0)),
        'gd_dt_bias': dt + jnp.log(-jnp.expm1(-dt)),
        'gd_norm_g': 1.0 + nrm((N_GD, GD_DV), 0.02),
        'gd_out_w': nrm((N_GD, w, d), DN_BETA * w ** -0.5),
        'fn_out_w': nrm((N_FN, d, d), DN_BETA * d ** -0.5),
        'fn_out_b': nrm((N_FN, d), 0.02),
    }


def reference(x, c, ctx, c_ctx, ada_w, ada_b, ln_g, ln_b, pk_wq, pk_keys, pk_u, pk_v,
              hy_in_w, hy_in_b, hy_conv, hy_f_w1, hy_f_b1, hy_f_w2, hy_f_b2, hy_f_w3, hy_skip,
              hy_out_w, hy_out_b, gd_in_w, gd_conv, gd_a_log, gd_dt_bias, gd_norm_g, gd_out_w,
              fn_out_w, fn_out_b):
    b, length, d = x.shape
    rows = length // GRID_W
    x = x + sincos_2d(rows, d).astype(x.dtype)[None]
    xc = ctx
    gdn_layers = [i for i in range(DEPTH) if i % N_MIXERS == 1]
    ctx_until = gdn_layers[-1] if gdn_layers else -1
    hc = None
    mod_c = None
    for i in range(DEPTH):
        kind, j = i % N_MIXERS, i // N_MIXERS
        ctx_in, ctx_out = i <= ctx_until, i < ctx_until
        mod = (jax.nn.silu(c) @ ada_w[i] + ada_b[i]).reshape(b, N_MOD, 1, d)
        hl = x * (1 + mod[:, 1]) + mod[:, 0]
        if ctx_in:
            mod_c = (jax.nn.silu(c_ctx) @ ada_w[i] + ada_b[i]).reshape(N_MOD, 1, d)
            hc = xc * (1 + mod_c[1]) + mod_c[0]
        yc = None
        if kind == 0:
            hy = (hy_in_w[j], hy_in_b[j], hy_conv[j], hy_f_w1[j], hy_f_b1[j], hy_f_w2[j],
                  hy_f_b2[j], hy_f_w3[j], hy_skip[j], hy_out_w[j], hy_out_b[j])
            yl = hyena_mixer(hl, *hy)
            if ctx_out:
                yc = hyena_mixer(hc, *hy)
        elif kind == 1:
            yc, yl = gdn_mixer(hc, hl, gd_in_w[j], gd_conv[j], gd_a_log[j], gd_dt_bias[j],
                               gd_norm_g[j], gd_out_w[j], ctx_out)
        else:
            yl = fnet_mixer(hl, fn_out_w[j], fn_out_b[j])
            if ctx_out:
                yc = fnet_mixer(hc, fn_out_w[j], fn_out_b[j])
        pk = (pk_wq[i], pk_keys[i], pk_u[i], pk_v[i])
        x = layer_norm(DN_ALPHA * x + mod[:, 2] * yl, ln_g[i, 0], ln_b[i, 0])
        x = layer_norm(DN_ALPHA * x + mod[:, 5] * peer_ffn(x * (1 + mod[:, 4]) + mod[:, 3], *pk),
                       ln_g[i, 1], ln_b[i, 1])
        if ctx_out:
            xc = layer_norm(DN_ALPHA * xc + mod_c[2] * yc, ln_g[i, 0], ln_b[i, 0])
            xc = layer_norm(DN_ALPHA * xc + mod_c[5] * peer_ffn(xc * (1 + mod_c[4]) + mod_c[3], *pk),
                            ln_g[i, 1], ln_b[i, 1])
    return x
```

```python
import functools
import math

import numpy as np
import jax
import jax.numpy as jnp
from jax import lax
from jax.experimental import pallas as pl
from jax.experimental.pallas import tpu as pltpu
from jax.experimental.pallas import tpu_sc as plsc

F32 = jnp.float32
BF16 = jnp.bfloat16

GRID_W = 64
N_MIXERS = 3
N_MOD = 6
LN_EPS = 1e-5
HY_EMB = 33
HY_BANDS = (HY_EMB - 1) // 2
HY_SHIFT = 0.05
HY_TARGET = 1e-2
HY_MIN_DECAY = math.log(HY_TARGET) / 1.5
HY_MAX_DECAY = math.log(HY_TARGET) / 0.3
GD_HEADS = 8
GD_DK = 128
GD_CHUNK = 64
FN_GROUPS = 4
PK_HEADS = 8
PK_NKEYS = 128
PK_DH = 128
PK_TOPK = 16
PK_TOK = 128

VMEM_LIMIT_BYTES = 56 * 1024 * 1024
SC_WORKERS = 32
SC_IDX_TILE = 128
SC_ROWS = 32


def _cparams(*sem):
    return pltpu.CompilerParams(dimension_semantics=sem, vmem_limit_bytes=VMEM_LIMIT_BYTES)


def _dot(a, b):
    return jnp.dot(a, b, preferred_element_type=F32)


def _dot_hi(a, b):
    return jnp.dot(a, b, preferred_element_type=F32, precision=lax.Precision.HIGHEST)


def _split(a):
    hi = a.astype(BF16)
    lo = (a - hi.astype(F32)).astype(BF16)
    return hi, lo


def _dot3(a, b):
    ah, al = _split(a)
    bh, bl = _split(b)
    return _dot(ah, bh) + (_dot(ah, bl) + _dot(al, bh))


def _mm_body(*refs, has_mod, has_bias, silu_in):
    a_ref, w_ref = refs[0], refs[1]
    k = 2
    a = a_ref[...]
    if has_mod:
        a = a * (1.0 + refs[k][...]) + refs[k + 1][...]
        k += 2
    if silu_in:
        a = a * jax.nn.sigmoid(a)
    o = _dot(a.astype(BF16), w_ref[...])
    if has_bias:
        o = o + refs[k][...]
        k += 1
    refs[k][...] = o


def _mm(a, w_bf16, bias=None, mod=None, rows_per_group=None, silu_in=False, tm=512, tn=None):
    m, k = a.shape
    n = w_bf16.shape[1]
    tm = math.gcd(tm, m, rows_per_group or m)
    if tn is None:
        tn = max(t for t in range(128, min(n, 2048) + 1, 128) if n % t == 0)
    assert m % tm == 0 and n % tn == 0
    ins = [a, w_bf16]
    specs = [pl.BlockSpec((tm, k), lambda j, i: (i, 0)), pl.BlockSpec((k, tn), lambda j, i: (0, j))]
    if mod is not None:
        assert rows_per_group % tm == 0
        g = rows_per_group // tm
        for v in mod:
            ins.append(v)
            specs.append(pl.BlockSpec((None, 1, k), lambda j, i: (i // g, 0, 0)))
    if bias is not None:
        ins.append(bias.reshape(1, n))
        specs.append(pl.BlockSpec((1, tn), lambda j, i: (0, j)))
    return pl.pallas_call(
        functools.partial(_mm_body, has_mod=mod is not None, has_bias=bias is not None, silu_in=silu_in),
        grid=(n // tn, m // tm),
        in_specs=specs,
        out_specs=pl.BlockSpec((tm, tn), lambda j, i: (i, j)),
        out_shape=jax.ShapeDtypeStruct((m, n), F32),
        compiler_params=_cparams("parallel", "parallel"),
        name="mm",
    )(*ins)


def _rowwise(fn, rows, vecs, out_cols, rows_per_group=None, tm=512, name="rowwise"):
    m = rows[0].shape[0]
    tm = math.gcd(tm, m, rows_per_group or m)
    assert m % tm == 0
    n_r, n_v, n_o = len(rows), len(vecs), len(out_cols)

    def body(*refs):
        outs = fn(*[r[...] for r in refs[:n_r + n_v]])
        for o_ref, o in zip(refs[n_r + n_v:], outs):
            o_ref[...] = o

    specs = [pl.BlockSpec((tm, r.shape[1]), lambda i: (i, 0)) for r in rows]
    for v in vecs:
        if v.shape[0] == 1:
            specs.append(pl.BlockSpec((None, 1, v.shape[2]), lambda i: (0, 0, 0)))
        else:
            assert rows_per_group % tm == 0
            g = rows_per_group // tm
            specs.append(pl.BlockSpec((None, 1, v.shape[2]), lambda i, g=g: (i // g, 0, 0)))
    return pl.pallas_call(
        body,
        grid=(m // tm,),
        in_specs=specs,
        out_specs=[pl.BlockSpec((tm, c), lambda i: (i, 0)) for c in out_cols],
        out_shape=[jax.ShapeDtypeStruct((m, c), F32) for c in out_cols],
        compiler_params=_cparams("parallel"),
        name=name,
    )(*rows, *vecs)


def _ln(v, g, b):
    mu = jnp.mean(v, -1, keepdims=True)
    d = v - mu
    var = jnp.mean(d * d, -1, keepdims=True)
    return d * lax.rsqrt(var + LN_EPS) * g + b


def _res_ln(x, y, gate, ln_g, ln_b, alpha, rows_per_group):
    fn = lambda xt, yt, gt, lg, lb: (_ln(alpha * xt + gt * yt, lg, lb),)
    return _rowwise(fn, [x, y], [gate, ln_g, ln_b], [x.shape[1]], rows_per_group, name="res_ln")[0]


def _pos_add_body(x_ref, er_ref, ec_ref, o_ref):
    half = er_ref.shape[-1]
    x = x_ref[...]
    er = jnp.broadcast_to(er_ref[...], x.shape[:2] + (half,))
    ec = jnp.broadcast_to(ec_ref[...][None], x.shape[:2] + (half,))
    o_ref[...] = x + jnp.concatenate([er, ec], -1)


def _pos_add(x):
    b, length, d = x.shape
    rows = length // GRID_W
    quarter = d // 4
    omega = 1.0 / (10000.0 ** (jnp.arange(quarter, dtype=F32) / quarter))
    er = jnp.arange(rows, dtype=F32)[:, None] * omega
    ec = jnp.arange(GRID_W, dtype=F32)[:, None] * omega
    emb_r = jnp.concatenate([jnp.sin(er), jnp.cos(er)], -1).reshape(rows, 1, d // 2)
    emb_c = jnp.concatenate([jnp.sin(ec), jnp.cos(ec)], -1)
    rt = 8
    x4 = x.reshape(b, rows, GRID_W, d)
    out = pl.pallas_call(
        _pos_add_body,
        grid=(b, rows // rt),
        in_specs=[pl.BlockSpec((None, rt, GRID_W, d), lambda i, j: (i, j, 0, 0)),
                  pl.BlockSpec((rt, 1, d // 2), lambda i, j: (j, 0, 0)),
                  pl.BlockSpec((GRID_W, d // 2), lambda i, j: (0, 0))],
        out_specs=pl.BlockSpec((None, rt, GRID_W, d), lambda i, j: (i, j, 0, 0)),
        out_shape=jax.ShapeDtypeStruct(x4.shape, F32),
        compiler_params=_cparams("parallel", "parallel"),
        name="pos_add",
    )(x4, emb_r, emb_c)
    return out.reshape(b * length, d)


def _conv3_body(*refs, n_parts, tm, seq_len, post, n_vec):
    i = pl.program_id(0)
    first = (i * tm) % seq_len == 0
    last = ((i + 1) * tm) % seq_len == 0
    row = lax.broadcasted_iota(jnp.int32, (tm, 1), 0)
    parts = []
    for p in range(n_parts):
        main_ref, prev_ref, next_ref, w_ref = refs[4 * p:4 * p + 4]
        u = main_ref[...]
        w = w_ref[...]
        prev = jnp.where(first, 0.0, prev_ref[7:8, :])
        nxt = jnp.where(last, 0.0, next_ref[0:1, :])
        up = jnp.where(row == 0, prev, pltpu.roll(u, 1, 0))
        dn = jnp.where(row == tm - 1, nxt, pltpu.roll(u, tm - 1, 0))
        parts.append(up * w[0:1, :] + u * w[1:2, :] + dn * w[2:3, :])
    k = 4 * n_parts
    vecs = [refs[k + j][...] for j in range(n_vec)]
    outs = post(*parts, *vecs)
    for o_ref, o in zip(refs[k + n_vec:], outs):
        o_ref[...] = o


def _conv3(u, w, col_parts, tc, seq_len, post, n_out, vecs=(), tm=256, name="conv3"):
    m = u.shape[0]
    tm = min(tm, seq_len)
    assert seq_len % tm == 0 and m % tm == 0 and tm % 8 == 0
    width = col_parts[1] - col_parts[0] if len(col_parts) > 1 else tc
    ncol = width // tc
    t8 = tm // 8
    nb8 = m // 8
    ins, specs = [], []
    for c0 in col_parts:
        cb = c0 // tc
        ins += [u, u, u, w]
        specs += [
            pl.BlockSpec((tm, tc), lambda i, j, cb=cb: (i, cb + j)),
            pl.BlockSpec((8, tc), lambda i, j, cb=cb: (jnp.maximum(i * t8 - 1, 0), cb + j)),
            pl.BlockSpec((8, tc), lambda i, j, cb=cb: (jnp.minimum((i + 1) * t8, nb8 - 1), cb + j)),
            pl.BlockSpec((3, tc), lambda i, j, cb=cb: (0, cb + j)),
        ]
    for v in vecs:
        ins.append(v)
        specs.append(pl.BlockSpec((1, tc), lambda i, j: (0, j)))
    return pl.pallas_call(
        functools.partial(_conv3_body, n_parts=len(col_parts), tm=tm, seq_len=seq_len, post=post,
                          n_vec=len(vecs)),
        grid=(m // tm, ncol),
        in_specs=specs,
        out_specs=[pl.BlockSpec((tm, tc), lambda i, j: (i, j)) for _ in range(n_out)],
        out_shape=[jax.ShapeDtypeStruct((m, width), F32) for _ in range(n_out)],
        compiler_params=_cparams("parallel", "parallel"),
        name=name,
    )(*ins)


def _lmm_body(*refs, n_x):
    w_ref, o_ref = refs[0], refs[1 + n_x]
    tc = o_ref.shape[-1]
    xs = [r[...].reshape(-1, tc) for r in refs[1:1 + n_x]]
    x = xs[0] if n_x == 1 else jnp.concatenate(xs, 0)
    w = w_ref[...]
    w = w.reshape(w.shape[-2], w.shape[-1])
    o_ref[...] = _dot(w, x.astype(BF16)).reshape(o_ref.shape)


def _lmm(w, w_spec, xs, x_specs, out_shape, out_spec, grid, name):
    return pl.pallas_call(
        functools.partial(_lmm_body, n_x=len(xs)),
        grid=grid,
        in_specs=[w_spec] + list(x_specs),
        out_specs=out_spec,
        out_shape=jax.ShapeDtypeStruct(out_shape, F32),
        compiler_params=_cparams(*(["parallel"] * len(grid))),
        name=name,
    )(w, *xs)


def _spec_mul_body(w1_ref, w2_ref, x_ref, h_ref, o_ref):
    tc = o_ref.shape[-1]
    x = x_ref[...].reshape(-1, tc)
    z = _dot(w1_ref[...], x.astype(BF16))
    half = z.shape[0] // 2
    zr, zi = z[:half], z[half:]
    hr, hi = h_ref[0], h_ref[1]
    y = jnp.concatenate([zr * hr - zi * hi, zr * hi + zi * hr], 0)
    o_ref[...] = _dot(w2_ref[...], y.astype(BF16)).reshape(o_ref.shape)


def _cplx_mat(ang):
    c, s = jnp.cos(ang), jnp.sin(ang)
    return jnp.concatenate([jnp.concatenate([c, -s], -1), jnp.concatenate([s, c], -1)], -2)


def _phase(k, n):
    return (k % n).astype(F32) * (2.0 * math.pi / n)


def _outer_mats(n1, n2, n_in, sign):
    n = n1 * n2
    s1 = jnp.arange(n1, dtype=jnp.int32)[:, None, None]
    f2 = jnp.arange(n2, dtype=jnp.int32)[None, :, None]
    s2 = jnp.arange(n_in, dtype=jnp.int32)[None, None, :]
    return sign * _phase(f2 * (s1 + n1 * s2), n)


def _inner_phase(n1, sign):
    a = jnp.arange(n1, dtype=jnp.int32)
    return sign * _phase(a[:, None] * a[None, :], n1)


def _split_len(n):
    n1 = 1 << (int(math.log2(n)) // 2)
    return n1, n // n1


def _fft_conv_pair(z2, hspec, n1, n2, tcol=512):
    _, length, c = z2.shape
    n = 2 * length
    assert n1 * n2 == n
    h2 = n2 // 2
    m_in = _cplx_mat(_outer_mats(n1, n2, h2, -1.0)).astype(BF16)
    a = _lmm(m_in, pl.BlockSpec((1, 2 * n2, n2), lambda s: (s, 0, 0)),
             [z2.reshape(2, h2, n1 * c)], [pl.BlockSpec((2, h2, c), lambda s: (0, 0, s))],
             (2, n1, n2, c), pl.BlockSpec((2, 1, n2, c), lambda s: (0, s, 0, 0)), (n1,), "fft_in")
    w1 = _cplx_mat(_inner_phase(n1, -1.0)).astype(BF16)
    w2 = (_cplx_mat(_inner_phase(n1, 1.0)) * (1.0 / n)).astype(BF16)
    cols = n2 * c
    tcol = min(tcol, cols)
    b = pl.pallas_call(
        _spec_mul_body,
        grid=(cols // tcol,),
        in_specs=[pl.BlockSpec((2 * n1, 2 * n1), lambda j: (0, 0)),
                  pl.BlockSpec((2 * n1, 2 * n1), lambda j: (0, 0)),
                  pl.BlockSpec((2, n1, tcol), lambda j: (0, 0, j)),
                  pl.BlockSpec((2, n1, tcol), lambda j: (0, 0, j))],
        out_specs=pl.BlockSpec((2, n1, tcol), lambda j: (0, 0, j)),
        out_shape=jax.ShapeDtypeStruct((2, n1, cols), F32),
        compiler_params=_cparams("parallel"),
        name="fft_mid",
    )(w1, w2, a.reshape(2, n1, cols), hspec.reshape(2, n1, cols))
    if n2 == 1:
        return b.reshape(2, n, c)[:, :length]
    m_out = _cplx_mat(jnp.swapaxes(_outer_mats(n1, n2, h2, 1.0), 1, 2)).astype(BF16)
    y = _lmm(m_out, pl.BlockSpec((1, n2, 2 * n2), lambda s: (s, 0, 0)),
             [b.reshape(2, n1, n2, c)], [pl.BlockSpec((2, 1, n2, c), lambda s: (0, s, 0, 0))],
             (2, h2, n1 * c), pl.BlockSpec((2, h2, c), lambda s: (0, 0, s)), (n1,), "fft_out")
    return y.reshape(2, length, c)


def _fft_real_spectrum(f, n1, n2, tcol=512):
    n, c = f.shape
    if n2 == 1:
        ph = _inner_phase(n1, -1.0)
        w = jnp.concatenate([jnp.cos(ph), jnp.sin(ph)], 0).astype(BF16)
        tcol = min(tcol, c)
        return _lmm(w, pl.BlockSpec((2 * n1, n1), lambda j: (0, 0)),
                    [f], [pl.BlockSpec((n1, tcol), lambda j: (0, j))],
                    (2, n1, c), pl.BlockSpec((2, n1, tcol), lambda j: (0, 0, j)), (c // tcol,), "fft_spec1")
    ph = _outer_mats(n1, n2, n2, -1.0)
    m_in = jnp.concatenate([jnp.cos(ph), jnp.sin(ph)], 1).astype(BF16)
    a = _lmm(m_in, pl.BlockSpec((1, 2 * n2, n2), lambda s: (s, 0, 0)),
             [f.reshape(n2, n1 * c)], [pl.BlockSpec((n2, c), lambda s: (0, s))],
             (2, n1, n2, c), pl.BlockSpec((2, 1, n2, c), lambda s: (0, s, 0, 0)), (n1,), "fft_spec_in")
    w1 = _cplx_mat(_inner_phase(n1, -1.0)).astype(BF16)
    cols = n2 * c
    tcol = min(tcol, cols)
    h = _lmm(w1, pl.BlockSpec((2 * n1, 2 * n1), lambda j: (0, 0)),
             [a.reshape(2, n1, cols)], [pl.BlockSpec((2, n1, tcol), lambda j: (0, 0, j))],
             (2, n1, cols), pl.BlockSpec((2, n1, tcol), lambda j: (0, 0, j)), (cols // tcol,), "fft_spec_mid")
    return h.reshape(2, n, c)


def _hy_filter_body(wt_ref, wc_ref, ws_ref, b1_ref, w2_ref, b2_ref, w3_ref, bands_ref, dl_ref, o_ref,
                    *, length, tr):
    d = o_ref.shape[-1]
    j = pl.program_id(0) * tr + lax.broadcasted_iota(jnp.int32, (tr, 1), 0)
    k = jnp.where(j < length, j, 2 * length - j)
    t = k.astype(F32) / length
    ang = 2.0 * jnp.pi * t * bands_ref[...]
    pre = t * wt_ref[...] + _dot_hi(jnp.cos(ang), wc_ref[...]) + _dot_hi(-jnp.sin(ang), ws_ref[...])
    hdn = jnp.sin(pre + b1_ref[...])
    hdn = jnp.sin(_dot_hi(hdn, w2_ref[...]) + b2_ref[...])
    hf = _dot_hi(hdn, w3_ref[...])
    win = jnp.exp(-t * dl_ref[...]) + HY_SHIFT
    h = jnp.where(j < length, hf[:, :d], hf[:, d:]) * win
    o_ref[...] = jnp.where(j == length, 0.0, h)


def _hy_filter(length, f_w1, f_b1, f_w2, f_b2, f_w3):
    d = f_w3.shape[1] // 2
    ffn = f_w2.shape[0]
    tr = min(512, length)
    bands = jnp.linspace(1e-4, HY_BANDS - 1, HY_BANDS, dtype=F32).reshape(1, HY_BANDS)
    deltas = jnp.abs(jnp.linspace(HY_MIN_DECAY, HY_MAX_DECAY, d, dtype=F32)).reshape(1, d)
    ins = [f_w1[0:1], f_w1[1:1 + HY_BANDS], f_w1[1 + HY_BANDS:], f_b1.reshape(1, ffn), f_w2,
           f_b2.reshape(1, ffn), f_w3, bands, deltas]
    return pl.pallas_call(
        functools.partial(_hy_filter_body, length=length, tr=tr),
        grid=(2 * length // tr,),
        in_specs=[pl.BlockSpec(a.shape, lambda i: (0, 0)) for a in ins],
        out_specs=pl.BlockSpec((tr, d), lambda i: (i, 0)),
        out_shape=jax.ShapeDtypeStruct((2 * length, d), F32),
        compiler_params=_cparams("parallel"),
        name="hy_filter",
    )(*ins)


def _hyena(x, scale, shift, rows_per_group, length, p, j):
    m, d = x.shape
    assert m == 2 * length
    u = _mm(x, p["hy_in_w"][j].astype(BF16), bias=p["hy_in_b"][j], mod=(scale, shift),
            rows_per_group=rows_per_group)
    post = lambda x0, x1, v: (x0, v * x1)
    x0c, z = _conv3(u, p["hy_conv"][j], [0, d, 2 * d], 512, length, post, 2, name="hy_conv")
    filt = _hy_filter(length, p["hy_f_w1"][j], p["hy_f_b1"][j], p["hy_f_w2"][j], p["hy_f_b2"][j],
                      p["hy_f_w3"][j])
    n = 2 * length
    n1, n2 = (n, 1) if n <= 1024 else _split_len(n)
    hspec = _fft_real_spectrum(filt, n1, n2)
    if n2 == 1:
        z2 = jnp.pad(z.reshape(2, length, d), ((0, 0), (0, length), (0, 0)))
        y = _fft_conv_pair_single(z2, hspec, n)[:, :length]
    else:
        y = _fft_conv_pair(z.reshape(2, length, d), hspec, n1, n2)
    fn = lambda yt, zt, x0t, sk: ((yt + sk * zt) * x0t,)
    g = _rowwise(fn, [y.reshape(m, d), z, x0c], [p["hy_skip"][j].reshape(1, 1, d)], [d], name="hy_gate")[0]
    return _mm(g, p["hy_out_w"][j].astype(BF16), bias=p["hy_out_b"][j])


def _fft_conv_pair_single(z2, hspec, n, tcol=512):
    c = z2.shape[-1]
    w1 = _cplx_mat(_inner_phase(n, -1.0)).astype(BF16)
    w2 = (_cplx_mat(_inner_phase(n, 1.0)) * (1.0 / n)).astype(BF16)
    tcol = min(tcol, c)
    return pl.pallas_call(
        _spec_mul_body,
        grid=(c // tcol,),
        in_specs=[pl.BlockSpec((2 * n, 2 * n), lambda j: (0, 0)),
                  pl.BlockSpec((2 * n, 2 * n), lambda j: (0, 0)),
                  pl.BlockSpec((2, n, tcol), lambda j: (0, 0, j)),
                  pl.BlockSpec((2, n, tcol), lambda j: (0, 0, j))],
        out_specs=pl.BlockSpec((2, n, tcol), lambda j: (0, 0, j)),
        out_shape=jax.ShapeDtypeStruct((2, n, c), F32),
        compiler_params=_cparams("parallel"),
        name="fft_mid1",
    )(w1, w2, z2, hspec)


def _fnet(x, scale, shift, rows_per_group, batch, length, p, j):
    m, d = x.shape
    gc = d // FN_GROUPS
    ph = _inner_phase(gc, -1.0)
    eye = jnp.eye(FN_GROUPS, dtype=F32)
    w_c = jnp.concatenate([jnp.kron(eye, jnp.cos(ph)), jnp.kron(eye, jnp.sin(ph))], 1).astype(BF16)
    w = _mm(x, w_c, mod=(scale, shift), rows_per_group=rows_per_group)
    if length <= 1024:
        n1, n2 = length, 1
    else:
        n1, n2 = _split_len(length)
    norm = 1.0 / math.sqrt(length * gc)
    if n2 == 1:
        ph1 = _inner_phase(n1, -1.0)
        wr = (jnp.concatenate([jnp.cos(ph1), -jnp.sin(ph1)], 1) * norm).astype(BF16)
        y = _lmm(wr, pl.BlockSpec((n1, 2 * n1), lambda b, c: (0, 0)),
                 [w.reshape(batch, n1, 2 * d)] * 2,
                 [pl.BlockSpec((None, n1, d), lambda b, c: (b, 0, 0)),
                  pl.BlockSpec((None, n1, d), lambda b, c: (b, 0, 1))],
                 (batch, n1, d), pl.BlockSpec((None, n1, d), lambda b, c: (b, 0, 0)), (batch, 1), "fn_pos1")
        y = y.reshape(m, d)
    else:
        m_in = _cplx_mat(_outer_mats(n1, n2, n2, -1.0)).astype(BF16)
        wv = w.reshape(batch, n2, n1 * 2 * d)
        a = _lmm(m_in, pl.BlockSpec((1, 2 * n2, 2 * n2), lambda b, s: (s, 0, 0)),
                 [wv, wv],
                 [pl.BlockSpec((None, n2, d), lambda b, s: (b, 0, 2 * s)),
                  pl.BlockSpec((None, n2, d), lambda b, s: (b, 0, 2 * s + 1))],
                 (batch, 2, n1, n2, d), pl.BlockSpec((None, 2, 1, n2, d), lambda b, s: (b, 0, s, 0, 0)),
                 (batch, n1), "fn_pos_in")
        ph1 = _inner_phase(n1, -1.0)
        wr = (jnp.concatenate([jnp.cos(ph1), -jnp.sin(ph1)], 1) * norm).astype(BF16)
        cols = n2 * d
        tcol = 1024
        y = _lmm(wr, pl.BlockSpec((n1, 2 * n1), lambda b, c: (0, 0)),
                 [a.reshape(batch, 2, n1, cols)],
                 [pl.BlockSpec((None, 2, n1, tcol), lambda b, c: (b, 0, 0, c))],
                 (batch, n1, cols), pl.BlockSpec((None, n1, tcol), lambda b, c: (b, 0, c)),
                 (batch, cols // tcol), "fn_pos_mid")
        y = y.reshape(m, d)
    return _mm(y, p["fn_out_w"][j].astype(BF16), bias=p["fn_out_b"][j])


def _head_l2(t, extra):
    outs = []
    for h in range(t.shape[1] // GD_DK):
        th = t[:, h * GD_DK:(h + 1) * GD_DK]
        outs.append(th * (lax.rsqrt(jnp.sum(th * th, -1, keepdims=True) + 1e-6) * extra))
    return jnp.concatenate(outs, -1)


def _silu(v):
    return v * jax.nn.sigmoid(v)


def _gdn_scan_body(q_ref, k_ref, v_ref, gb_ref, gbt_ref, s0_ref, o_ref, sfin_ref, s_scr, *, n_chunks):
    direction = pl.program_id(0)
    c = pl.program_id(2)
    cs = GD_CHUNK

    @pl.when(c == 0)
    def _():
        s_scr[...] = s0_ref[...]

    ri = lax.broadcasted_iota(jnp.int32, (cs, cs), 0)
    ci = lax.broadcasted_iota(jnp.int32, (cs, cs), 1)
    lag = (ri - ci) * (1 - 2 * direction)
    incl = lag >= 0
    strict = lag > 0
    tri = incl.astype(F32)
    tri_t = (lag <= 0).astype(F32)
    eye = (ri == ci).astype(F32)
    pair_masks = []
    for lvl in range(int(math.log2(cs))):
        rb, cb = lax.shift_right_logical(ri, lvl), lax.shift_right_logical(ci, lvl)
        pair_masks.append((jnp.abs(rb - cb) == 1) & ((jnp.minimum(rb, cb) & 1) == 0))
    gb = gb_ref[...]
    gbt = gbt_ref[...]
    gc_cols = _dot_hi(tri, gb)
    gc_rows = _dot_hi(gbt, tri_t)
    tot = jnp.sum(gb, 0, keepdims=True)
    outs = []
    for h in range(GD_HEADS):
        sl = slice(h * GD_DK, (h + 1) * GD_DK)
        q, k, v = q_ref[:, sl], k_ref[:, sl], v_ref[:, sl]
        gcol = gc_cols[:, h:h + 1]
        grow = gc_rows[h:h + 1, :]
        beta = gb[:, GD_HEADS + h:GD_HEADS + h + 1]
        gtot = tot[:, h:h + 1]
        decay = jnp.exp(jnp.where(incl, gcol - grow, -jnp.inf))
        eg = jnp.exp(gcol)
        kb = k * beta
        kbf = k.astype(BF16)
        kk = lax.dot_general(kb.astype(BF16), kbf, (((1,), (1,)), ((), ())), preferred_element_type=F32)
        a = jnp.where(strict, kk * decay, 0.0)
        inv = eye - jnp.where(pair_masks[0], a, 0.0)
        for pm in pair_masks[1:]:
            inv = inv - _dot3(_dot3(inv, jnp.where(pm, a, 0.0)), inv)
        rhs = jnp.concatenate([v * beta, kb * eg], -1)
        sol = _dot3(inv, rhs)
        u_c, w_c = sol[:, :GD_DK], sol[:, GD_DK:]
        kd = k * jnp.exp(gtot - gcol)
        qk = lax.dot_general(q.astype(BF16), kbf, (((1,), (1,)), ((), ())), preferred_element_type=F32)
        attn = qk * decay
        s = s_scr[h]
        sb = s.astype(BF16)
        v_new = u_c - _dot(w_c.astype(BF16), sb)
        vnb = v_new.astype(BF16)
        o = _dot((q * eg).astype(BF16), sb) + _dot(attn.astype(BF16), vnb)
        s_scr[h] = s * jnp.exp(gtot) + lax.dot_general(kd.astype(BF16), vnb, (((0,), (0,)), ((), ())),
                                                       preferred_element_type=F32)
        outs.append(o)
    o_ref[...] = jnp.concatenate(outs, -1)

    @pl.when(c == n_chunks - 1)
    def _():
        sfin_ref[...] = s_scr[...]


def _gdn_scan(q, k, v, gb, gbt, s0):
    b, length, w = q.shape
    n_chunks = length // GD_CHUNK
    cidx = lambda d, c: c + d * (n_chunks - 1 - 2 * c)
    seq = lambda d, bi, c: (bi, cidx(d, c), 0)
    return pl.pallas_call(
        functools.partial(_gdn_scan_body, n_chunks=n_chunks),
        grid=(2, b, n_chunks),
        in_specs=[pl.BlockSpec((None, GD_CHUNK, w), seq),
                  pl.BlockSpec((None, GD_CHUNK, w), seq),
                  pl.BlockSpec((None, GD_CHUNK, w), seq),
                  pl.BlockSpec((None, None, GD_CHUNK, 128), lambda d, bi, c: (d, bi, cidx(d, c), 0)),
                  pl.BlockSpec((None, None, None, 16, GD_CHUNK), lambda d, bi, c: (d, bi, cidx(d, c), 0, 0)),
                  pl.BlockSpec((None, None, GD_HEADS, GD_DK, GD_DK), lambda d, bi, c: (d, bi, 0, 0, 0))],
        out_specs=[pl.BlockSpec((None, None, GD_CHUNK, w), lambda d, bi, c: (d, bi, cidx(d, c), 0)),
                   pl.BlockSpec((None, None, GD_HEADS, GD_DK, GD_DK), lambda d, bi, c: (d, bi, 0, 0, 0))],
        out_shape=[jax.ShapeDtypeStruct((2, b, length, w), F32),
                   jax.ShapeDtypeStruct((2, b, GD_HEADS, GD_DK, GD_DK), F32)],
        scratch_shapes=[pltpu.VMEM((GD_HEADS, GD_DK, GD_DK), F32)],
        compiler_params=_cparams("parallel", "parallel", "arbitrary"),
        name="gdn_scan",
    )(q, k, v, gb, gbt, s0)


def _gdn_inputs(x, scale, shift, rows_per_group, batch, length, p, j):
    m, d = x.shape
    wd = GD_HEADS * GD_DK
    in_w = p["gd_in_w"][j]
    u = _mm(x, in_w[:, :4 * wd].astype(BF16), mod=(scale, shift), rows_per_group=rows_per_group)
    w_ab = jnp.pad(in_w[:, 4 * wd:], ((0, 0), (0, 128 - 4 * GD_HEADS))).astype(BF16)
    ab = _mm(x, w_ab, mod=(scale, shift), rows_per_group=rows_per_group)
    qscale = GD_DK ** -0.5
    post = lambda qc, kc, vc: (_head_l2(_silu(qc), qscale), _head_l2(_silu(kc), 1.0), _silu(vc))
    q, k, v = _conv3(u, p["gd_conv"][j], [0, wd, 2 * wd], 512, length, post, 3, name="gd_conv")
    nh = GD_HEADS
    a_par = jnp.zeros((1, 1, 128), F32).at[0, 0, :2 * nh].set(-jnp.exp(p["gd_a_log"][j]).reshape(-1))
    dt_par = jnp.zeros((1, 1, 128), F32).at[0, 0, :2 * nh].set(p["gd_dt_bias"][j].reshape(-1))

    def gate_fn(abt, an, dtb):
        pre = abt + dtb
        sp = jnp.maximum(pre, 0.0) + jnp.log(1.0 + jnp.exp(-jnp.abs(pre)))
        lane = lax.broadcasted_iota(jnp.int32, abt.shape, 1)
        return (jnp.where(lane < 2 * nh, an * sp, jax.nn.sigmoid(abt)),)

    gall = _rowwise(gate_fn, [ab], [a_par, dt_par], [128], name="gd_gate")[0]
    pad = jnp.zeros((m, 128 - 2 * nh), F32)
    gb = jnp.stack([jnp.concatenate([gall[:, dr * nh:(dr + 1) * nh],
                                     gall[:, (2 + dr) * nh:(3 + dr) * nh], pad], -1) for dr in range(2)])
    gb = gb.reshape(2, batch, length, 128)
    gbt = jnp.swapaxes(gb[..., :2 * nh].reshape(2, batch, length // GD_CHUNK, GD_CHUNK, 2 * nh), -1, -2)
    rs = lambda t: t.reshape(batch, length, wd)
    return rs(q), rs(k), rs(v), gb, gbt, u


def _gdn_out(o2, u, norm_g, out_w):
    m, wd = o2.shape[1], o2.shape[2]
    ng = jnp.tile(norm_g, wd // norm_g.shape[0]).reshape(1, 1, wd)

    def body(of_ref, ob_ref, z_ref, ng_ref, o_ref):
        o = of_ref[...] + ob_ref[...]
        outs = []
        for h in range(wd // GD_DK):
            oh = o[:, h * GD_DK:(h + 1) * GD_DK]
            outs.append(oh * lax.rsqrt(jnp.mean(oh * oh, -1, keepdims=True) + 1e-6))
        o_ref[...] = jnp.concatenate(outs, -1) * ng_ref[...] * _silu(z_ref[...])

    tm = math.gcd(512, m)
    g = pl.pallas_call(
        body,
        grid=(m // tm,),
        in_specs=[pl.BlockSpec((None, tm, wd), lambda i: (0, i, 0)),
                  pl.BlockSpec((None, tm, wd), lambda i: (1, i, 0)),
                  pl.BlockSpec((tm, wd), lambda i: (i, 3)),
                  pl.BlockSpec((None, 1, wd), lambda i: (0, 0, 0))],
        out_specs=pl.BlockSpec((tm, wd), lambda i: (i, 0)),
        out_shape=jax.ShapeDtypeStruct((m, wd), F32),
        compiler_params=_cparams("parallel"),
        name="gd_norm",
    )(o2, o2, u, ng)
    return _mm(g, out_w.astype(BF16))


def _topk_rows(s, n_take, val_ref, idx_ref, base):
    r = s.shape[0]
    rid = lax.broadcasted_iota(jnp.int32, s.shape, 0)
    for t in range(n_take):
        mx = jnp.max(s, 0, keepdims=True)
        am = jnp.min(jnp.where(s == mx, rid, r), 0, keepdims=True)
        val_ref[base + t:base + t + 1, :] = mx
        idx_ref[base + t:base + t + 1, :] = am
        s = jnp.where(rid == am, -jnp.inf, s)


def _pk_topk_body(q_ref, keys_ref, idx_ref, gate_ref, sv_scr, si_scr, cv_scr, ci_scr):
    kk = PK_TOPK
    tt = q_ref.shape[0]
    for h in range(PK_HEADS):
        for p in range(2):
            qh = q_ref[:, (2 * h + p) * PK_DH:(2 * h + p + 1) * PK_DH].astype(BF16)
            st = lax.dot_general(keys_ref[h, p], qh, (((1,), (1,)), ((), ())),
                                 preferred_element_type=F32)
            _topk_rows(st, kk, sv_scr, si_scr, p * kk)
        sv1, sv2 = sv_scr[0:kk, :], sv_scr[kk:2 * kk, :]
        si1, si2 = si_scr[0:kk, :], si_scr[kk:2 * kk, :]
        cand = jnp.concatenate([sv1[i:i + 1, :] + sv2 for i in range(kk)], 0)
        _topk_rows(cand, kk, cv_scr, ci_scr, 0)
        cv, ci = cv_scr[...], ci_scr[...]
        a1, a2 = lax.shift_right_logical(ci, int(math.log2(kk))), ci & (kk - 1)
        i1 = jnp.zeros((kk, tt), jnp.int32)
        i2 = jnp.zeros((kk, tt), jnp.int32)
        for r in range(kk):
            i1 = i1 + jnp.where(a1 == r, si1[r:r + 1, :], 0)
            i2 = i2 + jnp.where(a2 == r, si2[r:r + 1, :], 0)
        idx_ref[h * kk:(h + 1) * kk, :] = i1 * PK_NKEYS + i2
        e = jnp.exp(cv - jnp.max(cv, 0, keepdims=True))
        gate_ref[h * kk:(h + 1) * kk, :] = e / jnp.sum(e, 0, keepdims=True)


def _pk_topk(q, keys_bf16):
    m = q.shape[0]
    tt = PK_TOK
    nb = m // tt
    hk = PK_HEADS * PK_TOPK
    return pl.pallas_call(
        _pk_topk_body,
        grid=(nb,),
        in_specs=[pl.BlockSpec((tt, q.shape[1]), lambda i: (i, 0)),
                  pl.BlockSpec(keys_bf16.shape, lambda i: (0, 0, 0, 0))],
        out_specs=[pl.BlockSpec((None, hk, tt), lambda i: (i, 0, 0)),
                   pl.BlockSpec((None, hk, tt), lambda i: (i, 0, 0))],
        out_shape=[jax.ShapeDtypeStruct((nb, hk, tt), jnp.int32),
                   jax.ShapeDtypeStruct((nb, hk, tt), F32)],
        scratch_shapes=[pltpu.VMEM((2 * PK_TOPK, tt), F32), pltpu.VMEM((2 * PK_TOPK, tt), jnp.int32),
                        pltpu.VMEM((PK_TOPK, tt), F32), pltpu.VMEM((PK_TOPK, tt), jnp.int32)],
        compiler_params=_cparams("parallel"),
        name="pk_topk",
    )(q, keys_bf16)


def _pack_rows(tab):
    half = tab.shape[1] // 2
    bits = lax.bitcast_convert_type(tab.astype(BF16), jnp.uint16).astype(jnp.uint32)
    return lax.bitcast_convert_type(bits[:, :half] | (bits[:, half:] << 16), jnp.int32)


def _sc_gather2(tab_u, tab_v, idx):
    n = idx.shape[0]
    wcols = tab_u.shape[1]
    per_w = n // SC_WORKERS
    nblk = per_w // SC_IDX_TILE
    assert nblk * SC_IDX_TILE * SC_WORKERS == n
    nsub = SC_IDX_TILE // SC_ROWS
    idx3 = idx.reshape(SC_WORKERS, nblk, SC_IDX_TILE)
    mesh = plsc.VectorSubcoreMesh(core_axis_name="core", subcore_axis_name="subcore")
    out_t = (jax.ShapeDtypeStruct((n, wcols), jnp.int32), jax.ShapeDtypeStruct((n, wcols), jnp.int32))
    buf = pltpu.VMEM((SC_ROWS, wcols), jnp.int32)

    @functools.partial(pl.kernel, out_type=out_t, mesh=mesh,
                       scratch_types=[pltpu.VMEM((nblk, SC_IDX_TILE), jnp.int32), buf, buf, buf, buf,
                                      pltpu.SemaphoreType.DMA, pltpu.SemaphoreType.DMA,
                                      pltpu.SemaphoreType.DMA, pltpu.SemaphoreType.DMA])
    def k(u_hbm, v_hbm, i_hbm, ou_hbm, ov_hbm, i_v, bu0, bv0, bu1, bv1, g0, g1, w0, w1):
        wid = lax.axis_index("subcore") * 2 + lax.axis_index("core")
        base = wid * per_w
        pltpu.sync_copy(i_hbm.at[wid], i_v)
        sets = ((bu0, bv0, g0, w0), (bu1, bv1, g1, w1))

        def gathers(b, s, st):
            ii = i_v.at[b, pl.ds(s * SC_ROWS, SC_ROWS)]
            return (pltpu.make_async_copy(u_hbm.at[ii], st[0], st[2]),
                    pltpu.make_async_copy(v_hbm.at[ii], st[1], st[2]))

        def writes(b, s, st):
            off = base + b * SC_IDX_TILE + s * SC_ROWS
            return (pltpu.make_async_copy(st[0], ou_hbm.at[pl.ds(off, SC_ROWS)], st[3]),
                    pltpu.make_async_copy(st[1], ov_hbm.at[pl.ds(off, SC_ROWS)], st[3]))

        def start(cs):
            for c in cs:
                c.start()

        def wait(cs):
            for c in cs:
                c.wait()

        start(gathers(0, 0, sets[0]))

        @pl.loop(0, nblk)
        def _(b):
            for s in range(nsub):
                st, ot = sets[s % 2], sets[(s + 1) % 2]
                if s >= 1:
                    wait(writes(b, s - 1, ot))
                else:
                    @pl.when(b > 0)
                    def _():
                        wait(writes(b - 1, nsub - 1, ot))
                if s + 1 < nsub:
                    start(gathers(b, s + 1, ot))
                else:
                    @pl.when(b + 1 < nblk)
                    def _():
                        start(gathers(b + 1, 0, ot))
                wait(gathers(b, s, st))
                start(writes(b, s, st))

        wait(writes(nblk - 1, nsub - 1, sets[(nsub - 1) % 2]))

    return k(tab_u, tab_v, idx3)


def _unpack(wd):
    lo = lax.bitcast_convert_type(lax.shift_left(wd, 16), F32)
    hi = lax.bitcast_convert_type(wd & jnp.int32(-65536), F32)
    return lo, hi


def _pk_apply_body(x_ref, sc_ref, sh_ref, gt_ref, gate_ref, lg_ref, lb_ref, u_ref, v_ref, o_ref, acc_ref,
                   *, alpha):
    h = pl.program_id(1)
    kk = PK_TOPK
    tt, d = x_ref.shape
    half = d // 2
    rs = 32

    @pl.when(h == 0)
    def _():
        acc_ref[...] = jnp.zeros_like(acc_ref)

    lane = lax.broadcasted_iota(jnp.int32, (rs, kk), 1)
    for r0 in range(0, tt, rs):
        rows = slice(r0, r0 + rs)
        hb = x_ref[rows, :] * (1.0 + sc_ref[...]) + sh_ref[...]
        hb_lo, hb_hi = hb[:, :half], hb[:, half:]
        act = jnp.zeros((rs, kk), F32)
        for k in range(kk):
            lo, hi = _unpack(u_ref[k, rows, :])
            s = jnp.sum(lo * hb_lo + hi * hb_hi, -1, keepdims=True)
            act = jnp.where(lane == k, s, act)
        wgt = jax.nn.gelu(act) * gate_ref[rows, :]
        acc_lo = acc_ref[rows, :half]
        acc_hi = acc_ref[rows, half:]
        for k in range(kk):
            lo, hi = _unpack(v_ref[k, rows, :])
            wk = wgt[:, k:k + 1]
            acc_lo = acc_lo + wk * lo
            acc_hi = acc_hi + wk * hi
        acc_ref[rows, :half] = acc_lo
        acc_ref[rows, half:] = acc_hi

    @pl.when(h == PK_HEADS - 1)
    def _():
        o_ref[...] = _ln(alpha * x_ref[...] + gt_ref[...] * acc_ref[...], lg_ref[...], lb_ref[...])


def _pk_apply(x, scale, shift, gate_vec, ln_g, ln_b, gate_t, u_rows, v_rows, rows_per_group, alpha, blk0):
    d = x.shape[1]
    nb = u_rows.shape[0]
    tt = PK_TOK
    g = rows_per_group // tt
    xb = lambda i, h: (blk0 + i, 0)
    vec = lambda i, h: ((blk0 + i) // g, 0, 0)
    one = lambda i, h: (0, 0, 0)
    return pl.pallas_call(
        functools.partial(_pk_apply_body, alpha=alpha),
        grid=(nb, PK_HEADS),
        in_specs=[pl.BlockSpec((tt, d), xb),
                  pl.BlockSpec((None, 1, d), vec), pl.BlockSpec((None, 1, d), vec),
                  pl.BlockSpec((None, 1, d), vec),
                  pl.BlockSpec((None, None, tt, PK_TOPK), lambda i, h: (blk0 + i, h, 0, 0)),
                  pl.BlockSpec((None, 1, d), one), pl.BlockSpec((None, 1, d), one),
                  pl.BlockSpec((None, None, PK_TOPK, tt, d // 2), lambda i, h: (i, h, 0, 0, 0)),
                  pl.BlockSpec((None, None, PK_TOPK, tt, d // 2), lambda i, h: (i, h, 0, 0, 0))],
        out_specs=pl.BlockSpec((tt, d), lambda i, h: (i, 0)),
        out_shape=jax.ShapeDtypeStruct((nb * tt, d), F32),
        scratch_shapes=[pltpu.VMEM((tt, d), F32)],
        compiler_params=_cparams("parallel", "arbitrary"),
        name="pk_apply",
    )(x, scale, shift, gate_vec, gate_t, ln_g, ln_b, u_rows, v_rows)


def _gather_rows(tab_u, tab_v, idx):
    return _sc_gather2(tab_u, tab_v, idx)


def _peer_ln(x, scale, shift, gate_vec, ln_g, ln_b, rows_per_group, alpha, wq_bf16, keys_bf16, u_pack, v_pack,
             chunk_tokens=4096):
    m, d = x.shape
    q = _mm(x, wq_bf16, mod=(scale, shift), rows_per_group=rows_per_group)
    idx, gate = _pk_topk(q, keys_bf16)
    nb_all = m // PK_TOK
    gate_t = jnp.swapaxes(gate.reshape(nb_all, PK_HEADS, PK_TOPK, PK_TOK), -1, -2)
    chunk_tokens = min(chunk_tokens, m)
    cb = chunk_tokens // PK_TOK
    outs = []
    for c0 in range(0, nb_all, cb):
        ids = idx[c0:c0 + cb].reshape(-1)
        u_rows, v_rows = _gather_rows(u_pack, v_pack, ids)
        shp = (cb, PK_HEADS, PK_TOPK, PK_TOK, d // 2)
        outs.append(_pk_apply(x, scale, shift, gate_vec, ln_g, ln_b, gate_t, u_rows.reshape(shp),
                              v_rows.reshape(shp), rows_per_group, alpha, c0))
    return outs[0] if len(outs) == 1 else jnp.concatenate(outs, 0)


def kernel(x, c, ctx, c_ctx, ada_w, ada_b, ln_g, ln_b, pk_wq, pk_keys, pk_u, pk_v, hy_in_w, hy_in_b, hy_conv, hy_f_w1, hy_f_b1, hy_f_w2, hy_f_b2, hy_f_w3, hy_skip, hy_out_w, hy_out_b, gd_in_w, gd_conv, gd_a_log, gd_dt_bias, gd_norm_g, gd_out_w, fn_out_w, fn_out_b):
    p = dict(hy_in_w=hy_in_w, hy_in_b=hy_in_b, hy_conv=hy_conv, hy_f_w1=hy_f_w1, hy_f_b1=hy_f_b1,
             hy_f_w2=hy_f_w2, hy_f_b2=hy_f_b2, hy_f_w3=hy_f_w3, hy_skip=hy_skip, hy_out_w=hy_out_w,
             hy_out_b=hy_out_b, gd_in_w=gd_in_w, gd_conv=gd_conv, gd_a_log=gd_a_log, gd_dt_bias=gd_dt_bias,
             fn_out_w=fn_out_w, fn_out_b=fn_out_b)
    b, length, d = x.shape
    lc = ctx.shape[1]
    depth = ada_w.shape[0]
    alpha = (2 * depth) ** 0.25
    xl = _pos_add(x)
    xc = ctx.reshape(b * lc, d)
    gdn_layers = [i for i in range(depth) if i % N_MIXERS == 1]
    ctx_until = gdn_layers[-1] if gdn_layers else -1
    cond = jnp.concatenate([c, c_ctx[None], jnp.zeros((8 - b - 1, d), F32)], 0)
    for i in range(depth):
        kind, j = i % N_MIXERS, i // N_MIXERS
        ctx_in, ctx_out = i <= ctx_until, i < ctx_until
        mod = _mm(cond, ada_w[i].astype(BF16), bias=ada_b[i], silu_in=True, tn=2048)
        mod = mod.reshape(8, N_MOD, 1, d)
        ml = [mod[:b, t] for t in range(N_MOD)]
        mc = [mod[b:b + 1, t] for t in range(N_MOD)]
        lg = [ln_g[i, t].reshape(1, 1, d) for t in range(2)]
        lb = [ln_b[i, t].reshape(1, 1, d) for t in range(2)]
        yc = None
        if kind == 0:
            yl = _hyena(xl, ml[1], ml[0], length, length, p, j)
            if ctx_out:
                yc = _hyena(xc, mc[1], mc[0], b * lc, lc, p, j)
        elif kind == 1:
            qc, kc, vc, gbc, gbtc, uc = _gdn_inputs(xc, mc[1], mc[0], b * lc, b, lc, p, j)
            ql, kl, vl, gbl, gbtl, ul = _gdn_inputs(xl, ml[1], ml[0], length, b, length, p, j)
            s0 = jnp.zeros((2, b, GD_HEADS, GD_DK, GD_DK), F32)
            oc, s_ctx = _gdn_scan(qc, kc, vc, gbc, gbtc, s0)
            ol, _ = _gdn_scan(ql, kl, vl, gbl, gbtl, s_ctx)
            yl = _gdn_out(ol.reshape(2, b * length, -1), ul, gd_norm_g[j], gd_out_w[j])
            if ctx_out:
                yc = _gdn_out(oc.reshape(2, b * lc, -1), uc, gd_norm_g[j], gd_out_w[j])
        else:
            yl = _fnet(xl, ml[1], ml[0], length, b, length, p, j)
            if ctx_out:
                yc = _fnet(xc, mc[1], mc[0], b * lc, b, lc, p, j)
        wq = pk_wq[i].astype(BF16)
        keys = pk_keys[i].astype(BF16)
        u_pack, v_pack = _pack_rows(pk_u[i]), _pack_rows(pk_v[i])
        xl = _res_ln(xl, yl, ml[2], lg[0], lb[0], alpha, length)
        xl = _peer_ln(xl, ml[4], ml[3], ml[5], lg[1], lb[1], length, alpha, wq, keys, u_pack, v_pack)
        if ctx_out:
            xc = _res_ln(xc, yc, mc[2], lg[0], lb[0], alpha, b * lc)
            xc = _peer_ln(xc, mc[4], mc[3], mc[5], lg[1], lb[1], b * lc, alpha, wq, keys, u_pack, v_pack)
    return xl.reshape(b, length, d)
```

```python
import functools
import math

import numpy as np
import jax
import jax.numpy as jnp
from jax import lax
from jax.experimental import pallas as pl
from jax.experimental.pallas import tpu as pltpu

F32 = jnp.float32
BF16 = jnp.bfloat16

GRID_W = 64
N_MIXERS = 3
N_MOD = 6
LN_EPS = 1e-5
HY_EMB = 33
HY_BANDS = (HY_EMB - 1) // 2
HY_SHIFT = 0.05
HY_TARGET = 1e-2
HY_MIN_DECAY = math.log(HY_TARGET) / 1.5
HY_MAX_DECAY = math.log(HY_TARGET) / 0.3
GD_HEADS = 8
GD_DK = 128
GD_CHUNK = 64
FN_GROUPS = 4
PK_HEADS = 8
PK_NKEYS = 128
PK_DH = 128
PK_TOPK = 16
PK_TOK = 128
PK_TM = 256
PK_EC = 1024

VMEM_LIMIT_BYTES = 56 * 1024 * 1024


def _cparams(*sem):
    return pltpu.CompilerParams(dimension_semantics=sem, vmem_limit_bytes=VMEM_LIMIT_BYTES)


def _dot(a, b):
    return jnp.dot(a, b, preferred_element_type=F32)


def _dot_hi(a, b):
    return jnp.dot(a, b, preferred_element_type=F32, precision=lax.Precision.HIGHEST)


def _split(a):
    hi = a.astype(BF16)
    lo = (a - hi.astype(F32)).astype(BF16)
    return hi, lo


def _dot3(a, b):
    ah, al = _split(a)
    bh, bl = _split(b)
    return _dot(ah, bh) + (_dot(ah, bl) + _dot(al, bh))


def _mm_body(*refs, has_mod, has_bias, silu_in):
    a_ref, w_ref = refs[0], refs[1]
    k = 2
    a = a_ref[...]
    if has_mod:
        a = a * (1.0 + refs[k][...]) + refs[k + 1][...]
        k += 2
    if silu_in:
        a = a * jax.nn.sigmoid(a)
    o = _dot(a.astype(BF16), w_ref[...])
    if has_bias:
        o = o + refs[k][...]
        k += 1
    refs[k][...] = o


def _mm(a, w_bf16, bias=None, mod=None, rows_per_group=None, silu_in=False, tm=512, tn=None):
    m, k = a.shape
    n = w_bf16.shape[1]
    tm = math.gcd(tm, m, rows_per_group or m)
    if tn is None:
        tn = max(t for t in range(128, min(n, 2048) + 1, 128) if n % t == 0)
    assert m % tm == 0 and n % tn == 0
    ins = [a, w_bf16]
    specs = [pl.BlockSpec((tm, k), lambda j, i: (i, 0)), pl.BlockSpec((k, tn), lambda j, i: (0, j))]
    if mod is not None:
        assert rows_per_group % tm == 0
        g = rows_per_group // tm
        for v in mod:
            ins.append(v)
            specs.append(pl.BlockSpec((None, 1, k), lambda j, i: (i // g, 0, 0)))
    if bias is not None:
        ins.append(bias.reshape(1, n))
        specs.append(pl.BlockSpec((1, tn), lambda j, i: (0, j)))
    return pl.pallas_call(
        functools.partial(_mm_body, has_mod=mod is not None, has_bias=bias is not None, silu_in=silu_in),
        grid=(n // tn, m // tm),
        in_specs=specs,
        out_specs=pl.BlockSpec((tm, tn), lambda j, i: (i, j)),
        out_shape=jax.ShapeDtypeStruct((m, n), F32),
        compiler_params=_cparams("parallel", "parallel"),
        name="mm",
    )(*ins)


def _rowwise(fn, rows, vecs, out_cols, rows_per_group=None, tm=512, name="rowwise"):
    m = rows[0].shape[0]
    tm = math.gcd(tm, m, rows_per_group or m)
    assert m % tm == 0
    n_r, n_v, n_o = len(rows), len(vecs), len(out_cols)

    def body(*refs):
        outs = fn(*[r[...] for r in refs[:n_r + n_v]])
        for o_ref, o in zip(refs[n_r + n_v:], outs):
            o_ref[...] = o

    specs = [pl.BlockSpec((tm, r.shape[1]), lambda i: (i, 0)) for r in rows]
    for v in vecs:
        if v.shape[0] == 1:
            specs.append(pl.BlockSpec((None, 1, v.shape[2]), lambda i: (0, 0, 0)))
        else:
            assert rows_per_group % tm == 0
            g = rows_per_group // tm
            specs.append(pl.BlockSpec((None, 1, v.shape[2]), lambda i, g=g: (i // g, 0, 0)))
    return pl.pallas_call(
        body,
        grid=(m // tm,),
        in_specs=specs,
        out_specs=[pl.BlockSpec((tm, c), lambda i: (i, 0)) for c in out_cols],
        out_shape=[jax.ShapeDtypeStruct((m, c), F32) for c in out_cols],
        compiler_params=_cparams("parallel"),
        name=name,
    )(*rows, *vecs)


def _ln(v, g, b):
    mu = jnp.mean(v, -1, keepdims=True)
    d = v - mu
    var = jnp.mean(d * d, -1, keepdims=True)
    return d * lax.rsqrt(var + LN_EPS) * g + b


def _res_ln(x, y, gate, ln_g, ln_b, alpha, rows_per_group):
    fn = lambda xt, yt, gt, lg, lb: (_ln(alpha * xt + gt * yt, lg, lb),)
    return _rowwise(fn, [x, y], [gate, ln_g, ln_b], [x.shape[1]], rows_per_group, name="res_ln")[0]


def _pos_add_body(x_ref, er_ref, ec_ref, o_ref):
    half = er_ref.shape[-1]
    x = x_ref[...]
    er = jnp.broadcast_to(er_ref[...], x.shape[:2] + (half,))
    ec = jnp.broadcast_to(ec_ref[...][None], x.shape[:2] + (half,))
    o_ref[...] = x + jnp.concatenate([er, ec], -1)


def _pos_add(x):
    b, length, d = x.shape
    rows = length // GRID_W
    quarter = d // 4
    omega = 1.0 / (10000.0 ** (jnp.arange(quarter, dtype=F32) / quarter))
    er = jnp.arange(rows, dtype=F32)[:, None] * omega
    ec = jnp.arange(GRID_W, dtype=F32)[:, None] * omega
    emb_r = jnp.concatenate([jnp.sin(er), jnp.cos(er)], -1).reshape(rows, 1, d // 2)
    emb_c = jnp.concatenate([jnp.sin(ec), jnp.cos(ec)], -1)
    rt = 8
    x4 = x.reshape(b, rows, GRID_W, d)
    out = pl.pallas_call(
        _pos_add_body,
        grid=(b, rows // rt),
        in_specs=[pl.BlockSpec((None, rt, GRID_W, d), lambda i, j: (i, j, 0, 0)),
                  pl.BlockSpec((rt, 1, d // 2), lambda i, j: (j, 0, 0)),
                  pl.BlockSpec((GRID_W, d // 2), lambda i, j: (0, 0))],
        out_specs=pl.BlockSpec((None, rt, GRID_W, d), lambda i, j: (i, j, 0, 0)),
        out_shape=jax.ShapeDtypeStruct(x4.shape, F32),
        compiler_params=_cparams("parallel", "parallel"),
        name="pos_add",
    )(x4, emb_r, emb_c)
    return out.reshape(b * length, d)


def _conv3_body(*refs, n_parts, tm, seq_len, post, n_vec):
    i = pl.program_id(0)
    first = (i * tm) % seq_len == 0
    last = ((i + 1) * tm) % seq_len == 0
    row = lax.broadcasted_iota(jnp.int32, (tm, 1), 0)
    parts = []
    for p in range(n_parts):
        main_ref, prev_ref, next_ref, w_ref = refs[4 * p:4 * p + 4]
        u = main_ref[...]
        w = w_ref[...]
        prev = jnp.where(first, 0.0, prev_ref[7:8, :])
        nxt = jnp.where(last, 0.0, next_ref[0:1, :])
        up = jnp.where(row == 0, prev, pltpu.roll(u, 1, 0))
        dn = jnp.where(row == tm - 1, nxt, pltpu.roll(u, tm - 1, 0))
        parts.append(up * w[0:1, :] + u * w[1:2, :] + dn * w[2:3, :])
    k = 4 * n_parts
    vecs = [refs[k + j][...] for j in range(n_vec)]
    outs = post(*parts, *vecs)
    for o_ref, o in zip(refs[k + n_vec:], outs):
        o_ref[...] = o


def _conv3(u, w, col_parts, tc, seq_len, post, n_out, vecs=(), tm=256, name="conv3"):
    m = u.shape[0]
    tm = min(tm, seq_len)
    assert seq_len % tm == 0 and m % tm == 0 and tm % 8 == 0
    width = col_parts[1] - col_parts[0] if len(col_parts) > 1 else tc
    ncol = width // tc
    t8 = tm // 8
    nb8 = m // 8
    ins, specs = [], []
    for c0 in col_parts:
        cb = c0 // tc
        ins += [u, u, u, w]
        specs += [
            pl.BlockSpec((tm, tc), lambda i, j, cb=cb: (i, cb + j)),
            pl.BlockSpec((8, tc), lambda i, j, cb=cb: (jnp.maximum(i * t8 - 1, 0), cb + j)),
            pl.BlockSpec((8, tc), lambda i, j, cb=cb: (jnp.minimum((i + 1) * t8, nb8 - 1), cb + j)),
            pl.BlockSpec((3, tc), lambda i, j, cb=cb: (0, cb + j)),
        ]
    for v in vecs:
        ins.append(v)
        specs.append(pl.BlockSpec((1, tc), lambda i, j: (0, j)))
    return pl.pallas_call(
        functools.partial(_conv3_body, n_parts=len(col_parts), tm=tm, seq_len=seq_len, post=post,
                          n_vec=len(vecs)),
        grid=(m // tm, ncol),
        in_specs=specs,
        out_specs=[pl.BlockSpec((tm, tc), lambda i, j: (i, j)) for _ in range(n_out)],
        out_shape=[jax.ShapeDtypeStruct((m, width), F32) for _ in range(n_out)],
        compiler_params=_cparams("parallel", "parallel"),
        name=name,
    )(*ins)


def _lmm_body(*refs, n_x):
    w_ref, o_ref = refs[0], refs[1 + n_x]
    tc = o_ref.shape[-1]
    xs = [r[...].reshape(-1, tc) for r in refs[1:1 + n_x]]
    x = xs[0] if n_x == 1 else jnp.concatenate(xs, 0)
    w = w_ref[...]
    w = w.reshape(w.shape[-2], w.shape[-1])
    o_ref[...] = _dot(w, x.astype(BF16)).reshape(o_ref.shape)


def _lmm(w, w_spec, xs, x_specs, out_shape, out_spec, grid, name):
    return pl.pallas_call(
        functools.partial(_lmm_body, n_x=len(xs)),
        grid=grid,
        in_specs=[w_spec] + list(x_specs),
        out_specs=out_spec,
        out_shape=jax.ShapeDtypeStruct(out_shape, F32),
        compiler_params=_cparams(*(["parallel"] * len(grid))),
        name=name,
    )(w, *xs)


def _spec_mul_body(w1_ref, w2_ref, x_ref, h_ref, o_ref):
    tc = o_ref.shape[-1]
    x = x_ref[...].reshape(-1, tc)
    z = _dot(w1_ref[...], x.astype(BF16))
    half = z.shape[0] // 2
    zr, zi = z[:half], z[half:]
    hr, hi = h_ref[0], h_ref[1]
    y = jnp.concatenate([zr * hr - zi * hi, zr * hi + zi * hr], 0)
    o_ref[...] = _dot(w2_ref[...], y.astype(BF16)).reshape(o_ref.shape)


def _cplx_mat(ang):
    c, s = jnp.cos(ang), jnp.sin(ang)
    return jnp.concatenate([jnp.concatenate([c, -s], -1), jnp.concatenate([s, c], -1)], -2)


def _phase(k, n):
    return (k % n).astype(F32) * (2.0 * math.pi / n)


def _outer_mats(n1, n2, n_in, sign):
    n = n1 * n2
    s1 = jnp.arange(n1, dtype=jnp.int32)[:, None, None]
    f2 = jnp.arange(n2, dtype=jnp.int32)[None, :, None]
    s2 = jnp.arange(n_in, dtype=jnp.int32)[None, None, :]
    return sign * _phase(f2 * (s1 + n1 * s2), n)


def _inner_phase(n1, sign):
    a = jnp.arange(n1, dtype=jnp.int32)
    return sign * _phase(a[:, None] * a[None, :], n1)


def _split_len(n):
    n1 = 1 << (int(math.log2(n)) // 2)
    return n1, n // n1


def _fft_conv_pair(z2, hspec, n1, n2, tcol=512):
    _, length, c = z2.shape
    n = 2 * length
    assert n1 * n2 == n
    h2 = n2 // 2
    m_in = _cplx_mat(_outer_mats(n1, n2, h2, -1.0)).astype(BF16)
    a = _lmm(m_in, pl.BlockSpec((1, 2 * n2, n2), lambda s: (s, 0, 0)),
             [z2.reshape(2, h2, n1 * c)], [pl.BlockSpec((2, h2, c), lambda s: (0, 0, s))],
             (2, n1, n2, c), pl.BlockSpec((2, 1, n2, c), lambda s: (0, s, 0, 0)), (n1,), "fft_in")
    w1 = _cplx_mat(_inner_phase(n1, -1.0)).astype(BF16)
    w2 = (_cplx_mat(_inner_phase(n1, 1.0)) * (1.0 / n)).astype(BF16)
    cols = n2 * c
    tcol = min(tcol, cols)
    b = pl.pallas_call(
        _spec_mul_body,
        grid=(cols // tcol,),
        in_specs=[pl.BlockSpec((2 * n1, 2 * n1), lambda j: (0, 0)),
                  pl.BlockSpec((2 * n1, 2 * n1), lambda j: (0, 0)),
                  pl.BlockSpec((2, n1, tcol), lambda j: (0, 0, j)),
                  pl.BlockSpec((2, n1, tcol), lambda j: (0, 0, j))],
        out_specs=pl.BlockSpec((2, n1, tcol), lambda j: (0, 0, j)),
        out_shape=jax.ShapeDtypeStruct((2, n1, cols), F32),
        compiler_params=_cparams("parallel"),
        name="fft_mid",
    )(w1, w2, a.reshape(2, n1, cols), hspec.reshape(2, n1, cols))
    if n2 == 1:
        return b.reshape(2, n, c)[:, :length]
    m_out = _cplx_mat(jnp.swapaxes(_outer_mats(n1, n2, h2, 1.0), 1, 2)).astype(BF16)
    y = _lmm(m_out, pl.BlockSpec((1, n2, 2 * n2), lambda s: (s, 0, 0)),
             [b.reshape(2, n1, n2, c)], [pl.BlockSpec((2, 1, n2, c), lambda s: (0, s, 0, 0))],
             (2, h2, n1 * c), pl.BlockSpec((2, h2, c), lambda s: (0, 0, s)), (n1,), "fft_out")
    return y.reshape(2, length, c)


def _fft_real_spectrum(f, n1, n2, tcol=512):
    n, c = f.shape
    if n2 == 1:
        ph = _inner_phase(n1, -1.0)
        w = jnp.concatenate([jnp.cos(ph), jnp.sin(ph)], 0).astype(BF16)
        tcol = min(tcol, c)
        return _lmm(w, pl.BlockSpec((2 * n1, n1), lambda j: (0, 0)),
                    [f], [pl.BlockSpec((n1, tcol), lambda j: (0, j))],
                    (2, n1, c), pl.BlockSpec((2, n1, tcol), lambda j: (0, 0, j)), (c // tcol,), "fft_spec1")
    ph = _outer_mats(n1, n2, n2, -1.0)
    m_in = jnp.concatenate([jnp.cos(ph), jnp.sin(ph)], 1).astype(BF16)
    a = _lmm(m_in, pl.BlockSpec((1, 2 * n2, n2), lambda s: (s, 0, 0)),
             [f.reshape(n2, n1 * c)], [pl.BlockSpec((n2, c), lambda s: (0, s))],
             (2, n1, n2, c), pl.BlockSpec((2, 1, n2, c), lambda s: (0, s, 0, 0)), (n1,), "fft_spec_in")
    w1 = _cplx_mat(_inner_phase(n1, -1.0)).astype(BF16)
    cols = n2 * c
    tcol = min(tcol, cols)
    h = _lmm(w1, pl.BlockSpec((2 * n1, 2 * n1), lambda j: (0, 0)),
             [a.reshape(2, n1, cols)], [pl.BlockSpec((2, n1, tcol), lambda j: (0, 0, j))],
             (2, n1, cols), pl.BlockSpec((2, n1, tcol), lambda j: (0, 0, j)), (cols // tcol,), "fft_spec_mid")
    return h.reshape(2, n, c)


def _hy_filter_body(wt_ref, wc_ref, ws_ref, b1_ref, w2_ref, b2_ref, w3_ref, bands_ref, dl_ref, o_ref,
                    *, length, tr):
    d = o_ref.shape[-1]
    j = pl.program_id(0) * tr + lax.broadcasted_iota(jnp.int32, (tr, 1), 0)
    k = jnp.where(j < length, j, 2 * length - j)
    t = k.astype(F32) / length
    ang = 2.0 * jnp.pi * t * bands_ref[...]
    pre = t * wt_ref[...] + _dot_hi(jnp.cos(ang), wc_ref[...]) + _dot_hi(-jnp.sin(ang), ws_ref[...])
    hdn = jnp.sin(pre + b1_ref[...])
    hdn = jnp.sin(_dot_hi(hdn, w2_ref[...]) + b2_ref[...])
    hf = _dot_hi(hdn, w3_ref[...])
    win = jnp.exp(-t * dl_ref[...]) + HY_SHIFT
    h = jnp.where(j < length, hf[:, :d], hf[:, d:]) * win
    o_ref[...] = jnp.where(j == length, 0.0, h)


def _hy_filter(length, f_w1, f_b1, f_w2, f_b2, f_w3):
    d = f_w3.shape[1] // 2
    ffn = f_w2.shape[0]
    tr = min(512, length)
    bands = jnp.linspace(1e-4, HY_BANDS - 1, HY_BANDS, dtype=F32).reshape(1, HY_BANDS)
    deltas = jnp.abs(jnp.linspace(HY_MIN_DECAY, HY_MAX_DECAY, d, dtype=F32)).reshape(1, d)
    ins = [f_w1[0:1], f_w1[1:1 + HY_BANDS], f_w1[1 + HY_BANDS:], f_b1.reshape(1, ffn), f_w2,
           f_b2.reshape(1, ffn), f_w3, bands, deltas]
    return pl.pallas_call(
        functools.partial(_hy_filter_body, length=length, tr=tr),
        grid=(2 * length // tr,),
        in_specs=[pl.BlockSpec(a.shape, lambda i: (0, 0)) for a in ins],
        out_specs=pl.BlockSpec((tr, d), lambda i: (i, 0)),
        out_shape=jax.ShapeDtypeStruct((2 * length, d), F32),
        compiler_params=_cparams("parallel"),
        name="hy_filter",
    )(*ins)


def _hyena(x, scale, shift, rows_per_group, length, p, j):
    m, d = x.shape
    assert m == 2 * length
    u = _mm(x, p["hy_in_w"][j].astype(BF16), bias=p["hy_in_b"][j], mod=(scale, shift),
            rows_per_group=rows_per_group)
    post = lambda x0, x1, v: (x0, v * x1)
    x0c, z = _conv3(u, p["hy_conv"][j], [0, d, 2 * d], 512, length, post, 2, name="hy_conv")
    filt = _hy_filter(length, p["hy_f_w1"][j], p["hy_f_b1"][j], p["hy_f_w2"][j], p["hy_f_b2"][j],
                      p["hy_f_w3"][j])
    n = 2 * length
    n1, n2 = (n, 1) if n <= 1024 else _split_len(n)
    hspec = _fft_real_spectrum(filt, n1, n2)
    if n2 == 1:
        z2 = jnp.pad(z.reshape(2, length, d), ((0, 0), (0, length), (0, 0)))
        y = _fft_conv_pair_single(z2, hspec, n)[:, :length]
    else:
        y = _fft_conv_pair(z.reshape(2, length, d), hspec, n1, n2)
    fn = lambda yt, zt, x0t, sk: ((yt + sk * zt) * x0t,)
    g = _rowwise(fn, [y.reshape(m, d), z, x0c], [p["hy_skip"][j].reshape(1, 1, d)], [d], name="hy_gate")[0]
    return _mm(g, p["hy_out_w"][j].astype(BF16), bias=p["hy_out_b"][j])


def _fft_conv_pair_single(z2, hspec, n, tcol=512):
    c = z2.shape[-1]
    w1 = _cplx_mat(_inner_phase(n, -1.0)).astype(BF16)
    w2 = (_cplx_mat(_inner_phase(n, 1.0)) * (1.0 / n)).astype(BF16)
    tcol = min(tcol, c)
    return pl.pallas_call(
        _spec_mul_body,
        grid=(c // tcol,),
        in_specs=[pl.BlockSpec((2 * n, 2 * n), lambda j: (0, 0)),
                  pl.BlockSpec((2 * n, 2 * n), lambda j: (0, 0)),
                  pl.BlockSpec((2, n, tcol), lambda j: (0, 0, j)),
                  pl.BlockSpec((2, n, tcol), lambda j: (0, 0, j))],
        out_specs=pl.BlockSpec((2, n, tcol), lambda j: (0, 0, j)),
        out_shape=jax.ShapeDtypeStruct((2, n, c), F32),
        compiler_params=_cparams("parallel"),
        name="fft_mid1",
    )(w1, w2, z2, hspec)


def _fnet(x, scale, shift, rows_per_group, batch, length, p, j):
    m, d = x.shape
    gc = d // FN_GROUPS
    ph = _inner_phase(gc, -1.0)
    eye = jnp.eye(FN_GROUPS, dtype=F32)
    w_c = jnp.concatenate([jnp.kron(eye, jnp.cos(ph)), jnp.kron(eye, jnp.sin(ph))], 1).astype(BF16)
    w = _mm(x, w_c, mod=(scale, shift), rows_per_group=rows_per_group)
    if length <= 1024:
        n1, n2 = length, 1
    else:
        n1, n2 = _split_len(length)
    norm = 1.0 / math.sqrt(length * gc)
    if n2 == 1:
        ph1 = _inner_phase(n1, -1.0)
        wr = (jnp.concatenate([jnp.cos(ph1), -jnp.sin(ph1)], 1) * norm).astype(BF16)
        y = _lmm(wr, pl.BlockSpec((n1, 2 * n1), lambda b, c: (0, 0)),
                 [w.reshape(batch, n1, 2 * d)] * 2,
                 [pl.BlockSpec((None, n1, d), lambda b, c: (b, 0, 0)),
                  pl.BlockSpec((None, n1, d), lambda b, c: (b, 0, 1))],
                 (batch, n1, d), pl.BlockSpec((None, n1, d), lambda b, c: (b, 0, 0)), (batch, 1), "fn_pos1")
        y = y.reshape(m, d)
    else:
        m_in = _cplx_mat(_outer_mats(n1, n2, n2, -1.0)).astype(BF16)
        wv = w.reshape(batch, n2, n1 * 2 * d)
        a = _lmm(m_in, pl.BlockSpec((1, 2 * n2, 2 * n2), lambda b, s: (s, 0, 0)),
                 [wv, wv],
                 [pl.BlockSpec((None, n2, d), lambda b, s: (b, 0, 2 * s)),
                  pl.BlockSpec((None, n2, d), lambda b, s: (b, 0, 2 * s + 1))],
                 (batch, 2, n1, n2, d), pl.BlockSpec((None, 2, 1, n2, d), lambda b, s: (b, 0, s, 0, 0)),
                 (batch, n1), "fn_pos_in")
        ph1 = _inner_phase(n1, -1.0)
        wr = (jnp.concatenate([jnp.cos(ph1), -jnp.sin(ph1)], 1) * norm).astype(BF16)
        cols = n2 * d
        tcol = 1024
        y = _lmm(wr, pl.BlockSpec((n1, 2 * n1), lambda b, c: (0, 0)),
                 [a.reshape(batch, 2, n1, cols)],
                 [pl.BlockSpec((None, 2, n1, tcol), lambda b, c: (b, 0, 0, c))],
                 (batch, n1, cols), pl.BlockSpec((None, n1, tcol), lambda b, c: (b, 0, c)),
                 (batch, cols // tcol), "fn_pos_mid")
        y = y.reshape(m, d)
    return _mm(y, p["fn_out_w"][j].astype(BF16), bias=p["fn_out_b"][j])


def _head_l2(t, extra):
    outs = []
    for h in range(t.shape[1] // GD_DK):
        th = t[:, h * GD_DK:(h + 1) * GD_DK]
        outs.append(th * (lax.rsqrt(jnp.sum(th * th, -1, keepdims=True) + 1e-6) * extra))
    return jnp.concatenate(outs, -1)


def _silu(v):
    return v * jax.nn.sigmoid(v)


def _gdn_scan_body(q_ref, k_ref, v_ref, gb_ref, gbt_ref, s0_ref, o_ref, sfin_ref, s_scr, *, n_chunks):
    direction = pl.program_id(0)
    c = pl.program_id(2)
    cs = GD_CHUNK

    @pl.when(c == 0)
    def _():
        s_scr[...] = s0_ref[...]

    ri = lax.broadcasted_iota(jnp.int32, (cs, cs), 0)
    ci = lax.broadcasted_iota(jnp.int32, (cs, cs), 1)
    lag = (ri - ci) * (1 - 2 * direction)
    incl = lag >= 0
    strict = lag > 0
    tri = incl.astype(F32)
    tri_t = (lag <= 0).astype(F32)
    eye = (ri == ci).astype(F32)
    pair_masks = []
    for lvl in range(int(math.log2(cs))):
        rb, cb = lax.shift_right_logical(ri, lvl), lax.shift_right_logical(ci, lvl)
        pair_masks.append((jnp.abs(rb - cb) == 1) & ((jnp.minimum(rb, cb) & 1) == 0))
    gb = gb_ref[...]
    gbt = gbt_ref[...]
    gc_cols = _dot_hi(tri, gb)
    gc_rows = _dot_hi(gbt, tri_t)
    tot = jnp.sum(gb, 0, keepdims=True)
    outs = []
    for h in range(GD_HEADS):
        sl = slice(h * GD_DK, (h + 1) * GD_DK)
        q, k, v = q_ref[:, sl], k_ref[:, sl], v_ref[:, sl]
        gcol = gc_cols[:, h:h + 1]
        grow = gc_rows[h:h + 1, :]
        beta = gb[:, GD_HEADS + h:GD_HEADS + h + 1]
        gtot = tot[:, h:h + 1]
        decay = jnp.exp(jnp.where(incl, gcol - grow, -jnp.inf))
        eg = jnp.exp(gcol)
        kb = k * beta
        kbf = k.astype(BF16)
        kk = lax.dot_general(kb.astype(BF16), kbf, (((1,), (1,)), ((), ())), preferred_element_type=F32)
        a = jnp.where(strict, kk * decay, 0.0)
        inv = eye - jnp.where(pair_masks[0], a, 0.0)
        for pm in pair_masks[1:]:
            inv = inv - _dot3(_dot3(inv, jnp.where(pm, a, 0.0)), inv)
        rhs = jnp.concatenate([v * beta, kb * eg], -1)
        sol = _dot3(inv, rhs)
        u_c, w_c = sol[:, :GD_DK], sol[:, GD_DK:]
        kd = k * jnp.exp(gtot - gcol)
        qk = lax.dot_general(q.astype(BF16), kbf, (((1,), (1,)), ((), ())), preferred_element_type=F32)
        attn = qk * decay
        s = s_scr[h]
        sb = s.astype(BF16)
        v_new = u_c - _dot(w_c.astype(BF16), sb)
        vnb = v_new.astype(BF16)
        o = _dot((q * eg).astype(BF16), sb) + _dot(attn.astype(BF16), vnb)
        s_scr[h] = s * jnp.exp(gtot) + lax.dot_general(kd.astype(BF16), vnb, (((0,), (0,)), ((), ())),
                                                       preferred_element_type=F32)
        outs.append(o)
    o_ref[...] = jnp.concatenate(outs, -1)

    @pl.when(c == n_chunks - 1)
    def _():
        sfin_ref[...] = s_scr[...]


def _gdn_scan(q, k, v, gb, gbt, s0):
    b, length, w = q.shape
    n_chunks = length // GD_CHUNK
    cidx = lambda d, c: c + d * (n_chunks - 1 - 2 * c)
    seq = lambda d, bi, c: (bi, cidx(d, c), 0)
    return pl.pallas_call(
        functools.partial(_gdn_scan_body, n_chunks=n_chunks),
        grid=(2, b, n_chunks),
        in_specs=[pl.BlockSpec((None, GD_CHUNK, w), seq),
                  pl.BlockSpec((None, GD_CHUNK, w), seq),
                  pl.BlockSpec((None, GD_CHUNK, w), seq),
                  pl.BlockSpec((None, None, GD_CHUNK, 128), lambda d, bi, c: (d, bi, cidx(d, c), 0)),
                  pl.BlockSpec((None, None, None, 16, GD_CHUNK), lambda d, bi, c: (d, bi, cidx(d, c), 0, 0)),
                  pl.BlockSpec((None, None, GD_HEADS, GD_DK, GD_DK), lambda d, bi, c: (d, bi, 0, 0, 0))],
        out_specs=[pl.BlockSpec((None, None, GD_CHUNK, w), lambda d, bi, c: (d, bi, cidx(d, c), 0)),
                   pl.BlockSpec((None, None, GD_HEADS, GD_DK, GD_DK), lambda d, bi, c: (d, bi, 0, 0, 0))],
        out_shape=[jax.ShapeDtypeStruct((2, b, length, w), F32),
                   jax.ShapeDtypeStruct((2, b, GD_HEADS, GD_DK, GD_DK), F32)],
        scratch_shapes=[pltpu.VMEM((GD_HEADS, GD_DK, GD_DK), F32)],
        compiler_params=_cparams("parallel", "parallel", "arbitrary"),
        name="gdn_scan",
    )(q, k, v, gb, gbt, s0)


def _gdn_inputs(x, scale, shift, rows_per_group, batch, length, p, j):
    m, d = x.shape
    wd = GD_HEADS * GD_DK
    in_w = p["gd_in_w"][j]
    u = _mm(x, in_w[:, :4 * wd].astype(BF16), mod=(scale, shift), rows_per_group=rows_per_group)
    w_ab = jnp.pad(in_w[:, 4 * wd:], ((0, 0), (0, 128 - 4 * GD_HEADS))).astype(BF16)
    ab = _mm(x, w_ab, mod=(scale, shift), rows_per_group=rows_per_group)
    qscale = GD_DK ** -0.5
    post = lambda qc, kc, vc: (_head_l2(_silu(qc), qscale), _head_l2(_silu(kc), 1.0), _silu(vc))
    q, k, v = _conv3(u, p["gd_conv"][j], [0, wd, 2 * wd], 512, length, post, 3, name="gd_conv")
    nh = GD_HEADS
    a_par = jnp.zeros((1, 1, 128), F32).at[0, 0, :2 * nh].set(-jnp.exp(p["gd_a_log"][j]).reshape(-1))
    dt_par = jnp.zeros((1, 1, 128), F32).at[0, 0, :2 * nh].set(p["gd_dt_bias"][j].reshape(-1))

    def gate_fn(abt, an, dtb):
        pre = abt + dtb
        sp = jnp.maximum(pre, 0.0) + jnp.log(1.0 + jnp.exp(-jnp.abs(pre)))
        lane = lax.broadcasted_iota(jnp.int32, abt.shape, 1)
        return (jnp.where(lane < 2 * nh, an * sp, jax.nn.sigmoid(abt)),)

    gall = _rowwise(gate_fn, [ab], [a_par, dt_par], [128], name="gd_gate")[0]
    pad = jnp.zeros((m, 128 - 2 * nh), F32)
    gb = jnp.stack([jnp.concatenate([gall[:, dr * nh:(dr + 1) * nh],
                                     gall[:, (2 + dr) * nh:(3 + dr) * nh], pad], -1) for dr in range(2)])
    gb = gb.reshape(2, batch, length, 128)
    gbt = jnp.swapaxes(gb[..., :2 * nh].reshape(2, batch, length // GD_CHUNK, GD_CHUNK, 2 * nh), -1, -2)
    rs = lambda t: t.reshape(batch, length, wd)
    return rs(q), rs(k), rs(v), gb, gbt, u


def _gdn_out(o2, u, norm_g, out_w):
    m, wd = o2.shape[1], o2.shape[2]
    ng = jnp.tile(norm_g, wd // norm_g.shape[0]).reshape(1, 1, wd)

    def body(of_ref, ob_ref, z_ref, ng_ref, o_ref):
        o = of_ref[...] + ob_ref[...]
        outs = []
        for h in range(wd // GD_DK):
            oh = o[:, h * GD_DK:(h + 1) * GD_DK]
            outs.append(oh * lax.rsqrt(jnp.mean(oh * oh, -1, keepdims=True) + 1e-6))
        o_ref[...] = jnp.concatenate(outs, -1) * ng_ref[...] * _silu(z_ref[...])

    tm = math.gcd(512, m)
    g = pl.pallas_call(
        body,
        grid=(m // tm,),
        in_specs=[pl.BlockSpec((None, tm, wd), lambda i: (0, i, 0)),
                  pl.BlockSpec((None, tm, wd), lambda i: (1, i, 0)),
                  pl.BlockSpec((tm, wd), lambda i: (i, 3)),
                  pl.BlockSpec((None, 1, wd), lambda i: (0, 0, 0))],
        out_specs=pl.BlockSpec((tm, wd), lambda i: (i, 0)),
        out_shape=jax.ShapeDtypeStruct((m, wd), F32),
        compiler_params=_cparams("parallel"),
        name="gd_norm",
    )(o2, o2, u, ng)
    return _mm(g, out_w.astype(BF16))


def _topk_rows(s, n_take, val_ref, idx_ref, base):
    r = s.shape[0]
    rid = lax.broadcasted_iota(jnp.int32, s.shape, 0)
    for t in range(n_take):
        mx = jnp.max(s, 0, keepdims=True)
        am = jnp.min(jnp.where(s == mx, rid, r), 0, keepdims=True)
        val_ref[base + t:base + t + 1, :] = mx
        idx_ref[base + t:base + t + 1, :] = am
        s = jnp.where(rid == am, -jnp.inf, s)


def _pk_topk_body(q_ref, keys_ref, i1_ref, i2_ref, gate_ref, sv_scr, si_scr, cv_scr, ci_scr,
                  i1p_scr, i2p_scr, gp_scr):
    kk = PK_TOPK
    tt = q_ref.shape[0]
    for h in range(PK_HEADS):
        for p in range(2):
            qh = q_ref[:, (2 * h + p) * PK_DH:(2 * h + p + 1) * PK_DH].astype(BF16)
            st = lax.dot_general(keys_ref[h, p], qh, (((1,), (1,)), ((), ())),
                                 preferred_element_type=F32)
            _topk_rows(st, kk, sv_scr, si_scr, p * kk)
        sv1, sv2 = sv_scr[0:kk, :], sv_scr[kk:2 * kk, :]
        si1, si2 = si_scr[0:kk, :], si_scr[kk:2 * kk, :]
        cand = jnp.concatenate([sv1[i:i + 1, :] + sv2 for i in range(kk)], 0)
        _topk_rows(cand, kk, cv_scr, ci_scr, 0)
        cv, ci = cv_scr[...], ci_scr[...]
        a1, a2 = lax.shift_right_logical(ci, int(math.log2(kk))), ci & (kk - 1)
        i1 = jnp.zeros((kk, tt), jnp.int32)
        i2 = jnp.zeros((kk, tt), jnp.int32)
        for r in range(kk):
            i1 = i1 + jnp.where(a1 == r, si1[r:r + 1, :], 0)
            i2 = i2 + jnp.where(a2 == r, si2[r:r + 1, :], 0)
        i1p_scr[h * kk:(h + 1) * kk, :] = i1
        i2p_scr[h * kk:(h + 1) * kk, :] = i2
        e = jnp.exp(cv - jnp.max(cv, 0, keepdims=True))
        gp_scr[h * kk:(h + 1) * kk, :] = e / jnp.sum(e, 0, keepdims=True)
    i1_ref[...] = i1p_scr[...].T
    i2_ref[...] = i2p_scr[...].T
    gate_ref[...] = gp_scr[...].T


def _pk_topk(q, keys_bf16):
    m = q.shape[0]
    tt = PK_TOK
    hk = PK_HEADS * PK_TOPK
    assert hk == tt
    return pl.pallas_call(
        _pk_topk_body,
        grid=(m // tt,),
        in_specs=[pl.BlockSpec((tt, q.shape[1]), lambda i: (i, 0)),
                  pl.BlockSpec(keys_bf16.shape, lambda i: (0, 0, 0, 0))],
        out_specs=[pl.BlockSpec((tt, hk), lambda i: (i, 0))] * 3,
        out_shape=[jax.ShapeDtypeStruct((m, hk), jnp.int32), jax.ShapeDtypeStruct((m, hk), jnp.int32),
                   jax.ShapeDtypeStruct((m, hk), F32)],
        scratch_shapes=[pltpu.VMEM((2 * PK_TOPK, tt), F32), pltpu.VMEM((2 * PK_TOPK, tt), jnp.int32),
                        pltpu.VMEM((PK_TOPK, tt), F32), pltpu.VMEM((PK_TOPK, tt), jnp.int32),
                        pltpu.VMEM((hk, tt), jnp.int32), pltpu.VMEM((hk, tt), jnp.int32),
                        pltpu.VMEM((hk, tt), F32)],
        compiler_params=_cparams("parallel"),
        name="pk_topk",
    )(q, keys_bf16)


def _pk_dense_body(x_ref, sc_ref, sh_ref, gt_ref, lg_ref, lb_ref, i1_ref, i2_ref, gate_ref, ut_ref, v_ref,
                   o_ref, hb_scr, act_scr, w_scr, grid_scr, acc_scr, *, alpha, n_chunks):
    e = pl.program_id(1)
    tm = x_ref.shape[0]
    nk = PK_NKEYS
    per = ut_ref.shape[1] // nk

    @pl.when(e == 0)
    def _():
        hb_scr[...] = (x_ref[...] * (1.0 + sc_ref[...]) + sh_ref[...]).astype(BF16)
        act_scr[...] = jnp.zeros_like(act_scr)

    @pl.when(e < n_chunks)
    def _():
        s = _dot(hb_scr[...], ut_ref[...])
        i1 = i1_ref[...]
        i2 = i2_ref[...]
        act = act_scr[...]
        for k in range(per):
            got = jnp.take_along_axis(s[:, k * nk:(k + 1) * nk], i2, axis=1)
            act = jnp.where(i1 == e * per + k, got, act)
        act_scr[...] = act

    @pl.when(e == n_chunks)
    def _():
        w_scr[...] = jax.nn.gelu(act_scr[...]) * gate_ref[...]
        acc_scr[...] = jnp.zeros_like(acc_scr)
        sub = lax.broadcasted_iota(jnp.int32, (nk, nk), 0)

        def tok(t, carry):
            wrow = w_scr[pl.ds(t, 1), :]
            at = jnp.where(sub == i1_ref[pl.ds(t, 1), :], wrow, 0.0).astype(BF16)
            bt = (sub == i2_ref[pl.ds(t, 1), :]).astype(BF16)
            grid_scr[t] = lax.dot_general(at, bt, (((1,), (1,)), ((), ())), preferred_element_type=F32)
            return carry

        lax.fori_loop(0, tm, tok, 0)

    @pl.when(e >= n_chunks)
    def _():
        c = e - n_chunks
        lhs = jnp.concatenate([grid_scr[:, c * per + k, :] for k in range(per)], -1).astype(BF16)
        acc_scr[...] += _dot(lhs, v_ref[...])

    @pl.when(e == 2 * n_chunks - 1)
    def _():
        o_ref[...] = _ln(alpha * x_ref[...] + gt_ref[...] * acc_scr[...], lg_ref[...], lb_ref[...])


def _pk_dense(x, scale, shift, gate_vec, ln_g, ln_b, i1, i2, gate, ut_bf16, v_bf16, rows_per_group, alpha):
    m, d = x.shape
    n_exp = v_bf16.shape[0]
    tm = math.gcd(PK_TM, m, rows_per_group)
    n_chunks = n_exp // PK_EC
    g = rows_per_group // tm
    assert n_exp % PK_EC == 0 and PK_EC % PK_NKEYS == 0
    row = lambda i, e: (i, 0)
    vec = lambda i, e: (i // g, 0, 0)
    one = lambda i, e: (0, 0, 0)
    npk = i1.shape[1]
    return pl.pallas_call(
        functools.partial(_pk_dense_body, alpha=alpha, n_chunks=n_chunks),
        grid=(m // tm, 2 * n_chunks),
        in_specs=[pl.BlockSpec((tm, d), row),
                  pl.BlockSpec((None, 1, d), vec), pl.BlockSpec((None, 1, d), vec), pl.BlockSpec((None, 1, d), vec),
                  pl.BlockSpec((None, 1, d), one), pl.BlockSpec((None, 1, d), one),
                  pl.BlockSpec((tm, npk), row), pl.BlockSpec((tm, npk), row), pl.BlockSpec((tm, npk), row),
                  pl.BlockSpec((d, PK_EC), lambda i, e: (0, jnp.minimum(e, n_chunks - 1))),
                  pl.BlockSpec((PK_EC, d), lambda i, e: (jnp.maximum(e - n_chunks, 0), 0))],
        out_specs=pl.BlockSpec((tm, d), row),
        out_shape=jax.ShapeDtypeStruct((m, d), F32),
        scratch_shapes=[pltpu.VMEM((tm, d), BF16), pltpu.VMEM((tm, npk), F32), pltpu.VMEM((tm, npk), F32),
                        pltpu.VMEM((tm, PK_NKEYS, PK_NKEYS), F32), pltpu.VMEM((tm, d), F32)],
        compiler_params=_cparams("parallel", "arbitrary"),
        name="pk_dense",
    )(x, scale, shift, gate_vec, ln_g, ln_b, i1, i2, gate, ut_bf16, v_bf16)


def _peer_ln(x, scale, shift, gate_vec, ln_g, ln_b, rows_per_group, alpha, wq_bf16, keys_bf16, ut_bf16, v_bf16):
    q = _mm(x, wq_bf16, mod=(scale, shift), rows_per_group=rows_per_group)
    i1, i2, gate = _pk_topk(q, keys_bf16)
    return _pk_dense(x, scale, shift, gate_vec, ln_g, ln_b, i1, i2, gate, ut_bf16, v_bf16, rows_per_group, alpha)


def kernel(x, c, ctx, c_ctx, ada_w, ada_b, ln_g, ln_b, pk_wq, pk_keys, pk_u, pk_v, hy_in_w, hy_in_b, hy_conv, hy_f_w1, hy_f_b1, hy_f_w2, hy_f_b2, hy_f_w3, hy_skip, hy_out_w, hy_out_b, gd_in_w, gd_conv, gd_a_log, gd_dt_bias, gd_norm_g, gd_out_w, fn_out_w, fn_out_b):
    p = dict(hy_in_w=hy_in_w, hy_in_b=hy_in_b, hy_conv=hy_conv, hy_f_w1=hy_f_w1, hy_f_b1=hy_f_b1,
             hy_f_w2=hy_f_w2, hy_f_b2=hy_f_b2, hy_f_w3=hy_f_w3, hy_skip=hy_skip, hy_out_w=hy_out_w,
             hy_out_b=hy_out_b, gd_in_w=gd_in_w, gd_conv=gd_conv, gd_a_log=gd_a_log, gd_dt_bias=gd_dt_bias,
             fn_out_w=fn_out_w, fn_out_b=fn_out_b)
    b, length, d = x.shape
    lc = ctx.shape[1]
    depth = ada_w.shape[0]
    alpha = (2 * depth) ** 0.25
    xl = _pos_add(x)
    xc = ctx.reshape(b * lc, d)
    gdn_layers = [i for i in range(depth) if i % N_MIXERS == 1]
    ctx_until = gdn_layers[-1] if gdn_layers else -1
    cond = jnp.concatenate([c, c_ctx[None], jnp.zeros((8 - b - 1, d), F32)], 0)
    for i in range(depth):
        kind, j = i % N_MIXERS, i // N_MIXERS
        ctx_in, ctx_out = i <= ctx_until, i < ctx_until
        mod = _mm(cond, ada_w[i].astype(BF16), bias=ada_b[i], silu_in=True, tn=2048)
        mod = mod.reshape(8, N_MOD, 1, d)
        ml = [mod[:b, t] for t in range(N_MOD)]
        mc = [mod[b:b + 1, t] for t in range(N_MOD)]
        lg = [ln_g[i, t].reshape(1, 1, d) for t in range(2)]
        lb = [ln_b[i, t].reshape(1, 1, d) for t in range(2)]
        yc = None
        if kind == 0:
            yl = _hyena(xl, ml[1], ml[0], length, length, p, j)
            if ctx_out:
                yc = _hyena(xc, mc[1], mc[0], b * lc, lc, p, j)
        elif kind == 1:
            qc, kc, vc, gbc, gbtc, uc = _gdn_inputs(xc, mc[1], mc[0], b * lc, b, lc, p, j)
            ql, kl, vl, gbl, gbtl, ul = _gdn_inputs(xl, ml[1], ml[0], length, b, length, p, j)
            s0 = jnp.zeros((2, b, GD_HEADS, GD_DK, GD_DK), F32)
            oc, s_ctx = _gdn_scan(qc, kc, vc, gbc, gbtc, s0)
            ol, _ = _gdn_scan(ql, kl, vl, gbl, gbtl, s_ctx)
            yl = _gdn_out(ol.reshape(2, b * length, -1), ul, gd_norm_g[j], gd_out_w[j])
            if ctx_out:
                yc = _gdn_out(oc.reshape(2, b * lc, -1), uc, gd_norm_g[j], gd_out_w[j])
        else:
            yl = _fnet(xl, ml[1], ml[0], length, b, length, p, j)
            if ctx_out:
                yc = _fnet(xc, mc[1], mc[0], b * lc, b, lc, p, j)
        wq = pk_wq[i].astype(BF16)
        keys = pk_keys[i].astype(BF16)
        ut, vt = pk_u[i].astype(BF16).T, pk_v[i].astype(BF16)
        xl = _res_ln(xl, yl, ml[2], lg[0], lb[0], alpha, length)
        xl = _peer_ln(xl, ml[4], ml[3], ml[5], lg[1], lb[1], length, alpha, wq, keys, ut, vt)
        if ctx_out:
            xc = _res_ln(xc, yc, mc[2], lg[0], lb[0], alpha, b * lc)
            xc = _peer_ln(xc, mc[4], mc[3], mc[5], lg[1], lb[1], b * lc, alpha, wq, keys, ut, vt)
    return xl.reshape(b, length, d)
```

```python
import functools
import math

import numpy as np
import jax
import jax.numpy as jnp
from jax import lax
from jax.experimental import pallas as pl
from jax.experimental.pallas import tpu as pltpu

F32 = jnp.float32
BF16 = jnp.bfloat16

GRID_W = 64
N_MIXERS = 3
N_MOD = 6
LN_EPS = 1e-5
HY_EMB = 33
HY_BANDS = (HY_EMB - 1) // 2
HY_SHIFT = 0.05
HY_TARGET = 1e-2
HY_MIN_DECAY = math.log(HY_TARGET) / 1.5
HY_MAX_DECAY = math.log(HY_TARGET) / 0.3
GD_HEADS = 8
GD_DK = 128
GD_CHUNK = 64
FN_GROUPS = 4
PK_HEADS = 8
PK_NKEYS = 128
PK_DH = 128
PK_TOPK = 16
PK_TOK = 128
PK_TM = 256
PK_EC = 1024
PK_UNROLL = 8

VMEM_LIMIT_BYTES = 56 * 1024 * 1024


def _cparams(*sem):
    return pltpu.CompilerParams(dimension_semantics=sem, vmem_limit_bytes=VMEM_LIMIT_BYTES)


def _dot(a, b):
    return jnp.dot(a, b, preferred_element_type=F32)


def _dot_hi(a, b):
    return jnp.dot(a, b, preferred_element_type=F32, precision=lax.Precision.HIGHEST)


def _split(a):
    hi = a.astype(BF16)
    lo = (a - hi.astype(F32)).astype(BF16)
    return hi, lo


def _dot3(a, b):
    ah, al = _split(a)
    bh, bl = _split(b)
    return _dot(ah, bh) + (_dot(ah, bl) + _dot(al, bh))


def _mm_body(*refs, has_mod, has_bias, silu_in):
    a_ref, w_ref = refs[0], refs[1]
    k = 2
    a = a_ref[...]
    if has_mod:
        a = a * (1.0 + refs[k][...]) + refs[k + 1][...]
        k += 2
    if silu_in:
        a = a * jax.nn.sigmoid(a)
    o = _dot(a.astype(BF16), w_ref[...])
    if has_bias:
        o = o + refs[k][...]
        k += 1
    refs[k][...] = o


def _mm(a, w_bf16, bias=None, mod=None, rows_per_group=None, silu_in=False, tm=512, tn=None):
    m, k = a.shape
    n = w_bf16.shape[1]
    tm = math.gcd(tm, m, rows_per_group or m)
    if tn is None:
        tn = max(t for t in range(128, min(n, 2048) + 1, 128) if n % t == 0)
    assert m % tm == 0 and n % tn == 0
    ins = [a, w_bf16]
    specs = [pl.BlockSpec((tm, k), lambda j, i: (i, 0)), pl.BlockSpec((k, tn), lambda j, i: (0, j))]
    if mod is not None:
        assert rows_per_group % tm == 0
        g = rows_per_group // tm
        for v in mod:
            ins.append(v)
            specs.append(pl.BlockSpec((None, 1, k), lambda j, i: (i // g, 0, 0)))
    if bias is not None:
        ins.append(bias.reshape(1, n))
        specs.append(pl.BlockSpec((1, tn), lambda j, i: (0, j)))
    return pl.pallas_call(
        functools.partial(_mm_body, has_mod=mod is not None, has_bias=bias is not None, silu_in=silu_in),
        grid=(n // tn, m // tm),
        in_specs=specs,
        out_specs=pl.BlockSpec((tm, tn), lambda j, i: (i, j)),
        out_shape=jax.ShapeDtypeStruct((m, n), F32),
        compiler_params=_cparams("parallel", "parallel"),
        name="mm",
    )(*ins)


def _rowwise(fn, rows, vecs, out_cols, rows_per_group=None, tm=512, name="rowwise"):
    m = rows[0].shape[0]
    tm = math.gcd(tm, m, rows_per_group or m)
    assert m % tm == 0
    n_r, n_v, n_o = len(rows), len(vecs), len(out_cols)

    def body(*refs):
        outs = fn(*[r[...] for r in refs[:n_r + n_v]])
        for o_ref, o in zip(refs[n_r + n_v:], outs):
            o_ref[...] = o

    specs = [pl.BlockSpec((tm, r.shape[1]), lambda i: (i, 0)) for r in rows]
    for v in vecs:
        if v.shape[0] == 1:
            specs.append(pl.BlockSpec((None, 1, v.shape[2]), lambda i: (0, 0, 0)))
        else:
            assert rows_per_group % tm == 0
            g = rows_per_group // tm
            specs.append(pl.BlockSpec((None, 1, v.shape[2]), lambda i, g=g: (i // g, 0, 0)))
    return pl.pallas_call(
        body,
        grid=(m // tm,),
        in_specs=specs,
        out_specs=[pl.BlockSpec((tm, c), lambda i: (i, 0)) for c in out_cols],
        out_shape=[jax.ShapeDtypeStruct((m, c), F32) for c in out_cols],
        compiler_params=_cparams("parallel"),
        name=name,
    )(*rows, *vecs)


def _ln(v, g, b):
    mu = jnp.mean(v, -1, keepdims=True)
    d = v - mu
    var = jnp.mean(d * d, -1, keepdims=True)
    return d * lax.rsqrt(var + LN_EPS) * g + b


def _res_ln(x, y, gate, ln_g, ln_b, alpha, rows_per_group):
    fn = lambda xt, yt, gt, lg, lb: (_ln(alpha * xt + gt * yt, lg, lb),)
    return _rowwise(fn, [x, y], [gate, ln_g, ln_b], [x.shape[1]], rows_per_group, name="res_ln")[0]


def _pos_add_body(x_ref, er_ref, ec_ref, o_ref):
    half = er_ref.shape[-1]
    x = x_ref[...]
    er = jnp.broadcast_to(er_ref[...], x.shape[:2] + (half,))
    ec = jnp.broadcast_to(ec_ref[...][None], x.shape[:2] + (half,))
    o_ref[...] = x + jnp.concatenate([er, ec], -1)


def _pos_add(x):
    b, length, d = x.shape
    rows = length // GRID_W
    quarter = d // 4
    omega = 1.0 / (10000.0 ** (jnp.arange(quarter, dtype=F32) / quarter))
    er = jnp.arange(rows, dtype=F32)[:, None] * omega
    ec = jnp.arange(GRID_W, dtype=F32)[:, None] * omega
    emb_r = jnp.concatenate([jnp.sin(er), jnp.cos(er)], -1).reshape(rows, 1, d // 2)
    emb_c = jnp.concatenate([jnp.sin(ec), jnp.cos(ec)], -1)
    rt = 8
    x4 = x.reshape(b, rows, GRID_W, d)
    out = pl.pallas_call(
        _pos_add_body,
        grid=(b, rows // rt),
        in_specs=[pl.BlockSpec((None, rt, GRID_W, d), lambda i, j: (i, j, 0, 0)),
                  pl.BlockSpec((rt, 1, d // 2), lambda i, j: (j, 0, 0)),
                  pl.BlockSpec((GRID_W, d // 2), lambda i, j: (0, 0))],
        out_specs=pl.BlockSpec((None, rt, GRID_W, d), lambda i, j: (i, j, 0, 0)),
        out_shape=jax.ShapeDtypeStruct(x4.shape, F32),
        compiler_params=_cparams("parallel", "parallel"),
        name="pos_add",
    )(x4, emb_r, emb_c)
    return out.reshape(b * length, d)


def _conv3_body(*refs, n_parts, tm, seq_len, post, n_vec):
    i = pl.program_id(0)
    first = (i * tm) % seq_len == 0
    last = ((i + 1) * tm) % seq_len == 0
    row = lax.broadcasted_iota(jnp.int32, (tm, 1), 0)
    parts = []
    for p in range(n_parts):
        main_ref, prev_ref, next_ref, w_ref = refs[4 * p:4 * p + 4]
        u = main_ref[...]
        w = w_ref[...]
        prev = jnp.where(first, 0.0, prev_ref[7:8, :])
        nxt = jnp.where(last, 0.0, next_ref[0:1, :])
        up = jnp.where(row == 0, prev, pltpu.roll(u, 1, 0))
        dn = jnp.where(row == tm - 1, nxt, pltpu.roll(u, tm - 1, 0))
        parts.append(up * w[0:1, :] + u * w[1:2, :] + dn * w[2:3, :])
    k = 4 * n_parts
    vecs = [refs[k + j][...] for j in range(n_vec)]
    outs = post(*parts, *vecs)
    for o_ref, o in zip(refs[k + n_vec:], outs):
        o_ref[...] = o


def _conv3(u, w, col_parts, tc, seq_len, post, n_out, vecs=(), tm=256, name="conv3"):
    m = u.shape[0]
    tm = min(tm, seq_len)
    assert seq_len % tm == 0 and m % tm == 0 and tm % 8 == 0
    width = col_parts[1] - col_parts[0] if len(col_parts) > 1 else tc
    ncol = width // tc
    t8 = tm // 8
    nb8 = m // 8
    ins, specs = [], []
    for c0 in col_parts:
        cb = c0 // tc
        ins += [u, u, u, w]
        specs += [
            pl.BlockSpec((tm, tc), lambda i, j, cb=cb: (i, cb + j)),
            pl.BlockSpec((8, tc), lambda i, j, cb=cb: (jnp.maximum(i * t8 - 1, 0), cb + j)),
            pl.BlockSpec((8, tc), lambda i, j, cb=cb: (jnp.minimum((i + 1) * t8, nb8 - 1), cb + j)),
            pl.BlockSpec((3, tc), lambda i, j, cb=cb: (0, cb + j)),
        ]
    for v in vecs:
        ins.append(v)
        specs.append(pl.BlockSpec((1, tc), lambda i, j: (0, j)))
    return pl.pallas_call(
        functools.partial(_conv3_body, n_parts=len(col_parts), tm=tm, seq_len=seq_len, post=post,
                          n_vec=len(vecs)),
        grid=(m // tm, ncol),
        in_specs=specs,
        out_specs=[pl.BlockSpec((tm, tc), lambda i, j: (i, j)) for _ in range(n_out)],
        out_shape=[jax.ShapeDtypeStruct((m, width), F32) for _ in range(n_out)],
        compiler_params=_cparams("parallel", "parallel"),
        name=name,
    )(*ins)


def _lmm_body(*refs, n_x):
    w_ref, o_ref = refs[0], refs[1 + n_x]
    tc = o_ref.shape[-1]
    xs = [r[...].reshape(-1, tc) for r in refs[1:1 + n_x]]
    x = xs[0] if n_x == 1 else jnp.concatenate(xs, 0)
    w = w_ref[...]
    w = w.reshape(w.shape[-2], w.shape[-1])
    o_ref[...] = _dot(w, x.astype(BF16)).reshape(o_ref.shape)


def _lmm(w, w_spec, xs, x_specs, out_shape, out_spec, grid, name):
    return pl.pallas_call(
        functools.partial(_lmm_body, n_x=len(xs)),
        grid=grid,
        in_specs=[w_spec] + list(x_specs),
        out_specs=out_spec,
        out_shape=jax.ShapeDtypeStruct(out_shape, F32),
        compiler_params=_cparams(*(["parallel"] * len(grid))),
        name=name,
    )(w, *xs)


def _spec_mul_body(w1_ref, w2_ref, x_ref, h_ref, o_ref):
    tc = o_ref.shape[-1]
    x = x_ref[...].reshape(-1, tc)
    z = _dot(w1_ref[...], x.astype(BF16))
    half = z.shape[0] // 2
    zr, zi = z[:half], z[half:]
    hr, hi = h_ref[0], h_ref[1]
    y = jnp.concatenate([zr * hr - zi * hi, zr * hi + zi * hr], 0)
    o_ref[...] = _dot(w2_ref[...], y.astype(BF16)).reshape(o_ref.shape)


def _cplx_mat(ang):
    c, s = jnp.cos(ang), jnp.sin(ang)
    return jnp.concatenate([jnp.concatenate([c, -s], -1), jnp.concatenate([s, c], -1)], -2)


def _phase(k, n):
    return (k % n).astype(F32) * (2.0 * math.pi / n)


def _outer_mats(n1, n2, n_in, sign):
    n = n1 * n2
    s1 = jnp.arange(n1, dtype=jnp.int32)[:, None, None]
    f2 = jnp.arange(n2, dtype=jnp.int32)[None, :, None]
    s2 = jnp.arange(n_in, dtype=jnp.int32)[None, None, :]
    return sign * _phase(f2 * (s1 + n1 * s2), n)


def _inner_phase(n1, sign):
    a = jnp.arange(n1, dtype=jnp.int32)
    return sign * _phase(a[:, None] * a[None, :], n1)


def _split_len(n):
    n1 = 1 << (int(math.log2(n)) // 2)
    return n1, n // n1


def _fft_conv_pair(z2, hspec, n1, n2, tcol=512):
    _, length, c = z2.shape
    n = 2 * length
    assert n1 * n2 == n
    h2 = n2 // 2
    m_in = _cplx_mat(_outer_mats(n1, n2, h2, -1.0)).astype(BF16)
    a = _lmm(m_in, pl.BlockSpec((1, 2 * n2, n2), lambda s: (s, 0, 0)),
             [z2.reshape(2, h2, n1 * c)], [pl.BlockSpec((2, h2, c), lambda s: (0, 0, s))],
             (2, n1, n2, c), pl.BlockSpec((2, 1, n2, c), lambda s: (0, s, 0, 0)), (n1,), "fft_in")
    w1 = _cplx_mat(_inner_phase(n1, -1.0)).astype(BF16)
    w2 = (_cplx_mat(_inner_phase(n1, 1.0)) * (1.0 / n)).astype(BF16)
    cols = n2 * c
    tcol = min(tcol, cols)
    b = pl.pallas_call(
        _spec_mul_body,
        grid=(cols // tcol,),
        in_specs=[pl.BlockSpec((2 * n1, 2 * n1), lambda j: (0, 0)),
                  pl.BlockSpec((2 * n1, 2 * n1), lambda j: (0, 0)),
                  pl.BlockSpec((2, n1, tcol), lambda j: (0, 0, j)),
                  pl.BlockSpec((2, n1, tcol), lambda j: (0, 0, j))],
        out_specs=pl.BlockSpec((2, n1, tcol), lambda j: (0, 0, j)),
        out_shape=jax.ShapeDtypeStruct((2, n1, cols), F32),
        compiler_params=_cparams("parallel"),
        name="fft_mid",
    )(w1, w2, a.reshape(2, n1, cols), hspec.reshape(2, n1, cols))
    m_out = _cplx_mat(jnp.swapaxes(_outer_mats(n1, n2, h2, 1.0), 1, 2)).astype(BF16)
    y = _lmm(m_out, pl.BlockSpec((1, n2, 2 * n2), lambda s: (s, 0, 0)),
             [b.reshape(2, n1, n2, c)], [pl.BlockSpec((2, 1, n2, c), lambda s: (0, s, 0, 0))],
             (2, h2, n1 * c), pl.BlockSpec((2, h2, c), lambda s: (0, 0, s)), (n1,), "fft_out")
    return y.reshape(2, length, c)


def _fft_real_spectrum(f, n1, n2, tcol=512):
    n, c = f.shape
    if n2 == 1:
        ph = _inner_phase(n1, -1.0)
        w = jnp.concatenate([jnp.cos(ph), jnp.sin(ph)], 0).astype(BF16)
        tcol = min(tcol, c)
        return _lmm(w, pl.BlockSpec((2 * n1, n1), lambda j: (0, 0)),
                    [f], [pl.BlockSpec((n1, tcol), lambda j: (0, j))],
                    (2, n1, c), pl.BlockSpec((2, n1, tcol), lambda j: (0, 0, j)), (c // tcol,), "fft_spec1")
    ph = _outer_mats(n1, n2, n2, -1.0)
    m_in = jnp.concatenate([jnp.cos(ph), jnp.sin(ph)], 1).astype(BF16)
    a = _lmm(m_in, pl.BlockSpec((1, 2 * n2, n2), lambda s: (s, 0, 0)),
             [f.reshape(n2, n1 * c)], [pl.BlockSpec((n2, c), lambda s: (0, s))],
             (2, n1, n2, c), pl.BlockSpec((2, 1, n2, c), lambda s: (0, s, 0, 0)), (n1,), "fft_spec_in")
    w1 = _cplx_mat(_inner_phase(n1, -1.0)).astype(BF16)
    cols = n2 * c
    tcol = min(tcol, cols)
    h = _lmm(w1, pl.BlockSpec((2 * n1, 2 * n1), lambda j: (0, 0)),
             [a.reshape(2, n1, cols)], [pl.BlockSpec((2, n1, tcol), lambda j: (0, 0, j))],
             (2, n1, cols), pl.BlockSpec((2, n1, tcol), lambda j: (0, 0, j)), (cols // tcol,), "fft_spec_mid")
    return h.reshape(2, n, c)


def _hy_filter_body(wt_ref, wc_ref, ws_ref, b1_ref, w2_ref, b2_ref, w3_ref, bands_ref, dl_ref, o_ref,
                    *, length, tr):
    d = o_ref.shape[-1]
    j = pl.program_id(0) * tr + lax.broadcasted_iota(jnp.int32, (tr, 1), 0)
    k = jnp.where(j < length, j, 2 * length - j)
    t = k.astype(F32) / length
    ang = 2.0 * jnp.pi * t * bands_ref[...]
    pre = t * wt_ref[...] + _dot_hi(jnp.cos(ang), wc_ref[...]) + _dot_hi(-jnp.sin(ang), ws_ref[...])
    hdn = jnp.sin(pre + b1_ref[...])
    hdn = jnp.sin(_dot_hi(hdn, w2_ref[...]) + b2_ref[...])
    hf = _dot_hi(hdn, w3_ref[...])
    win = jnp.exp(-t * dl_ref[...]) + HY_SHIFT
    h = jnp.where(j < length, hf[:, :d], hf[:, d:]) * win
    o_ref[...] = jnp.where(j == length, 0.0, h)


def _hy_filter(length, f_w1, f_b1, f_w2, f_b2, f_w3):
    d = f_w3.shape[1] // 2
    ffn = f_w2.shape[0]
    tr = min(512, length)
    bands = jnp.linspace(1e-4, HY_BANDS - 1, HY_BANDS, dtype=F32).reshape(1, HY_BANDS)
    deltas = jnp.abs(jnp.linspace(HY_MIN_DECAY, HY_MAX_DECAY, d, dtype=F32)).reshape(1, d)
    ins = [f_w1[0:1], f_w1[1:1 + HY_BANDS], f_w1[1 + HY_BANDS:], f_b1.reshape(1, ffn), f_w2,
           f_b2.reshape(1, ffn), f_w3, bands, deltas]
    return pl.pallas_call(
        functools.partial(_hy_filter_body, length=length, tr=tr),
        grid=(2 * length // tr,),
        in_specs=[pl.BlockSpec(a.shape, lambda i: (0, 0)) for a in ins],
        out_specs=pl.BlockSpec((tr, d), lambda i: (i, 0)),
        out_shape=jax.ShapeDtypeStruct((2 * length, d), F32),
        compiler_params=_cparams("parallel"),
        name="hy_filter",
    )(*ins)


def _hyena(x, scale, shift, rows_per_group, length, p, j):
    m, d = x.shape
    assert m == 2 * length
    u = _mm(x, p["hy_in_w"][j].astype(BF16), bias=p["hy_in_b"][j], mod=(scale, shift),
            rows_per_group=rows_per_group)
    post = lambda x0, x1, v: (x0, v * x1)
    x0c, z = _conv3(u, p["hy_conv"][j], [0, d, 2 * d], 512, length, post, 2, name="hy_conv")
    filt = _hy_filter(length, p["hy_f_w1"][j], p["hy_f_b1"][j], p["hy_f_w2"][j], p["hy_f_b2"][j],
                      p["hy_f_w3"][j])
    n = 2 * length
    n1, n2 = (n, 1) if n <= 1024 else _split_len(n)
    hspec = _fft_real_spectrum(filt, n1, n2)
    if n2 == 1:
        z2 = jnp.pad(z.reshape(2, length, d), ((0, 0), (0, length), (0, 0)))
        y = _fft_conv_pair_single(z2, hspec, n)[:, :length]
    else:
        y = _fft_conv_pair(z.reshape(2, length, d), hspec, n1, n2)
    fn = lambda yt, zt, x0t, sk: ((yt + sk * zt) * x0t,)
    g = _rowwise(fn, [y.reshape(m, d), z, x0c], [p["hy_skip"][j].reshape(1, 1, d)], [d], name="hy_gate")[0]
    return _mm(g, p["hy_out_w"][j].astype(BF16), bias=p["hy_out_b"][j])


def _fft_conv_pair_single(z2, hspec, n, tcol=512):
    c = z2.shape[-1]
    w1 = _cplx_mat(_inner_phase(n, -1.0)).astype(BF16)
    w2 = (_cplx_mat(_inner_phase(n, 1.0)) * (1.0 / n)).astype(BF16)
    tcol = min(tcol, c)
    return pl.pallas_call(
        _spec_mul_body,
        grid=(c // tcol,),
        in_specs=[pl.BlockSpec((2 * n, 2 * n), lambda j: (0, 0)),
                  pl.BlockSpec((2 * n, 2 * n), lambda j: (0, 0)),
                  pl.BlockSpec((2, n, tcol), lambda j: (0, 0, j)),
                  pl.BlockSpec((2, n, tcol), lambda j: (0, 0, j))],
        out_specs=pl.BlockSpec((2, n, tcol), lambda j: (0, 0, j)),
        out_shape=jax.ShapeDtypeStruct((2, n, c), F32),
        compiler_params=_cparams("parallel"),
        name="fft_mid1",
    )(w1, w2, z2, hspec)


def _fnet(x, scale, shift, rows_per_group, batch, length, p, j):
    m, d = x.shape
    gc = d // FN_GROUPS
    ph = _inner_phase(gc, -1.0)
    eye = jnp.eye(FN_GROUPS, dtype=F32)
    w_c = jnp.concatenate([jnp.kron(eye, jnp.cos(ph)), jnp.kron(eye, jnp.sin(ph))], 1).astype(BF16)
    w = _mm(x, w_c, mod=(scale, shift), rows_per_group=rows_per_group)
    if length <= 1024:
        n1, n2 = length, 1
    else:
        n1, n2 = _split_len(length)
    norm = 1.0 / math.sqrt(length * gc)
    if n2 == 1:
        ph1 = _inner_phase(n1, -1.0)
        wr = (jnp.concatenate([jnp.cos(ph1), -jnp.sin(ph1)], 1) * norm).astype(BF16)
        y = _lmm(wr, pl.BlockSpec((n1, 2 * n1), lambda b, c: (0, 0)),
                 [w.reshape(batch, n1, 2 * d)] * 2,
                 [pl.BlockSpec((None, n1, d), lambda b, c: (b, 0, 0)),
                  pl.BlockSpec((None, n1, d), lambda b, c: (b, 0, 1))],
                 (batch, n1, d), pl.BlockSpec((None, n1, d), lambda b, c: (b, 0, 0)), (batch, 1), "fn_pos1")
        y = y.reshape(m, d)
    else:
        m_in = _cplx_mat(_outer_mats(n1, n2, n2, -1.0)).astype(BF16)
        wv = w.reshape(batch, n2, n1 * 2 * d)
        a = _lmm(m_in, pl.BlockSpec((1, 2 * n2, 2 * n2), lambda b, s: (s, 0, 0)),
                 [wv, wv],
                 [pl.BlockSpec((None, n2, d), lambda b, s: (b, 0, 2 * s)),
                  pl.BlockSpec((None, n2, d), lambda b, s: (b, 0, 2 * s + 1))],
                 (batch, 2, n1, n2, d), pl.BlockSpec((None, 2, 1, n2, d), lambda b, s: (b, 0, s, 0, 0)),
                 (batch, n1), "fn_pos_in")
        ph1 = _inner_phase(n1, -1.0)
        wr = (jnp.concatenate([jnp.cos(ph1), -jnp.sin(ph1)], 1) * norm).astype(BF16)
        cols = n2 * d
        tcol = 1024
        y = _lmm(wr, pl.BlockSpec((n1, 2 * n1), lambda b, c: (0, 0)),
                 [a.reshape(batch, 2, n1, cols)],
                 [pl.BlockSpec((None, 2, n1, tcol), lambda b, c: (b, 0, 0, c))],
                 (batch, n1, cols), pl.BlockSpec((None, n1, tcol), lambda b, c: (b, 0, c)),
                 (batch, cols // tcol), "fn_pos_mid")
        y = y.reshape(m, d)
    return _mm(y, p["fn_out_w"][j].astype(BF16), bias=p["fn_out_b"][j])


def _head_l2(t, extra):
    outs = []
    for h in range(t.shape[1] // GD_DK):
        th = t[:, h * GD_DK:(h + 1) * GD_DK]
        outs.append(th * (lax.rsqrt(jnp.sum(th * th, -1, keepdims=True) + 1e-6) * extra))
    return jnp.concatenate(outs, -1)


def _silu(v):
    return v * jax.nn.sigmoid(v)


def _gdn_scan_body(q_ref, k_ref, v_ref, gb_ref, gbt_ref, s0_ref, o_ref, sfin_ref, s_scr, *, n_chunks):
    direction = pl.program_id(0)
    c = pl.program_id(2)
    cs = GD_CHUNK

    @pl.when(c == 0)
    def _():
        s_scr[...] = s0_ref[...]

    ri = lax.broadcasted_iota(jnp.int32, (cs, cs), 0)
    ci = lax.broadcasted_iota(jnp.int32, (cs, cs), 1)
    lag = (ri - ci) * (1 - 2 * direction)
    incl = lag >= 0
    strict = lag > 0
    tri = incl.astype(F32)
    tri_t = (lag <= 0).astype(F32)
    eye = (ri == ci).astype(F32)
    pair_masks = []
    for lvl in range(int(math.log2(cs))):
        rb, cb = lax.shift_right_logical(ri, lvl), lax.shift_right_logical(ci, lvl)
        pair_masks.append((jnp.abs(rb - cb) == 1) & ((jnp.minimum(rb, cb) & 1) == 0))
    gb = gb_ref[...]
    gbt = gbt_ref[...]
    gc_cols = _dot_hi(tri, gb)
    gc_rows = _dot_hi(gbt, tri_t)
    tot = jnp.sum(gb, 0, keepdims=True)
    hs = range(GD_HEADS)
    nt = (((1,), (1,)), ((), ()))
    tn = (((0,), (0,)), ((), ()))
    sl = [slice(h * GD_DK, (h + 1) * GD_DK) for h in hs]
    k = [k_ref[:, sl[h]] for h in hs]
    gcol = [gc_cols[:, h:h + 1] for h in hs]
    beta = [gb[:, GD_HEADS + h:GD_HEADS + h + 1] for h in hs]
    gtot = [tot[:, h:h + 1] for h in hs]
    decay = [jnp.exp(jnp.where(incl, gcol[h] - gc_rows[h:h + 1, :], -jnp.inf)) for h in hs]
    eg = [jnp.exp(gcol[h]) for h in hs]
    kb = [k[h] * beta[h] for h in hs]
    kbf = [k[h].astype(BF16) for h in hs]
    kk = [lax.dot_general(kb[h].astype(BF16), kbf[h], nt, preferred_element_type=F32) for h in hs]
    a = [jnp.where(strict, kk[h] * decay[h], 0.0) for h in hs]
    inv = [eye - jnp.where(pair_masks[0], a[h], 0.0) for h in hs]
    for pm in pair_masks[1:]:
        tn_s = [_dot3(inv[h], jnp.where(pm, a[h], 0.0)) for h in hs]
        tnt = [_dot3(tn_s[h], inv[h]) for h in hs]
        inv = [inv[h] - tnt[h] for h in hs]
    rhs = [jnp.concatenate([v_ref[:, sl[h]] * beta[h], kb[h] * eg[h]], -1) for h in hs]
    sol = [_dot3(inv[h], rhs[h]) for h in hs]
    q = [q_ref[:, sl[h]] for h in hs]
    qk = [lax.dot_general(q[h].astype(BF16), kbf[h], nt, preferred_element_type=F32) for h in hs]
    sb = [s_scr[h].astype(BF16) for h in hs]
    ws = [_dot(sol[h][:, GD_DK:].astype(BF16), sb[h]) for h in hs]
    qs = [_dot((q[h] * eg[h]).astype(BF16), sb[h]) for h in hs]
    vnb = [(sol[h][:, :GD_DK] - ws[h]).astype(BF16) for h in hs]
    av = [_dot((qk[h] * decay[h]).astype(BF16), vnb[h]) for h in hs]
    kv = [lax.dot_general((k[h] * jnp.exp(gtot[h] - gcol[h])).astype(BF16), vnb[h], tn,
                          preferred_element_type=F32) for h in hs]
    for h in hs:
        s_scr[h] = s_scr[h] * jnp.exp(gtot[h]) + kv[h]
    o_ref[...] = jnp.concatenate([qs[h] + av[h] for h in hs], -1)

    @pl.when(c == n_chunks - 1)
    def _():
        sfin_ref[...] = s_scr[...]


def _gdn_scan(q, k, v, gb, gbt, s0):
    b, length, w = q.shape
    n_chunks = length // GD_CHUNK
    cidx = lambda d, c: c + d * (n_chunks - 1 - 2 * c)
    seq = lambda d, bi, c: (bi, cidx(d, c), 0)
    return pl.pallas_call(
        functools.partial(_gdn_scan_body, n_chunks=n_chunks),
        grid=(2, b, n_chunks),
        in_specs=[pl.BlockSpec((None, GD_CHUNK, w), seq),
                  pl.BlockSpec((None, GD_CHUNK, w), seq),
                  pl.BlockSpec((None, GD_CHUNK, w), seq),
                  pl.BlockSpec((None, None, GD_CHUNK, 128), lambda d, bi, c: (d, bi, cidx(d, c), 0)),
                  pl.BlockSpec((None, None, None, 16, GD_CHUNK), lambda d, bi, c: (d, bi, cidx(d, c), 0, 0)),
                  pl.BlockSpec((None, None, GD_HEADS, GD_DK, GD_DK), lambda d, bi, c: (d, bi, 0, 0, 0))],
        out_specs=[pl.BlockSpec((None, None, GD_CHUNK, w), lambda d, bi, c: (d, bi, cidx(d, c), 0)),
                   pl.BlockSpec((None, None, GD_HEADS, GD_DK, GD_DK), lambda d, bi, c: (d, bi, 0, 0, 0))],
        out_shape=[jax.ShapeDtypeStruct((2, b, length, w), F32),
                   jax.ShapeDtypeStruct((2, b, GD_HEADS, GD_DK, GD_DK), F32)],
        scratch_shapes=[pltpu.VMEM((GD_HEADS, GD_DK, GD_DK), F32)],
        compiler_params=_cparams("parallel", "parallel", "arbitrary"),
        name="gdn_scan",
    )(q, k, v, gb, gbt, s0)


def _gdn_inputs(x, scale, shift, rows_per_group, batch, length, p, j):
    m, d = x.shape
    wd = GD_HEADS * GD_DK
    in_w = p["gd_in_w"][j]
    u = _mm(x, in_w[:, :4 * wd].astype(BF16), mod=(scale, shift), rows_per_group=rows_per_group)
    w_ab = jnp.pad(in_w[:, 4 * wd:], ((0, 0), (0, 128 - 4 * GD_HEADS))).astype(BF16)
    ab = _mm(x, w_ab, mod=(scale, shift), rows_per_group=rows_per_group)
    qscale = GD_DK ** -0.5
    post = lambda qc, kc, vc: (_head_l2(_silu(qc), qscale), _head_l2(_silu(kc), 1.0), _silu(vc))
    q, k, v = _conv3(u, p["gd_conv"][j], [0, wd, 2 * wd], 512, length, post, 3, name="gd_conv")
    nh = GD_HEADS
    a_par = jnp.zeros((1, 1, 128), F32).at[0, 0, :2 * nh].set(-jnp.exp(p["gd_a_log"][j]).reshape(-1))
    dt_par = jnp.zeros((1, 1, 128), F32).at[0, 0, :2 * nh].set(p["gd_dt_bias"][j].reshape(-1))

    def gate_fn(abt, an, dtb):
        pre = abt + dtb
        sp = jnp.maximum(pre, 0.0) + jnp.log(1.0 + jnp.exp(-jnp.abs(pre)))
        lane = lax.broadcasted_iota(jnp.int32, abt.shape, 1)
        return (jnp.where(lane < 2 * nh, an * sp, jax.nn.sigmoid(abt)),)

    gall = _rowwise(gate_fn, [ab], [a_par, dt_par], [128], name="gd_gate")[0]
    pad = jnp.zeros((m, 128 - 2 * nh), F32)
    gb = jnp.stack([jnp.concatenate([gall[:, dr * nh:(dr + 1) * nh],
                                     gall[:, (2 + dr) * nh:(3 + dr) * nh], pad], -1) for dr in range(2)])
    gb = gb.reshape(2, batch, length, 128)
    gbt = jnp.swapaxes(gb[..., :2 * nh].reshape(2, batch, length // GD_CHUNK, GD_CHUNK, 2 * nh), -1, -2)
    rs = lambda t: t.reshape(batch, length, wd)
    return rs(q), rs(k), rs(v), gb, gbt, u


def _gdn_out(o2, u, norm_g, out_w):
    m, wd = o2.shape[1], o2.shape[2]
    ng = jnp.tile(norm_g, wd // norm_g.shape[0]).reshape(1, 1, wd)

    def body(of_ref, ob_ref, z_ref, ng_ref, o_ref):
        o = of_ref[...] + ob_ref[...]
        outs = []
        for h in range(wd // GD_DK):
            oh = o[:, h * GD_DK:(h + 1) * GD_DK]
            outs.append(oh * lax.rsqrt(jnp.mean(oh * oh, -1, keepdims=True) + 1e-6))
        o_ref[...] = jnp.concatenate(outs, -1) * ng_ref[...] * _silu(z_ref[...])

    tm = math.gcd(512, m)
    g = pl.pallas_call(
        body,
        grid=(m // tm,),
        in_specs=[pl.BlockSpec((None, tm, wd), lambda i: (0, i, 0)),
                  pl.BlockSpec((None, tm, wd), lambda i: (1, i, 0)),
                  pl.BlockSpec((tm, wd), lambda i: (i, 3)),
                  pl.BlockSpec((None, 1, wd), lambda i: (0, 0, 0))],
        out_specs=pl.BlockSpec((tm, wd), lambda i: (i, 0)),
        out_shape=jax.ShapeDtypeStruct((m, wd), F32),
        compiler_params=_cparams("parallel"),
        name="gd_norm",
    )(o2, o2, u, ng)
    return _mm(g, out_w.astype(BF16))


def _topk_rows(s, n_take, val_ref, idx_ref, base, rid=None):
    if rid is None:
        rid = lax.broadcasted_iota(jnp.int32, s.shape, 0)
    for t in range(n_take):
        mx = jnp.max(s, 0, keepdims=True)
        am = jnp.min(jnp.where(s == mx, rid, jnp.iinfo(jnp.int32).max), 0, keepdims=True)
        val_ref[base + t:base + t + 1, :] = mx
        idx_ref[base + t:base + t + 1, :] = am
        s = jnp.where(rid == am, -jnp.inf, s)


def _pk_topk_body(q_ref, keys_ref, i1_ref, i2_ref, gate_ref, sv_scr, si_scr, cv_scr, ci_scr,
                  i1p_scr, i2p_scr, gp_scr):
    kk = PK_TOPK
    tt = q_ref.shape[0]
    for h in range(PK_HEADS):
        for p in range(2):
            qh = q_ref[:, (2 * h + p) * PK_DH:(2 * h + p + 1) * PK_DH].astype(BF16)
            st = lax.dot_general(keys_ref[h, p], qh, (((1,), (1,)), ((), ())),
                                 preferred_element_type=F32)
            _topk_rows(st, kk, sv_scr, si_scr, p * kk)
        sv1, sv2 = sv_scr[0:kk, :], sv_scr[kk:2 * kk, :]
        si1, si2 = si_scr[0:kk, :], si_scr[kk:2 * kk, :]
        r8 = lax.broadcasted_iota(jnp.int32, (8, tt), 0)
        cand = jnp.concatenate([sv1[0:8, :] + sv2[r2:r2 + 1, :] for r2 in range(8)]
                               + [sv1[8:kk, :] + sv2[0:1, :], sv1[0:1, :] + sv2[8:kk, :]], 0)
        cid = jnp.concatenate([r8 * kk + r2 for r2 in range(8)] + [(r8 + 8) * kk, r8 + 8], 0)
        _topk_rows(cand, kk, cv_scr, ci_scr, 0, cid)
        cv, ci = cv_scr[...], ci_scr[...]
        a1, a2 = lax.shift_right_logical(ci, int(math.log2(kk))), ci & (kk - 1)
        i1 = jnp.zeros((kk, tt), jnp.int32)
        i2 = jnp.zeros((kk, tt), jnp.int32)
        for r in range(kk):
            i1 = i1 + jnp.where(a1 == r, si1[r:r + 1, :], 0)
            i2 = i2 + jnp.where(a2 == r, si2[r:r + 1, :], 0)
        i1p_scr[h * kk:(h + 1) * kk, :] = i1
        i2p_scr[h * kk:(h + 1) * kk, :] = i2
        e = jnp.exp(cv - jnp.max(cv, 0, keepdims=True))
        gp_scr[h * kk:(h + 1) * kk, :] = e / jnp.sum(e, 0, keepdims=True)
    i1_ref[...] = i1p_scr[...].T
    i2_ref[...] = i2p_scr[...].T
    gate_ref[...] = gp_scr[...].T


def _pk_topk(q, keys_bf16):
    m = q.shape[0]
    tt = PK_TOK
    hk = PK_HEADS * PK_TOPK
    assert hk == tt
    return pl.pallas_call(
        _pk_topk_body,
        grid=(m // tt,),
        in_specs=[pl.BlockSpec((tt, q.shape[1]), lambda i: (i, 0)),
                  pl.BlockSpec(keys_bf16.shape, lambda i: (0, 0, 0, 0))],
        out_specs=[pl.BlockSpec((tt, hk), lambda i: (i, 0))] * 3,
        out_shape=[jax.ShapeDtypeStruct((m, hk), jnp.int32), jax.ShapeDtypeStruct((m, hk), jnp.int32),
                   jax.ShapeDtypeStruct((m, hk), F32)],
        scratch_shapes=[pltpu.VMEM((2 * PK_TOPK, tt), F32), pltpu.VMEM((2 * PK_TOPK, tt), jnp.int32),
                        pltpu.VMEM((PK_TOPK, tt), F32), pltpu.VMEM((PK_TOPK, tt), jnp.int32),
                        pltpu.VMEM((hk, tt), jnp.int32), pltpu.VMEM((hk, tt), jnp.int32),
                        pltpu.VMEM((hk, tt), F32)],
        compiler_params=_cparams("parallel"),
        name="pk_topk",
    )(q, keys_bf16)


def _pk_dense_body(x_ref, sc_ref, sh_ref, gt_ref, lg_ref, lb_ref, i1_ref, i2_ref, gate_ref, ut_ref, v_ref,
                   o_ref, hb_scr, act_scr, w_scr, grid_scr, acc_scr, *, alpha, n_chunks):
    e = pl.program_id(1)
    tm = x_ref.shape[0]
    nk = PK_NKEYS
    per = ut_ref.shape[1] // nk

    @pl.when(e == 0)
    def _():
        hb_scr[...] = (x_ref[...] * (1.0 + sc_ref[...]) + sh_ref[...]).astype(BF16)
        act_scr[...] = jnp.zeros_like(act_scr)

    @pl.when(e < n_chunks)
    def _():
        s = _dot(hb_scr[...], ut_ref[...])
        i1 = i1_ref[...]
        i2 = i2_ref[...]
        act = act_scr[...]
        for k in range(per):
            got = jnp.take_along_axis(s[:, k * nk:(k + 1) * nk], i2, axis=1)
            act = jnp.where(i1 == e * per + k, got, act)
        act_scr[...] = act

    @pl.when(e == n_chunks)
    def _():
        w_scr[...] = jax.nn.gelu(act_scr[...]) * gate_ref[...]
        acc_scr[...] = jnp.zeros_like(acc_scr)
        sub = lax.broadcasted_iota(jnp.int32, (nk, nk), 0)

        def tok(tg, carry):
            for u in range(PK_UNROLL):
                t = tg * PK_UNROLL + u
                wrow = w_scr[pl.ds(t, 1), :]
                at = jnp.where(sub == i1_ref[pl.ds(t, 1), :], wrow, 0.0).astype(BF16)
                bt = (sub == i2_ref[pl.ds(t, 1), :]).astype(BF16)
                grid_scr[t] = lax.dot_general(at, bt, (((1,), (1,)), ((), ())), preferred_element_type=F32)
            return carry

        lax.fori_loop(0, tm // PK_UNROLL, tok, 0)

    @pl.when(e >= n_chunks)
    def _():
        c = e - n_chunks
        lhs = jnp.concatenate([grid_scr[:, c * per + k, :] for k in range(per)], -1).astype(BF16)
        acc_scr[...] += _dot(lhs, v_ref[...])

    @pl.when(e == 2 * n_chunks - 1)
    def _():
        o_ref[...] = _ln(alpha * x_ref[...] + gt_ref[...] * acc_scr[...], lg_ref[...], lb_ref[...])


def _pk_dense(x, scale, shift, gate_vec, ln_g, ln_b, i1, i2, gate, ut_bf16, v_bf16, rows_per_group, alpha):
    m, d = x.shape
    n_exp = v_bf16.shape[0]
    tm = math.gcd(PK_TM, m, rows_per_group)
    n_chunks = n_exp // PK_EC
    g = rows_per_group // tm
    assert n_exp % PK_EC == 0 and PK_EC % PK_NKEYS == 0
    row = lambda i, e: (i, 0)
    vec = lambda i, e: (i // g, 0, 0)
    one = lambda i, e: (0, 0, 0)
    npk = i1.shape[1]
    return pl.pallas_call(
        functools.partial(_pk_dense_body, alpha=alpha, n_chunks=n_chunks),
        grid=(m // tm, 2 * n_chunks),
        in_specs=[pl.BlockSpec((tm, d), row),
                  pl.BlockSpec((None, 1, d), vec), pl.BlockSpec((None, 1, d), vec), pl.BlockSpec((None, 1, d), vec),
                  pl.BlockSpec((None, 1, d), one), pl.BlockSpec((None, 1, d), one),
                  pl.BlockSpec((tm, npk), row), pl.BlockSpec((tm, npk), row), pl.BlockSpec((tm, npk), row),
                  pl.BlockSpec((d, PK_EC), lambda i, e: (0, jnp.minimum(e, n_chunks - 1))),
                  pl.BlockSpec((PK_EC, d), lambda i, e: (jnp.maximum(e - n_chunks, 0), 0))],
        out_specs=pl.BlockSpec((tm, d), row),
        out_shape=jax.ShapeDtypeStruct((m, d), F32),
        scratch_shapes=[pltpu.VMEM((tm, d), BF16), pltpu.VMEM((tm, npk), F32), pltpu.VMEM((tm, npk), F32),
                        pltpu.VMEM((tm, PK_NKEYS, PK_NKEYS), F32), pltpu.VMEM((tm, d), F32)],
        compiler_params=_cparams("parallel", "arbitrary"),
        name="pk_dense",
    )(x, scale, shift, gate_vec, ln_g, ln_b, i1, i2, gate, ut_bf16, v_bf16)


def _peer_ln(x, scale, shift, gate_vec, ln_g, ln_b, rows_per_group, alpha, wq_bf16, keys_bf16, ut_bf16, v_bf16):
    q = _mm(x, wq_bf16, mod=(scale, shift), rows_per_group=rows_per_group)
    i1, i2, gate = _pk_topk(q, keys_bf16)
    return _pk_dense(x, scale, shift, gate_vec, ln_g, ln_b, i1, i2, gate, ut_bf16, v_bf16, rows_per_group, alpha)


def kernel(x, c, ctx, c_ctx, ada_w, ada_b, ln_g, ln_b, pk_wq, pk_keys, pk_u, pk_v, hy_in_w, hy_in_b, hy_conv, hy_f_w1, hy_f_b1, hy_f_w2, hy_f_b2, hy_f_w3, hy_skip, hy_out_w, hy_out_b, gd_in_w, gd_conv, gd_a_log, gd_dt_bias, gd_norm_g, gd_out_w, fn_out_w, fn_out_b):
    p = dict(hy_in_w=hy_in_w, hy_in_b=hy_in_b, hy_conv=hy_conv, hy_f_w1=hy_f_w1, hy_f_b1=hy_f_b1,
             hy_f_w2=hy_f_w2, hy_f_b2=hy_f_b2, hy_f_w3=hy_f_w3, hy_skip=hy_skip, hy_out_w=hy_out_w,
             hy_out_b=hy_out_b, gd_in_w=gd_in_w, gd_conv=gd_conv, gd_a_log=gd_a_log, gd_dt_bias=gd_dt_bias,
             fn_out_w=fn_out_w, fn_out_b=fn_out_b)
    b, length, d = x.shape
    lc = ctx.shape[1]
    depth = ada_w.shape[0]
    alpha = (2 * depth) ** 0.25
    xl = _pos_add(x)
    xc = ctx.reshape(b * lc, d)
    gdn_layers = [i for i in range(depth) if i % N_MIXERS == 1]
    ctx_until = gdn_layers[-1] if gdn_layers else -1
    cond = jnp.concatenate([c, c_ctx[None], jnp.zeros((8 - b - 1, d), F32)], 0)
    for i in range(depth):
        kind, j = i % N_MIXERS, i // N_MIXERS
        ctx_in, ctx_out = i <= ctx_until, i < ctx_until
        mod = _mm(cond, ada_w[i].astype(BF16), bias=ada_b[i], silu_in=True, tn=2048)
        mod = mod.reshape(8, N_MOD, 1, d)
        ml = [mod[:b, t] for t in range(N_MOD)]
        mc = [mod[b:b + 1, t] for t in range(N_MOD)]
        lg = [ln_g[i, t].reshape(1, 1, d) for t in range(2)]
        lb = [ln_b[i, t].reshape(1, 1, d) for t in range(2)]
        yc = None
        if kind == 0:
            yl = _hyena(xl, ml[1], ml[0], length, length, p, j)
            if ctx_out:
                yc = _hyena(xc, mc[1], mc[0], b * lc, lc, p, j)
        elif kind == 1:
            qc, kc, vc, gbc, gbtc, uc = _gdn_inputs(xc, mc[1], mc[0], b * lc, b, lc, p, j)
            ql, kl, vl, gbl, gbtl, ul = _gdn_inputs(xl, ml[1], ml[0], length, b, length, p, j)
            s0 = jnp.zeros((2, b, GD_HEADS, GD_DK, GD_DK), F32)
            oc, s_ctx = _gdn_scan(qc, kc, vc, gbc, gbtc, s0)
            ol, _ = _gdn_scan(ql, kl, vl, gbl, gbtl, s_ctx)
            yl = _gdn_out(ol.reshape(2, b * length, -1), ul, gd_norm_g[j], gd_out_w[j])
            if ctx_out:
                yc = _gdn_out(oc.reshape(2, b * lc, -1), uc, gd_norm_g[j], gd_out_w[j])
        else:
            yl = _fnet(xl, ml[1], ml[0], length, b, length, p, j)
            if ctx_out:
                yc = _fnet(xc, mc[1], mc[0], b * lc, b, lc, p, j)
        wq = pk_wq[i].astype(BF16)
        keys = pk_keys[i].astype(BF16)
        ut, vt = pk_u[i].astype(BF16).T, pk_v[i].astype(BF16)
        xl = _res_ln(xl, yl, ml[2], lg[0], lb[0], alpha, length)
        xl = _peer_ln(xl, ml[4], ml[3], ml[5], lg[1], lb[1], length, alpha, wq, keys, ut, vt)
        if ctx_out:
            xc = _res_ln(xc, yc, mc[2], lg[0], lb[0], alpha, b * lc)
            xc = _peer_ln(xc, mc[4], mc[3], mc[5], lg[1], lb[1], b * lc, alpha, wq, keys, ut, vt)
    return xl.reshape(b, length, d)
```

```python
import functools
import math

import numpy as np
import jax
import jax.numpy as jnp
from jax import lax
from jax.experimental import pallas as pl
from jax.experimental.pallas import tpu as pltpu

F32 = jnp.float32
BF16 = jnp.bfloat16

GRID_W = 64
N_MIXERS = 3
N_MOD = 6
LN_EPS = 1e-5
HY_EMB = 33
HY_BANDS = (HY_EMB - 1) // 2
HY_SHIFT = 0.05
HY_TARGET = 1e-2
HY_MIN_DECAY = math.log(HY_TARGET) / 1.5
HY_MAX_DECAY = math.log(HY_TARGET) / 0.3
GD_HEADS = 8
GD_DK = 128
GD_CHUNK = 64
FN_GROUPS = 4
PK_HEADS = 8
PK_NKEYS = 128
PK_DH = 128
PK_TOPK = 16
PK_TOK = 128
PK_TM = 256
PK_EC = 2048
PK_UNROLL = 16

VMEM_LIMIT_BYTES = 56 * 1024 * 1024


def _cparams(*sem):
    return pltpu.CompilerParams(dimension_semantics=sem, vmem_limit_bytes=VMEM_LIMIT_BYTES)


def _dot(a, b):
    return jnp.dot(a, b, preferred_element_type=F32)


def _dot_hi(a, b):
    return jnp.dot(a, b, preferred_element_type=F32, precision=lax.Precision.HIGHEST)


def _split(a):
    hi = a.astype(BF16)
    lo = (a - hi.astype(F32)).astype(BF16)
    return hi, lo


def _dot3(a, b):
    ah, al = _split(a)
    bh, bl = _split(b)
    return _dot(ah, bh) + (_dot(ah, bl) + _dot(al, bh))


def _mm_body(*refs, has_mod, has_bias, silu_in):
    a_ref, w_ref = refs[0], refs[1]
    k = 2
    a = a_ref[...]
    if has_mod:
        a = a * (1.0 + refs[k][...]) + refs[k + 1][...]
        k += 2
    if silu_in:
        a = a * jax.nn.sigmoid(a)
    o = _dot(a.astype(BF16), w_ref[...])
    if has_bias:
        o = o + refs[k][...]
        k += 1
    refs[k][...] = o


def _mm(a, w_bf16, bias=None, mod=None, rows_per_group=None, silu_in=False, tm=512, tn=None):
    m, k = a.shape
    n = w_bf16.shape[1]
    tm = math.gcd(tm, m, rows_per_group or m)
    if tn is None:
        tn = max(t for t in range(128, min(n, 2048) + 1, 128) if n % t == 0)
    assert m % tm == 0 and n % tn == 0
    ins = [a, w_bf16]
    specs = [pl.BlockSpec((tm, k), lambda j, i: (i, 0)), pl.BlockSpec((k, tn), lambda j, i: (0, j))]
    if mod is not None:
        assert rows_per_group % tm == 0
        g = rows_per_group // tm
        for v in mod:
            ins.append(v)
            specs.append(pl.BlockSpec((None, 1, k), lambda j, i: (i // g, 0, 0)))
    if bias is not None:
        ins.append(bias.reshape(1, n))
        specs.append(pl.BlockSpec((1, tn), lambda j, i: (0, j)))
    return pl.pallas_call(
        functools.partial(_mm_body, has_mod=mod is not None, has_bias=bias is not None, silu_in=silu_in),
        grid=(n // tn, m // tm),
        in_specs=specs,
        out_specs=pl.BlockSpec((tm, tn), lambda j, i: (i, j)),
        out_shape=jax.ShapeDtypeStruct((m, n), F32),
        compiler_params=_cparams("parallel", "parallel"),
        name="mm",
    )(*ins)


def _rowwise(fn, rows, vecs, out_cols, rows_per_group=None, tm=512, name="rowwise"):
    m = rows[0].shape[0]
    tm = math.gcd(tm, m, rows_per_group or m)
    assert m % tm == 0
    n_r, n_v, n_o = len(rows), len(vecs), len(out_cols)

    def body(*refs):
        outs = fn(*[r[...] for r in refs[:n_r + n_v]])
        for o_ref, o in zip(refs[n_r + n_v:], outs):
            o_ref[...] = o

    specs = [pl.BlockSpec((tm, r.shape[1]), lambda i: (i, 0)) for r in rows]
    for v in vecs:
        if v.shape[0] == 1:
            specs.append(pl.BlockSpec((None, 1, v.shape[2]), lambda i: (0, 0, 0)))
        else:
            assert rows_per_group % tm == 0
            g = rows_per_group // tm
            specs.append(pl.BlockSpec((None, 1, v.shape[2]), lambda i, g=g: (i // g, 0, 0)))
    return pl.pallas_call(
        body,
        grid=(m // tm,),
        in_specs=specs,
        out_specs=[pl.BlockSpec((tm, c), lambda i: (i, 0)) for c in out_cols],
        out_shape=[jax.ShapeDtypeStruct((m, c), F32) for c in out_cols],
        compiler_params=_cparams("parallel"),
        name=name,
    )(*rows, *vecs)


def _ln(v, g, b):
    mu = jnp.mean(v, -1, keepdims=True)
    d = v - mu
    var = jnp.mean(d * d, -1, keepdims=True)
    return d * lax.rsqrt(var + LN_EPS) * g + b


def _res_ln(x, y, gate, ln_g, ln_b, alpha, rows_per_group):
    fn = lambda xt, yt, gt, lg, lb: (_ln(alpha * xt + gt * yt, lg, lb),)
    return _rowwise(fn, [x, y], [gate, ln_g, ln_b], [x.shape[1]], rows_per_group, name="res_ln")[0]


def _pos_add_body(x_ref, er_ref, ec_ref, o_ref):
    half = er_ref.shape[-1]
    x = x_ref[...]
    er = jnp.broadcast_to(er_ref[...], x.shape[:2] + (half,))
    ec = jnp.broadcast_to(ec_ref[...][None], x.shape[:2] + (half,))
    o_ref[...] = x + jnp.concatenate([er, ec], -1)


def _pos_add(x):
    b, length, d = x.shape
    rows = length // GRID_W
    quarter = d // 4
    omega = 1.0 / (10000.0 ** (jnp.arange(quarter, dtype=F32) / quarter))
    er = jnp.arange(rows, dtype=F32)[:, None] * omega
    ec = jnp.arange(GRID_W, dtype=F32)[:, None] * omega
    emb_r = jnp.concatenate([jnp.sin(er), jnp.cos(er)], -1).reshape(rows, 1, d // 2)
    emb_c = jnp.concatenate([jnp.sin(ec), jnp.cos(ec)], -1)
    rt = 8
    x4 = x.reshape(b, rows, GRID_W, d)
    out = pl.pallas_call(
        _pos_add_body,
        grid=(b, rows // rt),
        in_specs=[pl.BlockSpec((None, rt, GRID_W, d), lambda i, j: (i, j, 0, 0)),
                  pl.BlockSpec((rt, 1, d // 2), lambda i, j: (j, 0, 0)),
                  pl.BlockSpec((GRID_W, d // 2), lambda i, j: (0, 0))],
        out_specs=pl.BlockSpec((None, rt, GRID_W, d), lambda i, j: (i, j, 0, 0)),
        out_shape=jax.ShapeDtypeStruct(x4.shape, F32),
        compiler_params=_cparams("parallel", "parallel"),
        name="pos_add",
    )(x4, emb_r, emb_c)
    return out.reshape(b * length, d)


def _conv3_body(*refs, n_parts, tm, seq_len, post, n_vec):
    i = pl.program_id(0)
    first = (i * tm) % seq_len == 0
    last = ((i + 1) * tm) % seq_len == 0
    row = lax.broadcasted_iota(jnp.int32, (tm, 1), 0)
    parts = []
    for p in range(n_parts):
        main_ref, prev_ref, next_ref, w_ref = refs[4 * p:4 * p + 4]
        u = main_ref[...]
        w = w_ref[...]
        prev = jnp.where(first, 0.0, prev_ref[7:8, :])
        nxt = jnp.where(last, 0.0, next_ref[0:1, :])
        up = jnp.where(row == 0, prev, pltpu.roll(u, 1, 0))
        dn = jnp.where(row == tm - 1, nxt, pltpu.roll(u, tm - 1, 0))
        parts.append(up * w[0:1, :] + u * w[1:2, :] + dn * w[2:3, :])
    k = 4 * n_parts
    vecs = [refs[k + j][...] for j in range(n_vec)]
    outs = post(*parts, *vecs)
    for o_ref, o in zip(refs[k + n_vec:], outs):
        o_ref[...] = o


def _conv3(u, w, col_parts, tc, seq_len, post, n_out, vecs=(), tm=256, name="conv3"):
    m = u.shape[0]
    tm = min(tm, seq_len)
    assert seq_len % tm == 0 and m % tm == 0 and tm % 8 == 0
    width = col_parts[1] - col_parts[0] if len(col_parts) > 1 else tc
    ncol = width // tc
    t8 = tm // 8
    nb8 = m // 8
    ins, specs = [], []
    for c0 in col_parts:
        cb = c0 // tc
        ins += [u, u, u, w]
        specs += [
            pl.BlockSpec((tm, tc), lambda i, j, cb=cb: (i, cb + j)),
            pl.BlockSpec((8, tc), lambda i, j, cb=cb: (jnp.maximum(i * t8 - 1, 0), cb + j)),
            pl.BlockSpec((8, tc), lambda i, j, cb=cb: (jnp.minimum((i + 1) * t8, nb8 - 1), cb + j)),
            pl.BlockSpec((3, tc), lambda i, j, cb=cb: (0, cb + j)),
        ]
    for v in vecs:
        ins.append(v)
        specs.append(pl.BlockSpec((1, tc), lambda i, j: (0, j)))
    return pl.pallas_call(
        functools.partial(_conv3_body, n_parts=len(col_parts), tm=tm, seq_len=seq_len, post=post,
                          n_vec=len(vecs)),
        grid=(m // tm, ncol),
        in_specs=specs,
        out_specs=[pl.BlockSpec((tm, tc), lambda i, j: (i, j)) for _ in range(n_out)],
        out_shape=[jax.ShapeDtypeStruct((m, width), F32) for _ in range(n_out)],
        compiler_params=_cparams("parallel", "parallel"),
        name=name,
    )(*ins)


def _lmm_body(*refs, n_x):
    w_ref, o_ref = refs[0], refs[1 + n_x]
    tc = o_ref.shape[-1]
    xs = [r[...].reshape(-1, tc) for r in refs[1:1 + n_x]]
    x = xs[0] if n_x == 1 else jnp.concatenate(xs, 0)
    w = w_ref[...]
    w = w.reshape(w.shape[-2], w.shape[-1])
    o_ref[...] = _dot(w, x.astype(BF16)).reshape(o_ref.shape).astype(o_ref.dtype)


def _lmm(w, w_spec, xs, x_specs, out_shape, out_spec, grid, name, out_dtype=F32):
    return pl.pallas_call(
        functools.partial(_lmm_body, n_x=len(xs)),
        grid=grid,
        in_specs=[w_spec] + list(x_specs),
        out_specs=out_spec,
        out_shape=jax.ShapeDtypeStruct(out_shape, out_dtype),
        compiler_params=_cparams(*(["parallel"] * len(grid))),
        name=name,
    )(w, *xs)


def _spec_mul_body(w1_ref, w2_ref, x_ref, h_ref, o_ref):
    tc = o_ref.shape[-1]
    x = x_ref[...].reshape(-1, tc)
    z = _dot(w1_ref[...], x.astype(BF16))
    half = z.shape[0] // 2
    zr, zi = z[:half], z[half:]
    hr, hi = h_ref[0], h_ref[1]
    y = jnp.concatenate([zr * hr - zi * hi, zr * hi + zi * hr], 0)
    o_ref[...] = _dot(w2_ref[...], y.astype(BF16)).reshape(o_ref.shape).astype(o_ref.dtype)


def _cplx_mat(ang):
    c, s = jnp.cos(ang), jnp.sin(ang)
    return jnp.concatenate([jnp.concatenate([c, -s], -1), jnp.concatenate([s, c], -1)], -2)


def _phase(k, n):
    return (k % n).astype(F32) * (2.0 * math.pi / n)


def _outer_mats(n1, n2, n_in, sign):
    n = n1 * n2
    s1 = jnp.arange(n1, dtype=jnp.int32)[:, None, None]
    f2 = jnp.arange(n2, dtype=jnp.int32)[None, :, None]
    s2 = jnp.arange(n_in, dtype=jnp.int32)[None, None, :]
    return sign * _phase(f2 * (s1 + n1 * s2), n)


def _inner_phase(n1, sign):
    a = jnp.arange(n1, dtype=jnp.int32)
    return sign * _phase(a[:, None] * a[None, :], n1)


def _split_len(n):
    n1 = 1 << (int(math.log2(n)) // 2)
    return n1, n // n1


def _fft_conv_pair(z2, hspec, n1, n2, tcol=512):
    _, length, c = z2.shape
    n = 2 * length
    assert n1 * n2 == n
    h2 = n2 // 2
    m_in = _cplx_mat(_outer_mats(n1, n2, h2, -1.0)).astype(BF16)
    a = _lmm(m_in, pl.BlockSpec((1, 2 * n2, n2), lambda s: (s, 0, 0)),
             [z2.reshape(2, h2, n1 * c)], [pl.BlockSpec((2, h2, c), lambda s: (0, 0, s))],
             (2, n1, n2, c), pl.BlockSpec((2, 1, n2, c), lambda s: (0, s, 0, 0)), (n1,), "fft_in", BF16)
    w1 = _cplx_mat(_inner_phase(n1, -1.0)).astype(BF16)
    w2 = (_cplx_mat(_inner_phase(n1, 1.0)) * (1.0 / n)).astype(BF16)
    cols = n2 * c
    tcol = min(tcol, cols)
    b = pl.pallas_call(
        _spec_mul_body,
        grid=(cols // tcol,),
        in_specs=[pl.BlockSpec((2 * n1, 2 * n1), lambda j: (0, 0)),
                  pl.BlockSpec((2 * n1, 2 * n1), lambda j: (0, 0)),
                  pl.BlockSpec((2, n1, tcol), lambda j: (0, 0, j)),
                  pl.BlockSpec((2, n1, tcol), lambda j: (0, 0, j))],
        out_specs=pl.BlockSpec((2, n1, tcol), lambda j: (0, 0, j)),
        out_shape=jax.ShapeDtypeStruct((2, n1, cols), BF16),
        compiler_params=_cparams("parallel"),
        name="fft_mid",
    )(w1, w2, a.reshape(2, n1, cols), hspec.reshape(2, n1, cols))
    m_out = _cplx_mat(jnp.swapaxes(_outer_mats(n1, n2, h2, 1.0), 1, 2)).astype(BF16)
    y = _lmm(m_out, pl.BlockSpec((1, n2, 2 * n2), lambda s: (s, 0, 0)),
             [b.reshape(2, n1, n2, c)], [pl.BlockSpec((2, 1, n2, c), lambda s: (0, s, 0, 0))],
             (2, h2, n1 * c), pl.BlockSpec((2, h2, c), lambda s: (0, 0, s)), (n1,), "fft_out")
    return y.reshape(2, length, c)


def _fft_real_spectrum(f, n1, n2, tcol=512):
    n, c = f.shape
    if n2 == 1:
        ph = _inner_phase(n1, -1.0)
        w = jnp.concatenate([jnp.cos(ph), jnp.sin(ph)], 0).astype(BF16)
        tcol = min(tcol, c)
        return _lmm(w, pl.BlockSpec((2 * n1, n1), lambda j: (0, 0)),
                    [f], [pl.BlockSpec((n1, tcol), lambda j: (0, j))],
                    (2, n1, c), pl.BlockSpec((2, n1, tcol), lambda j: (0, 0, j)), (c // tcol,), "fft_spec1")
    ph = _outer_mats(n1, n2, n2, -1.0)
    m_in = jnp.concatenate([jnp.cos(ph), jnp.sin(ph)], 1).astype(BF16)
    a = _lmm(m_in, pl.BlockSpec((1, 2 * n2, n2), lambda s: (s, 0, 0)),
             [f.reshape(n2, n1 * c)], [pl.BlockSpec((n2, c), lambda s: (0, s))],
             (2, n1, n2, c), pl.BlockSpec((2, 1, n2, c), lambda s: (0, s, 0, 0)), (n1,), "fft_spec_in", BF16)
    w1 = _cplx_mat(_inner_phase(n1, -1.0)).astype(BF16)
    cols = n2 * c
    tcol = min(tcol, cols)
    h = _lmm(w1, pl.BlockSpec((2 * n1, 2 * n1), lambda j: (0, 0)),
             [a.reshape(2, n1, cols)], [pl.BlockSpec((2, n1, tcol), lambda j: (0, 0, j))],
             (2, n1, cols), pl.BlockSpec((2, n1, tcol), lambda j: (0, 0, j)), (cols // tcol,), "fft_spec_mid")
    return h.reshape(2, n, c)


def _hy_filter_body(wt_ref, wc_ref, ws_ref, b1_ref, w2_ref, b2_ref, w3_ref, bands_ref, dl_ref, o_ref,
                    *, length, tr):
    d = o_ref.shape[-1]
    j = pl.program_id(0) * tr + lax.broadcasted_iota(jnp.int32, (tr, 1), 0)
    k = jnp.where(j < length, j, 2 * length - j)
    t = k.astype(F32) / length
    ang = 2.0 * jnp.pi * t * bands_ref[...]
    pre = t * wt_ref[...] + _dot_hi(jnp.cos(ang), wc_ref[...]) + _dot_hi(-jnp.sin(ang), ws_ref[...])
    hdn = jnp.sin(pre + b1_ref[...])
    hdn = jnp.sin(_dot_hi(hdn, w2_ref[...]) + b2_ref[...])
    hf = _dot_hi(hdn, w3_ref[...])
    win = jnp.exp(-t * dl_ref[...]) + HY_SHIFT
    h = jnp.where(j < length, hf[:, :d], hf[:, d:]) * win
    o_ref[...] = jnp.where(j == length, 0.0, h)


def _hy_filter(length, f_w1, f_b1, f_w2, f_b2, f_w3):
    d = f_w3.shape[1] // 2
    ffn = f_w2.shape[0]
    tr = min(512, length)
    bands = jnp.linspace(1e-4, HY_BANDS - 1, HY_BANDS, dtype=F32).reshape(1, HY_BANDS)
    deltas = jnp.abs(jnp.linspace(HY_MIN_DECAY, HY_MAX_DECAY, d, dtype=F32)).reshape(1, d)
    ins = [f_w1[0:1], f_w1[1:1 + HY_BANDS], f_w1[1 + HY_BANDS:], f_b1.reshape(1, ffn), f_w2,
           f_b2.reshape(1, ffn), f_w3, bands, deltas]
    return pl.pallas_call(
        functools.partial(_hy_filter_body, length=length, tr=tr),
        grid=(2 * length // tr,),
        in_specs=[pl.BlockSpec(a.shape, lambda i: (0, 0)) for a in ins],
        out_specs=pl.BlockSpec((tr, d), lambda i: (i, 0)),
        out_shape=jax.ShapeDtypeStruct((2 * length, d), F32),
        compiler_params=_cparams("parallel"),
        name="hy_filter",
    )(*ins)


def _hyena(x, scale, shift, rows_per_group, length, p, j):
    m, d = x.shape
    assert m == 2 * length
    u = _mm(x, p["hy_in_w"][j].astype(BF16), bias=p["hy_in_b"][j], mod=(scale, shift),
            rows_per_group=rows_per_group)
    post = lambda x0, x1, v: (x0, v * x1)
    x0c, z = _conv3(u, p["hy_conv"][j], [0, d, 2 * d], 512, length, post, 2, name="hy_conv")
    filt = _hy_filter(length, p["hy_f_w1"][j], p["hy_f_b1"][j], p["hy_f_w2"][j], p["hy_f_b2"][j],
                      p["hy_f_w3"][j])
    n = 2 * length
    n1, n2 = (n, 1) if n <= 1024 else _split_len(n)
    hspec = _fft_real_spectrum(filt, n1, n2)
    if n2 == 1:
        z2 = jnp.pad(z.reshape(2, length, d), ((0, 0), (0, length), (0, 0)))
        y = _fft_conv_pair_single(z2, hspec, n)[:, :length]
    else:
        y = _fft_conv_pair(z.reshape(2, length, d), hspec, n1, n2)
    fn = lambda yt, zt, x0t, sk: ((yt + sk * zt) * x0t,)
    g = _rowwise(fn, [y.reshape(m, d), z, x0c], [p["hy_skip"][j].reshape(1, 1, d)], [d], name="hy_gate")[0]
    return _mm(g, p["hy_out_w"][j].astype(BF16), bias=p["hy_out_b"][j])


def _fft_conv_pair_single(z2, hspec, n, tcol=512):
    c = z2.shape[-1]
    w1 = _cplx_mat(_inner_phase(n, -1.0)).astype(BF16)
    w2 = (_cplx_mat(_inner_phase(n, 1.0)) * (1.0 / n)).astype(BF16)
    tcol = min(tcol, c)
    return pl.pallas_call(
        _spec_mul_body,
        grid=(c // tcol,),
        in_specs=[pl.BlockSpec((2 * n, 2 * n), lambda j: (0, 0)),
                  pl.BlockSpec((2 * n, 2 * n), lambda j: (0, 0)),
                  pl.BlockSpec((2, n, tcol), lambda j: (0, 0, j)),
                  pl.BlockSpec((2, n, tcol), lambda j: (0, 0, j))],
        out_specs=pl.BlockSpec((2, n, tcol), lambda j: (0, 0, j)),
        out_shape=jax.ShapeDtypeStruct((2, n, c), F32),
        compiler_params=_cparams("parallel"),
        name="fft_mid1",
    )(w1, w2, z2, hspec)


def _fnet(x, scale, shift, rows_per_group, batch, length, p, j):
    m, d = x.shape
    gc = d // FN_GROUPS
    ph = _inner_phase(gc, -1.0)
    eye = jnp.eye(FN_GROUPS, dtype=F32)
    w_c = jnp.concatenate([jnp.kron(eye, jnp.cos(ph)), jnp.kron(eye, jnp.sin(ph))], 1).astype(BF16)
    w = _mm(x, w_c, mod=(scale, shift), rows_per_group=rows_per_group)
    if length <= 1024:
        n1, n2 = length, 1
    else:
        n1, n2 = _split_len(length)
    norm = 1.0 / math.sqrt(length * gc)
    if n2 == 1:
        ph1 = _inner_phase(n1, -1.0)
        wr = (jnp.concatenate([jnp.cos(ph1), -jnp.sin(ph1)], 1) * norm).astype(BF16)
        y = _lmm(wr, pl.BlockSpec((n1, 2 * n1), lambda b, c: (0, 0)),
                 [w.reshape(batch, n1, 2 * d)] * 2,
                 [pl.BlockSpec((None, n1, d), lambda b, c: (b, 0, 0)),
                  pl.BlockSpec((None, n1, d), lambda b, c: (b, 0, 1))],
                 (batch, n1, d), pl.BlockSpec((None, n1, d), lambda b, c: (b, 0, 0)), (batch, 1), "fn_pos1")
        y = y.reshape(m, d)
    else:
        m_in = _cplx_mat(_outer_mats(n1, n2, n2, -1.0)).astype(BF16)
        wv = w.reshape(batch, n2, n1 * 2 * d)
        a = _lmm(m_in, pl.BlockSpec((1, 2 * n2, 2 * n2), lambda b, s: (s, 0, 0)),
                 [wv, wv],
                 [pl.BlockSpec((None, n2, d), lambda b, s: (b, 0, 2 * s)),
                  pl.BlockSpec((None, n2, d), lambda b, s: (b, 0, 2 * s + 1))],
                 (batch, 2, n1, n2, d), pl.BlockSpec((None, 2, 1, n2, d), lambda b, s: (b, 0, s, 0, 0)),
                 (batch, n1), "fn_pos_in", BF16)
        ph1 = _inner_phase(n1, -1.0)
        wr = (jnp.concatenate([jnp.cos(ph1), -jnp.sin(ph1)], 1) * norm).astype(BF16)
        cols = n2 * d
        tcol = 1024
        y = _lmm(wr, pl.BlockSpec((n1, 2 * n1), lambda b, c: (0, 0)),
                 [a.reshape(batch, 2, n1, cols)],
                 [pl.BlockSpec((None, 2, n1, tcol), lambda b, c: (b, 0, 0, c))],
                 (batch, n1, cols), pl.BlockSpec((None, n1, tcol), lambda b, c: (b, 0, c)),
                 (batch, cols // tcol), "fn_pos_mid")
        y = y.reshape(m, d)
    return _mm(y, p["fn_out_w"][j].astype(BF16), bias=p["fn_out_b"][j])


def _head_l2(t, extra):
    outs = []
    for h in range(t.shape[1] // GD_DK):
        th = t[:, h * GD_DK:(h + 1) * GD_DK]
        outs.append(th * (lax.rsqrt(jnp.sum(th * th, -1, keepdims=True) + 1e-6) * extra))
    return jnp.concatenate(outs, -1)


def _silu(v):
    return v * jax.nn.sigmoid(v)


def _gdn_scan_body(q_ref, k_ref, v_ref, gb_ref, gbt_ref, s0_ref, o_ref, sfin_ref, s_scr, *, n_chunks):
    direction = pl.program_id(0)
    c = pl.program_id(2)
    cs = GD_CHUNK

    @pl.when(c == 0)
    def _():
        s_scr[...] = s0_ref[...]

    ri = lax.broadcasted_iota(jnp.int32, (cs, cs), 0)
    ci = lax.broadcasted_iota(jnp.int32, (cs, cs), 1)
    lag = (ri - ci) * (1 - 2 * direction)
    incl = lag >= 0
    strict = lag > 0
    tri = incl.astype(F32)
    tri_t = (lag <= 0).astype(F32)
    eye = (ri == ci).astype(F32)
    pair_masks = []
    for lvl in range(int(math.log2(cs))):
        rb, cb = lax.shift_right_logical(ri, lvl), lax.shift_right_logical(ci, lvl)
        pair_masks.append((jnp.abs(rb - cb) == 1) & ((jnp.minimum(rb, cb) & 1) == 0))
    gb = gb_ref[...]
    gbt = gbt_ref[...]
    gc_cols = _dot_hi(tri, gb)
    gc_rows = _dot_hi(gbt, tri_t)
    tot = jnp.sum(gb, 0, keepdims=True)
    hs = range(GD_HEADS)
    nt = (((1,), (1,)), ((), ()))
    tn = (((0,), (0,)), ((), ()))
    sl = [slice(h * GD_DK, (h + 1) * GD_DK) for h in hs]
    k = [k_ref[:, sl[h]] for h in hs]
    gcol = [gc_cols[:, h:h + 1] for h in hs]
    beta = [gb[:, GD_HEADS + h:GD_HEADS + h + 1] for h in hs]
    gtot = [tot[:, h:h + 1] for h in hs]
    decay = [jnp.exp(jnp.where(incl, gcol[h] - gc_rows[h:h + 1, :], -jnp.inf)) for h in hs]
    eg = [jnp.exp(gcol[h]) for h in hs]
    kb = [k[h] * beta[h] for h in hs]
    kbf = [k[h].astype(BF16) for h in hs]
    kk = [lax.dot_general(kb[h].astype(BF16), kbf[h], nt, preferred_element_type=F32) for h in hs]
    a = [jnp.where(strict, kk[h] * decay[h], 0.0) for h in hs]
    inv = [eye - jnp.where(pair_masks[0], a[h], 0.0) for h in hs]
    for pm in pair_masks[1:]:
        tn_s = [_dot3(inv[h], jnp.where(pm, a[h], 0.0)) for h in hs]
        tnt = [_dot3(tn_s[h], inv[h]) for h in hs]
        inv = [inv[h] - tnt[h] for h in hs]
    rhs = [jnp.concatenate([v_ref[:, sl[h]] * beta[h], kb[h] * eg[h]], -1) for h in hs]
    sol = [_dot3(inv[h], rhs[h]) for h in hs]
    q = [q_ref[:, sl[h]] for h in hs]
    qk = [lax.dot_general(q[h].astype(BF16), kbf[h], nt, preferred_element_type=F32) for h in hs]
    sb = [s_scr[h].astype(BF16) for h in hs]
    ws = [_dot(sol[h][:, GD_DK:].astype(BF16), sb[h]) for h in hs]
    qs = [_dot((q[h] * eg[h]).astype(BF16), sb[h]) for h in hs]
    vnb = [(sol[h][:, :GD_DK] - ws[h]).astype(BF16) for h in hs]
    av = [_dot((qk[h] * decay[h]).astype(BF16), vnb[h]) for h in hs]
    kv = [lax.dot_general((k[h] * jnp.exp(gtot[h] - gcol[h])).astype(BF16), vnb[h], tn,
                          preferred_element_type=F32) for h in hs]
    for h in hs:
        s_scr[h] = s_scr[h] * jnp.exp(gtot[h]) + kv[h]
    o_ref[...] = jnp.concatenate([qs[h] + av[h] for h in hs], -1)

    @pl.when(c == n_chunks - 1)
    def _():
        sfin_ref[...] = s_scr[...]


def _gdn_scan(q, k, v, gb, gbt, s0):
    b, length, w = q.shape
    n_chunks = length // GD_CHUNK
    cidx = lambda d, c: c + d * (n_chunks - 1 - 2 * c)
    seq = lambda d, bi, c: (bi, cidx(d, c), 0)
    return pl.pallas_call(
        functools.partial(_gdn_scan_body, n_chunks=n_chunks),
        grid=(2, b, n_chunks),
        in_specs=[pl.BlockSpec((None, GD_CHUNK, w), seq),
                  pl.BlockSpec((None, GD_CHUNK, w), seq),
                  pl.BlockSpec((None, GD_CHUNK, w), seq),
                  pl.BlockSpec((None, None, GD_CHUNK, 128), lambda d, bi, c: (d, bi, cidx(d, c), 0)),
                  pl.BlockSpec((None, None, None, 16, GD_CHUNK), lambda d, bi, c: (d, bi, cidx(d, c), 0, 0)),
                  pl.BlockSpec((None, None, GD_HEADS, GD_DK, GD_DK), lambda d, bi, c: (d, bi, 0, 0, 0))],
        out_specs=[pl.BlockSpec((None, None, GD_CHUNK, w), lambda d, bi, c: (d, bi, cidx(d, c), 0)),
                   pl.BlockSpec((None, None, GD_HEADS, GD_DK, GD_DK), lambda d, bi, c: (d, bi, 0, 0, 0))],
        out_shape=[jax.ShapeDtypeStruct((2, b, length, w), F32),
                   jax.ShapeDtypeStruct((2, b, GD_HEADS, GD_DK, GD_DK), F32)],
        scratch_shapes=[pltpu.VMEM((GD_HEADS, GD_DK, GD_DK), F32)],
        compiler_params=_cparams("parallel", "parallel", "arbitrary"),
        name="gdn_scan",
    )(q, k, v, gb, gbt, s0)


def _gdn_inputs(x, scale, shift, rows_per_group, batch, length, p, j):
    m, d = x.shape
    wd = GD_HEADS * GD_DK
    in_w = p["gd_in_w"][j]
    u = _mm(x, in_w[:, :4 * wd].astype(BF16), mod=(scale, shift), rows_per_group=rows_per_group)
    w_ab = jnp.pad(in_w[:, 4 * wd:], ((0, 0), (0, 128 - 4 * GD_HEADS))).astype(BF16)
    ab = _mm(x, w_ab, mod=(scale, shift), rows_per_group=rows_per_group)
    qscale = GD_DK ** -0.5
    post = lambda qc, kc, vc: (_head_l2(_silu(qc), qscale), _head_l2(_silu(kc), 1.0), _silu(vc))
    q, k, v = _conv3(u, p["gd_conv"][j], [0, wd, 2 * wd], 512, length, post, 3, name="gd_conv")
    nh = GD_HEADS
    a_par = jnp.zeros((1, 1, 128), F32).at[0, 0, :2 * nh].set(-jnp.exp(p["gd_a_log"][j]).reshape(-1))
    dt_par = jnp.zeros((1, 1, 128), F32).at[0, 0, :2 * nh].set(p["gd_dt_bias"][j].reshape(-1))

    def gate_fn(abt, an, dtb):
        pre = abt + dtb
        sp = jnp.maximum(pre, 0.0) + jnp.log(1.0 + jnp.exp(-jnp.abs(pre)))
        lane = lax.broadcasted_iota(jnp.int32, abt.shape, 1)
        return (jnp.where(lane < 2 * nh, an * sp, jax.nn.sigmoid(abt)),)

    gall = _rowwise(gate_fn, [ab], [a_par, dt_par], [128], name="gd_gate")[0]
    pad = jnp.zeros((m, 128 - 2 * nh), F32)
    gb = jnp.stack([jnp.concatenate([gall[:, dr * nh:(dr + 1) * nh],
                                     gall[:, (2 + dr) * nh:(3 + dr) * nh], pad], -1) for dr in range(2)])
    gb = gb.reshape(2, batch, length, 128)
    gbt = jnp.swapaxes(gb[..., :2 * nh].reshape(2, batch, length // GD_CHUNK, GD_CHUNK, 2 * nh), -1, -2)
    rs = lambda t: t.reshape(batch, length, wd)
    return rs(q), rs(k), rs(v), gb, gbt, u


def _gdn_out(o2, u, norm_g, out_w):
    m, wd = o2.shape[1], o2.shape[2]
    ng = jnp.tile(norm_g, wd // norm_g.shape[0]).reshape(1, 1, wd)

    def body(of_ref, ob_ref, z_ref, ng_ref, o_ref):
        o = of_ref[...] + ob_ref[...]
        outs = []
        for h in range(wd // GD_DK):
            oh = o[:, h * GD_DK:(h + 1) * GD_DK]
            outs.append(oh * lax.rsqrt(jnp.mean(oh * oh, -1, keepdims=True) + 1e-6))
        o_ref[...] = jnp.concatenate(outs, -1) * ng_ref[...] * _silu(z_ref[...])

    tm = math.gcd(512, m)
    g = pl.pallas_call(
        body,
        grid=(m // tm,),
        in_specs=[pl.BlockSpec((None, tm, wd), lambda i: (0, i, 0)),
                  pl.BlockSpec((None, tm, wd), lambda i: (1, i, 0)),
                  pl.BlockSpec((tm, wd), lambda i: (i, 3)),
                  pl.BlockSpec((None, 1, wd), lambda i: (0, 0, 0))],
        out_specs=pl.BlockSpec((tm, wd), lambda i: (i, 0)),
        out_shape=jax.ShapeDtypeStruct((m, wd), F32),
        compiler_params=_cparams("parallel"),
        name="gd_norm",
    )(o2, o2, u, ng)
    return _mm(g, out_w.astype(BF16))


def _topk_rows(s, n_take, val_ref, idx_ref, base, rid=None):
    if rid is None:
        rid = lax.broadcasted_iota(jnp.int32, s.shape, 0)
    for t in range(n_take):
        mx = jnp.max(s, 0, keepdims=True)
        am = jnp.min(jnp.where(s == mx, rid, jnp.iinfo(jnp.int32).max), 0, keepdims=True)
        val_ref[base + t:base + t + 1, :] = mx
        idx_ref[base + t:base + t + 1, :] = am
        s = jnp.where(rid == am, -jnp.inf, s)


def _pk_topk_body(q_ref, keys_ref, i1_ref, i2_ref, gate_ref, sv_scr, si_scr, cv_scr, ci_scr,
                  i1p_scr, i2p_scr, gp_scr):
    kk = PK_TOPK
    tt = q_ref.shape[0]
    for h in range(PK_HEADS):
        for p in range(2):
            qh = q_ref[:, (2 * h + p) * PK_DH:(2 * h + p + 1) * PK_DH].astype(BF16)
            st = lax.dot_general(keys_ref[h, p], qh, (((1,), (1,)), ((), ())),
                                 preferred_element_type=F32)
            _topk_rows(st, kk, sv_scr, si_scr, p * kk)
        sv1, sv2 = sv_scr[0:kk, :], sv_scr[kk:2 * kk, :]
        si1, si2 = si_scr[0:kk, :], si_scr[kk:2 * kk, :]
        r8 = lax.broadcasted_iota(jnp.int32, (8, tt), 0)
        cand = jnp.concatenate([sv1[0:8, :] + sv2[r2:r2 + 1, :] for r2 in range(8)]
                               + [sv1[8:kk, :] + sv2[0:1, :], sv1[0:1, :] + sv2[8:kk, :]], 0)
        cid = jnp.concatenate([r8 * kk + r2 for r2 in range(8)] + [(r8 + 8) * kk, r8 + 8], 0)
        _topk_rows(cand, kk, cv_scr, ci_scr, 0, cid)
        cv, ci = cv_scr[...], ci_scr[...]
        a1, a2 = lax.shift_right_logical(ci, int(math.log2(kk))), ci & (kk - 1)
        i1 = jnp.zeros((kk, tt), jnp.int32)
        i2 = jnp.zeros((kk, tt), jnp.int32)
        for r in range(kk):
            i1 = i1 + jnp.where(a1 == r, si1[r:r + 1, :], 0)
            i2 = i2 + jnp.where(a2 == r, si2[r:r + 1, :], 0)
        i1p_scr[h * kk:(h + 1) * kk, :] = i1
        i2p_scr[h * kk:(h + 1) * kk, :] = i2
        e = jnp.exp(cv - jnp.max(cv, 0, keepdims=True))
        gp_scr[h * kk:(h + 1) * kk, :] = e / jnp.sum(e, 0, keepdims=True)
    i1_ref[...] = i1p_scr[...].T
    i2_ref[...] = i2p_scr[...].T
    gate_ref[...] = gp_scr[...].T


def _pk_topk(q, keys_bf16):
    m = q.shape[0]
    tt = PK_TOK
    hk = PK_HEADS * PK_TOPK
    assert hk == tt
    return pl.pallas_call(
        _pk_topk_body,
        grid=(m // tt,),
        in_specs=[pl.BlockSpec((tt, q.shape[1]), lambda i: (i, 0)),
                  pl.BlockSpec(keys_bf16.shape, lambda i: (0, 0, 0, 0))],
        out_specs=[pl.BlockSpec((tt, hk), lambda i: (i, 0))] * 3,
        out_shape=[jax.ShapeDtypeStruct((m, hk), jnp.int32), jax.ShapeDtypeStruct((m, hk), jnp.int32),
                   jax.ShapeDtypeStruct((m, hk), F32)],
        scratch_shapes=[pltpu.VMEM((2 * PK_TOPK, tt), F32), pltpu.VMEM((2 * PK_TOPK, tt), jnp.int32),
                        pltpu.VMEM((PK_TOPK, tt), F32), pltpu.VMEM((PK_TOPK, tt), jnp.int32),
                        pltpu.VMEM((hk, tt), jnp.int32), pltpu.VMEM((hk, tt), jnp.int32),
                        pltpu.VMEM((hk, tt), F32)],
        compiler_params=_cparams("parallel"),
        name="pk_topk",
    )(q, keys_bf16)


def _pk_dense_body(x_ref, sc_ref, sh_ref, gt_ref, lg_ref, lb_ref, i1_ref, i2_ref, gate_ref, ut_ref, v_ref,
                   o_ref, hb_scr, act_scr, w_scr, grid_scr, acc_scr, *, alpha, n_chunks):
    e = pl.program_id(1)
    tm = x_ref.shape[0]
    nk = PK_NKEYS
    per = ut_ref.shape[1] // nk

    @pl.when(e == 0)
    def _():
        hb_scr[...] = (x_ref[...] * (1.0 + sc_ref[...]) + sh_ref[...]).astype(BF16)
        act_scr[...] = jnp.zeros_like(act_scr)

    @pl.when(e < n_chunks)
    def _():
        hb = hb_scr[...]
        i1 = i1_ref[...]
        i2 = i2_ref[...]
        act = act_scr[...]
        for kp in range(per // 2):
            s = _dot(hb, ut_ref[:, 2 * kp * nk:2 * (kp + 1) * nk])
            for k in (2 * kp, 2 * kp + 1):
                got = jnp.take_along_axis(s[:, (k % 2) * nk:(k % 2 + 1) * nk], i2, axis=1)
                act = jnp.where(i1 == e * per + k, got, act)
        act_scr[...] = act

    @pl.when(e == n_chunks)
    def _():
        w_scr[...] = jax.nn.gelu(act_scr[...]) * gate_ref[...]
        acc_scr[...] = jnp.zeros_like(acc_scr)
        sub = lax.broadcasted_iota(jnp.int32, (nk, nk), 0)

        def tok(tg, carry):
            for u in range(PK_UNROLL):
                t = tg * PK_UNROLL + u
                wrow = w_scr[pl.ds(t, 1), :]
                at = jnp.where(sub == i1_ref[pl.ds(t, 1), :], wrow, 0.0).astype(BF16)
                bt = (sub == i2_ref[pl.ds(t, 1), :]).astype(BF16)
                grid_scr[t] = lax.dot_general(at, bt, (((1,), (1,)), ((), ())), preferred_element_type=F32)
            return carry

        lax.fori_loop(0, tm // PK_UNROLL, tok, 0)

    @pl.when(e >= n_chunks)
    def _():
        c = e - n_chunks
        lhs = jnp.concatenate([grid_scr[:, c * per + k, :] for k in range(per)], -1).astype(BF16)
        acc_scr[...] += _dot(lhs, v_ref[...])

    @pl.when(e == 2 * n_chunks - 1)
    def _():
        o_ref[...] = _ln(alpha * x_ref[...] + gt_ref[...] * acc_scr[...], lg_ref[...], lb_ref[...])


def _pk_dense(x, scale, shift, gate_vec, ln_g, ln_b, i1, i2, gate, ut_bf16, v_bf16, rows_per_group, alpha):
    m, d = x.shape
    n_exp = v_bf16.shape[0]
    tm = math.gcd(PK_TM, m, rows_per_group)
    n_chunks = n_exp // PK_EC
    g = rows_per_group // tm
    assert n_exp % PK_EC == 0 and PK_EC % PK_NKEYS == 0
    row = lambda i, e: (i, 0)
    vec = lambda i, e: (i // g, 0, 0)
    one = lambda i, e: (0, 0, 0)
    npk = i1.shape[1]
    return pl.pallas_call(
        functools.partial(_pk_dense_body, alpha=alpha, n_chunks=n_chunks),
        grid=(m // tm, 2 * n_chunks),
        in_specs=[pl.BlockSpec((tm, d), row),
                  pl.BlockSpec((None, 1, d), vec), pl.BlockSpec((None, 1, d), vec), pl.BlockSpec((None, 1, d), vec),
                  pl.BlockSpec((None, 1, d), one), pl.BlockSpec((None, 1, d), one),
                  pl.BlockSpec((tm, npk), row), pl.BlockSpec((tm, npk), row), pl.BlockSpec((tm, npk), row),
                  pl.BlockSpec((d, PK_EC), lambda i, e: (0, jnp.minimum(e, n_chunks - 1))),
                  pl.BlockSpec((PK_EC, d), lambda i, e: (jnp.maximum(e - n_chunks, 0), 0))],
        out_specs=pl.BlockSpec((tm, d), row),
        out_shape=jax.ShapeDtypeStruct((m, d), F32),
        scratch_shapes=[pltpu.VMEM((tm, d), BF16), pltpu.VMEM((tm, npk), F32), pltpu.VMEM((tm, npk), F32),
                        pltpu.VMEM((tm, PK_NKEYS, PK_NKEYS), F32), pltpu.VMEM((tm, d), F32)],
        compiler_params=_cparams("parallel", "arbitrary"),
        name="pk_dense",
    )(x, scale, shift, gate_vec, ln_g, ln_b, i1, i2, gate, ut_bf16, v_bf16)


def _peer_ln(x, scale, shift, gate_vec, ln_g, ln_b, rows_per_group, alpha, wq_bf16, keys_bf16, ut_bf16, v_bf16):
    q = _mm(x, wq_bf16, mod=(scale, shift), rows_per_group=rows_per_group)
    i1, i2, gate = _pk_topk(q, keys_bf16)
    return _pk_dense(x, scale, shift, gate_vec, ln_g, ln_b, i1, i2, gate, ut_bf16, v_bf16, rows_per_group, alpha)


def kernel(x, c, ctx, c_ctx, ada_w, ada_b, ln_g, ln_b, pk_wq, pk_keys, pk_u, pk_v, hy_in_w, hy_in_b, hy_conv, hy_f_w1, hy_f_b1, hy_f_w2, hy_f_b2, hy_f_w3, hy_skip, hy_out_w, hy_out_b, gd_in_w, gd_conv, gd_a_log, gd_dt_bias, gd_norm_g, gd_out_w, fn_out_w, fn_out_b):
    p = dict(hy_in_w=hy_in_w, hy_in_b=hy_in_b, hy_conv=hy_conv, hy_f_w1=hy_f_w1, hy_f_b1=hy_f_b1,
             hy_f_w2=hy_f_w2, hy_f_b2=hy_f_b2, hy_f_w3=hy_f_w3, hy_skip=hy_skip, hy_out_w=hy_out_w,
             hy_out_b=hy_out_b, gd_in_w=gd_in_w, gd_conv=gd_conv, gd_a_log=gd_a_log, gd_dt_bias=gd_dt_bias,
             fn_out_w=fn_out_w, fn_out_b=fn_out_b)
    b, length, d = x.shape
    lc = ctx.shape[1]
    depth = ada_w.shape[0]
    alpha = (2 * depth) ** 0.25
    xl = _pos_add(x)
    xc = ctx.reshape(b * lc, d)
    gdn_layers = [i for i in range(depth) if i % N_MIXERS == 1]
    ctx_until = gdn_layers[-1] if gdn_layers else -1
    cond = jnp.concatenate([c, c_ctx[None], jnp.zeros((8 - b - 1, d), F32)], 0)
    for i in range(depth):
        kind, j = i % N_MIXERS, i // N_MIXERS
        ctx_in, ctx_out = i <= ctx_until, i < ctx_until
        mod = _mm(cond, ada_w[i].astype(BF16), bias=ada_b[i], silu_in=True, tn=2048)
        mod = mod.reshape(8, N_MOD, 1, d)
        ml = [mod[:b, t] for t in range(N_MOD)]
        mc = [mod[b:b + 1, t] for t in range(N_MOD)]
        lg = [ln_g[i, t].reshape(1, 1, d) for t in range(2)]
        lb = [ln_b[i, t].reshape(1, 1, d) for t in range(2)]
        yc = None
        if kind == 0:
            yl = _hyena(xl, ml[1], ml[0], length, length, p, j)
            if ctx_out:
                yc = _hyena(xc, mc[1], mc[0], b * lc, lc, p, j)
        elif kind == 1:
            qc, kc, vc, gbc, gbtc, uc = _gdn_inputs(xc, mc[1], mc[0], b * lc, b, lc, p, j)
            ql, kl, vl, gbl, gbtl, ul = _gdn_inputs(xl, ml[1], ml[0], length, b, length, p, j)
            s0 = jnp.zeros((2, b, GD_HEADS, GD_DK, GD_DK), F32)
            oc, s_ctx = _gdn_scan(qc, kc, vc, gbc, gbtc, s0)
            ol, _ = _gdn_scan(ql, kl, vl, gbl, gbtl, s_ctx)
            yl = _gdn_out(ol.reshape(2, b * length, -1), ul, gd_norm_g[j], gd_out_w[j])
            if ctx_out:
                yc = _gdn_out(oc.reshape(2, b * lc, -1), uc, gd_norm_g[j], gd_out_w[j])
        else:
            yl = _fnet(xl, ml[1], ml[0], length, b, length, p, j)
            if ctx_out:
                yc = _fnet(xc, mc[1], mc[0], b * lc, b, lc, p, j)
        wq = pk_wq[i].astype(BF16)
        keys = pk_keys[i].astype(BF16)
        ut, vt = pk_u[i].astype(BF16).T, pk_v[i].astype(BF16)
        xl = _res_ln(xl, yl, ml[2], lg[0], lb[0], alpha, length)
        xl = _peer_ln(xl, ml[4], ml[3], ml[5], lg[1], lb[1], length, alpha, wq, keys, ut, vt)
        if ctx_out:
            xc = _res_ln(xc, yc, mc[2], lg[0], lb[0], alpha, b * lc)
            xc = _peer_ln(xc, mc[4], mc[3], mc[5], lg[1], lb[1], b * lc, alpha, wq, keys, ut, vt)
    return xl.reshape(b, length, d)
```

```python
import functools
import math

import numpy as np
import jax
import jax.numpy as jnp
from jax import lax
from jax.experimental import pallas as pl
from jax.experimental.pallas import tpu as pltpu

F32 = jnp.float32
BF16 = jnp.bfloat16

GRID_W = 64
N_MIXERS = 3
N_MOD = 6
LN_EPS = 1e-5
HY_EMB = 33
HY_BANDS = (HY_EMB - 1) // 2
HY_SHIFT = 0.05
HY_TARGET = 1e-2
HY_MIN_DECAY = math.log(HY_TARGET) / 1.5
HY_MAX_DECAY = math.log(HY_TARGET) / 0.3
GD_HEADS = 8
GD_DK = 128
GD_CHUNK = 64
FN_GROUPS = 4
PK_HEADS = 8
PK_NKEYS = 128
PK_DH = 128
PK_TOPK = 16
PK_TOK = 128
PK_TM = 512
PK_EC = 2048
PK_UNROLL = 16

VMEM_LIMIT_BYTES = 56 * 1024 * 1024


def _cparams(*sem):
    return pltpu.CompilerParams(dimension_semantics=sem, vmem_limit_bytes=VMEM_LIMIT_BYTES)


def _dot(a, b):
    return jnp.dot(a, b, preferred_element_type=F32)


def _dot_hi(a, b):
    return jnp.dot(a, b, preferred_element_type=F32, precision=lax.Precision.HIGHEST)


def _split(a):
    hi = a.astype(BF16)
    lo = (a - hi.astype(F32)).astype(BF16)
    return hi, lo


def _dot3(a, b):
    ah, al = _split(a)
    bh, bl = _split(b)
    return _dot(ah, bh) + (_dot(ah, bl) + _dot(al, bh))


def _mm_body(*refs, has_mod, has_bias, silu_in):
    a_ref, w_ref = refs[0], refs[1]
    k = 2
    a = a_ref[...]
    if has_mod:
        a = a * (1.0 + refs[k][...]) + refs[k + 1][...]
        k += 2
    if silu_in:
        a = a * jax.nn.sigmoid(a)
    o = _dot(a.astype(BF16), w_ref[...])
    if has_bias:
        o = o + refs[k][...]
        k += 1
    refs[k][...] = o


def _mm(a, w_bf16, bias=None, mod=None, rows_per_group=None, silu_in=False, tm=512, tn=None):
    m, k = a.shape
    n = w_bf16.shape[1]
    tm = math.gcd(tm, m, rows_per_group or m)
    if tn is None:
        tn = max(t for t in range(128, min(n, 2048) + 1, 128) if n % t == 0)
    assert m % tm == 0 and n % tn == 0
    ins = [a, w_bf16]
    specs = [pl.BlockSpec((tm, k), lambda j, i: (i, 0)), pl.BlockSpec((k, tn), lambda j, i: (0, j))]
    if mod is not None:
        assert rows_per_group % tm == 0
        g = rows_per_group // tm
        for v in mod:
            ins.append(v)
            specs.append(pl.BlockSpec((None, 1, k), lambda j, i: (i // g, 0, 0)))
    if bias is not None:
        ins.append(bias.reshape(1, n))
        specs.append(pl.BlockSpec((1, tn), lambda j, i: (0, j)))
    return pl.pallas_call(
        functools.partial(_mm_body, has_mod=mod is not None, has_bias=bias is not None, silu_in=silu_in),
        grid=(n // tn, m // tm),
        in_specs=specs,
        out_specs=pl.BlockSpec((tm, tn), lambda j, i: (i, j)),
        out_shape=jax.ShapeDtypeStruct((m, n), F32),
        compiler_params=_cparams("parallel", "parallel"),
        name="mm",
    )(*ins)


def _rowwise(fn, rows, vecs, out_cols, rows_per_group=None, tm=512, name="rowwise"):
    m = rows[0].shape[0]
    tm = math.gcd(tm, m, rows_per_group or m)
    assert m % tm == 0
    n_r, n_v, n_o = len(rows), len(vecs), len(out_cols)

    def body(*refs):
        outs = fn(*[r[...] for r in refs[:n_r + n_v]])
        for o_ref, o in zip(refs[n_r + n_v:], outs):
            o_ref[...] = o

    specs = [pl.BlockSpec((tm, r.shape[1]), lambda i: (i, 0)) for r in rows]
    for v in vecs:
        if v.shape[0] == 1:
            specs.append(pl.BlockSpec((None, 1, v.shape[2]), lambda i: (0, 0, 0)))
        else:
            assert rows_per_group % tm == 0
            g = rows_per_group // tm
            specs.append(pl.BlockSpec((None, 1, v.shape[2]), lambda i, g=g: (i // g, 0, 0)))
    return pl.pallas_call(
        body,
        grid=(m // tm,),
        in_specs=specs,
        out_specs=[pl.BlockSpec((tm, c), lambda i: (i, 0)) for c in out_cols],
        out_shape=[jax.ShapeDtypeStruct((m, c), F32) for c in out_cols],
        compiler_params=_cparams("parallel"),
        name=name,
    )(*rows, *vecs)


def _ln(v, g, b):
    mu = jnp.mean(v, -1, keepdims=True)
    d = v - mu
    var = jnp.mean(d * d, -1, keepdims=True)
    return d * lax.rsqrt(var + LN_EPS) * g + b


def _res_ln(x, y, gate, ln_g, ln_b, alpha, rows_per_group):
    fn = lambda xt, yt, gt, lg, lb: (_ln(alpha * xt + gt * yt, lg, lb),)
    return _rowwise(fn, [x, y], [gate, ln_g, ln_b], [x.shape[1]], rows_per_group, name="res_ln")[0]


def _pos_add_body(x_ref, er_ref, ec_ref, o_ref):
    half = er_ref.shape[-1]
    x = x_ref[...]
    er = jnp.broadcast_to(er_ref[...], x.shape[:2] + (half,))
    ec = jnp.broadcast_to(ec_ref[...][None], x.shape[:2] + (half,))
    o_ref[...] = x + jnp.concatenate([er, ec], -1)


def _pos_add(x):
    b, length, d = x.shape
    rows = length // GRID_W
    quarter = d // 4
    omega = 1.0 / (10000.0 ** (jnp.arange(quarter, dtype=F32) / quarter))
    er = jnp.arange(rows, dtype=F32)[:, None] * omega
    ec = jnp.arange(GRID_W, dtype=F32)[:, None] * omega
    emb_r = jnp.concatenate([jnp.sin(er), jnp.cos(er)], -1).reshape(rows, 1, d // 2)
    emb_c = jnp.concatenate([jnp.sin(ec), jnp.cos(ec)], -1)
    rt = 8
    x4 = x.reshape(b, rows, GRID_W, d)
    out = pl.pallas_call(
        _pos_add_body,
        grid=(b, rows // rt),
        in_specs=[pl.BlockSpec((None, rt, GRID_W, d), lambda i, j: (i, j, 0, 0)),
                  pl.BlockSpec((rt, 1, d // 2), lambda i, j: (j, 0, 0)),
                  pl.BlockSpec((GRID_W, d // 2), lambda i, j: (0, 0))],
        out_specs=pl.BlockSpec((None, rt, GRID_W, d), lambda i, j: (i, j, 0, 0)),
        out_shape=jax.ShapeDtypeStruct(x4.shape, F32),
        compiler_params=_cparams("parallel", "parallel"),
        name="pos_add",
    )(x4, emb_r, emb_c)
    return out.reshape(b * length, d)


def _conv3_body(*refs, n_parts, tm, seq_len, post, n_vec):
    i = pl.program_id(0)
    first = (i * tm) % seq_len == 0
    last = ((i + 1) * tm) % seq_len == 0
    row = lax.broadcasted_iota(jnp.int32, (tm, 1), 0)
    parts = []
    for p in range(n_parts):
        main_ref, prev_ref, next_ref, w_ref = refs[4 * p:4 * p + 4]
        u = main_ref[...]
        w = w_ref[...]
        prev = jnp.where(first, 0.0, prev_ref[7:8, :])
        nxt = jnp.where(last, 0.0, next_ref[0:1, :])
        up = jnp.where(row == 0, prev, pltpu.roll(u, 1, 0))
        dn = jnp.where(row == tm - 1, nxt, pltpu.roll(u, tm - 1, 0))
        parts.append(up * w[0:1, :] + u * w[1:2, :] + dn * w[2:3, :])
    k = 4 * n_parts
    vecs = [refs[k + j][...] for j in range(n_vec)]
    outs = post(*parts, *vecs)
    for o_ref, o in zip(refs[k + n_vec:], outs):
        o_ref[...] = o


def _conv3(u, w, col_parts, tc, seq_len, post, n_out, vecs=(), tm=256, name="conv3"):
    m = u.shape[0]
    tm = min(tm, seq_len)
    assert seq_len % tm == 0 and m % tm == 0 and tm % 8 == 0
    width = col_parts[1] - col_parts[0] if len(col_parts) > 1 else tc
    ncol = width // tc
    t8 = tm // 8
    nb8 = m // 8
    ins, specs = [], []
    for c0 in col_parts:
        cb = c0 // tc
        ins += [u, u, u, w]
        specs += [
            pl.BlockSpec((tm, tc), lambda i, j, cb=cb: (i, cb + j)),
            pl.BlockSpec((8, tc), lambda i, j, cb=cb: (jnp.maximum(i * t8 - 1, 0), cb + j)),
            pl.BlockSpec((8, tc), lambda i, j, cb=cb: (jnp.minimum((i + 1) * t8, nb8 - 1), cb + j)),
            pl.BlockSpec((3, tc), lambda i, j, cb=cb: (0, cb + j)),
        ]
    for v in vecs:
        ins.append(v)
        specs.append(pl.BlockSpec((1, tc), lambda i, j: (0, j)))
    return pl.pallas_call(
        functools.partial(_conv3_body, n_parts=len(col_parts), tm=tm, seq_len=seq_len, post=post,
                          n_vec=len(vecs)),
        grid=(m // tm, ncol),
        in_specs=specs,
        out_specs=[pl.BlockSpec((tm, tc), lambda i, j: (i, j)) for _ in range(n_out)],
        out_shape=[jax.ShapeDtypeStruct((m, width), F32) for _ in range(n_out)],
        compiler_params=_cparams("parallel", "parallel"),
        name=name,
    )(*ins)


def _lmm_body(*refs, n_x):
    w_ref, o_ref = refs[0], refs[1 + n_x]
    tc = o_ref.shape[-1]
    xs = [r[...].reshape(-1, tc) for r in refs[1:1 + n_x]]
    x = xs[0] if n_x == 1 else jnp.concatenate(xs, 0)
    w = w_ref[...]
    w = w.reshape(w.shape[-2], w.shape[-1])
    o_ref[...] = _dot(w, x.astype(BF16)).reshape(o_ref.shape).astype(o_ref.dtype)


def _lmm(w, w_spec, xs, x_specs, out_shape, out_spec, grid, name, out_dtype=F32):
    return pl.pallas_call(
        functools.partial(_lmm_body, n_x=len(xs)),
        grid=grid,
        in_specs=[w_spec] + list(x_specs),
        out_specs=out_spec,
        out_shape=jax.ShapeDtypeStruct(out_shape, out_dtype),
        compiler_params=_cparams(*(["parallel"] * len(grid))),
        name=name,
    )(w, *xs)


def _spec_mul_body(w1_ref, w2_ref, x_ref, h_ref, o_ref):
    tc = o_ref.shape[-1]
    x = x_ref[...].reshape(-1, tc)
    z = _dot(w1_ref[...], x.astype(BF16))
    half = z.shape[0] // 2
    zr, zi = z[:half], z[half:]
    hr, hi = h_ref[0], h_ref[1]
    y = jnp.concatenate([zr * hr - zi * hi, zr * hi + zi * hr], 0)
    o_ref[...] = _dot(w2_ref[...], y.astype(BF16)).reshape(o_ref.shape).astype(o_ref.dtype)


def _cplx_mat(ang):
    c, s = jnp.cos(ang), jnp.sin(ang)
    return jnp.concatenate([jnp.concatenate([c, -s], -1), jnp.concatenate([s, c], -1)], -2)


def _phase(k, n):
    return (k % n).astype(F32) * (2.0 * math.pi / n)


def _outer_mats(n1, n2, n_in, sign):
    n = n1 * n2
    s1 = jnp.arange(n1, dtype=jnp.int32)[:, None, None]
    f2 = jnp.arange(n2, dtype=jnp.int32)[None, :, None]
    s2 = jnp.arange(n_in, dtype=jnp.int32)[None, None, :]
    return sign * _phase(f2 * (s1 + n1 * s2), n)


def _inner_phase(n1, sign):
    a = jnp.arange(n1, dtype=jnp.int32)
    return sign * _phase(a[:, None] * a[None, :], n1)


def _split_len(n):
    n1 = 1 << (int(math.log2(n)) // 2)
    return n1, n // n1


def _fft_conv_pair(z2, hspec, n1, n2, tcol=512):
    _, length, c = z2.shape
    n = 2 * length
    assert n1 * n2 == n
    h2 = n2 // 2
    m_in = _cplx_mat(_outer_mats(n1, n2, h2, -1.0)).astype(BF16)
    a = _lmm(m_in, pl.BlockSpec((1, 2 * n2, n2), lambda s: (s, 0, 0)),
             [z2.reshape(2, h2, n1 * c)], [pl.BlockSpec((2, h2, c), lambda s: (0, 0, s))],
             (2, n1, n2, c), pl.BlockSpec((2, 1, n2, c), lambda s: (0, s, 0, 0)), (n1,), "fft_in", BF16)
    w1 = _cplx_mat(_inner_phase(n1, -1.0)).astype(BF16)
    w2 = (_cplx_mat(_inner_phase(n1, 1.0)) * (1.0 / n)).astype(BF16)
    cols = n2 * c
    tcol = min(tcol, cols)
    b = pl.pallas_call(
        _spec_mul_body,
        grid=(cols // tcol,),
        in_specs=[pl.BlockSpec((2 * n1, 2 * n1), lambda j: (0, 0)),
                  pl.BlockSpec((2 * n1, 2 * n1), lambda j: (0, 0)),
                  pl.BlockSpec((2, n1, tcol), lambda j: (0, 0, j)),
                  pl.BlockSpec((2, n1, tcol), lambda j: (0, 0, j))],
        out_specs=pl.BlockSpec((2, n1, tcol), lambda j: (0, 0, j)),
        out_shape=jax.ShapeDtypeStruct((2, n1, cols), BF16),
        compiler_params=_cparams("parallel"),
        name="fft_mid",
    )(w1, w2, a.reshape(2, n1, cols), hspec)
    m_out = _cplx_mat(jnp.swapaxes(_outer_mats(n1, n2, h2, 1.0), 1, 2)).astype(BF16)
    y = _lmm(m_out, pl.BlockSpec((1, n2, 2 * n2), lambda s: (s, 0, 0)),
             [b.reshape(2, n1, n2, c)], [pl.BlockSpec((2, 1, n2, c), lambda s: (0, s, 0, 0))],
             (2, h2, n1 * c), pl.BlockSpec((2, h2, c), lambda s: (0, 0, s)), (n1,), "fft_out")
    return y.reshape(2, length, c)


def _fft_real_spectrum(f, n1, n2, tcol=512):
    n, c = f.shape
    if n2 == 1:
        ph = _inner_phase(n1, -1.0)
        w = jnp.concatenate([jnp.cos(ph), jnp.sin(ph)], 0).astype(BF16)
        tcol = min(tcol, c)
        return _lmm(w, pl.BlockSpec((2 * n1, n1), lambda j: (0, 0)),
                    [f], [pl.BlockSpec((n1, tcol), lambda j: (0, j))],
                    (2, n1, c), pl.BlockSpec((2, n1, tcol), lambda j: (0, 0, j)), (c // tcol,), "fft_spec1")
    ph = _outer_mats(n1, n2, n2, -1.0)
    m_in = jnp.concatenate([jnp.cos(ph), jnp.sin(ph)], 1).astype(BF16)
    a = _lmm(m_in, pl.BlockSpec((1, 2 * n2, n2), lambda s: (s, 0, 0)),
             [f.reshape(n2, n1 * c)], [pl.BlockSpec((n2, c), lambda s: (0, s))],
             (2, n1, n2, c), pl.BlockSpec((2, 1, n2, c), lambda s: (0, s, 0, 0)), (n1,), "fft_spec_in", BF16)
    w1 = _cplx_mat(_inner_phase(n1, -1.0)).astype(BF16)
    cols = n2 * c
    tcol = min(tcol, cols)
    h = _lmm(w1, pl.BlockSpec((2 * n1, 2 * n1), lambda j: (0, 0)),
             [a.reshape(2, n1, cols)], [pl.BlockSpec((2, n1, tcol), lambda j: (0, 0, j))],
             (2, n1, cols), pl.BlockSpec((2, n1, tcol), lambda j: (0, 0, j)), (cols // tcol,), "fft_spec_mid")
    return h


def _hy_filter_body(wt_ref, wc_ref, ws_ref, b1_ref, w2_ref, b2_ref, w3_ref, bands_ref, dl_ref, o_ref,
                    *, length, tr):
    d = o_ref.shape[-1]
    j = pl.program_id(0) * tr + lax.broadcasted_iota(jnp.int32, (tr, 1), 0)
    k = jnp.where(j < length, j, 2 * length - j)
    t = k.astype(F32) / length
    ang = 2.0 * jnp.pi * t * bands_ref[...]
    pre = t * wt_ref[...] + _dot_hi(jnp.cos(ang), wc_ref[...]) + _dot_hi(-jnp.sin(ang), ws_ref[...])
    hdn = jnp.sin(pre + b1_ref[...])
    hdn = jnp.sin(_dot_hi(hdn, w2_ref[...]) + b2_ref[...])
    hf = _dot_hi(hdn, w3_ref[...])
    win = jnp.exp(-t * dl_ref[...]) + HY_SHIFT
    h = jnp.where(j < length, hf[:, :d], hf[:, d:]) * win
    o_ref[...] = jnp.where(j == length, 0.0, h)


def _hy_filter(length, f_w1, f_b1, f_w2, f_b2, f_w3):
    d = f_w3.shape[1] // 2
    ffn = f_w2.shape[0]
    tr = min(512, length)
    bands = jnp.linspace(1e-4, HY_BANDS - 1, HY_BANDS, dtype=F32).reshape(1, HY_BANDS)
    deltas = jnp.abs(jnp.linspace(HY_MIN_DECAY, HY_MAX_DECAY, d, dtype=F32)).reshape(1, d)
    ins = [f_w1[0:1], f_w1[1:1 + HY_BANDS], f_w1[1 + HY_BANDS:], f_b1.reshape(1, ffn), f_w2,
           f_b2.reshape(1, ffn), f_w3, bands, deltas]
    return pl.pallas_call(
        functools.partial(_hy_filter_body, length=length, tr=tr),
        grid=(2 * length // tr,),
        in_specs=[pl.BlockSpec(a.shape, lambda i: (0, 0)) for a in ins],
        out_specs=pl.BlockSpec((tr, d), lambda i: (i, 0)),
        out_shape=jax.ShapeDtypeStruct((2 * length, d), F32),
        compiler_params=_cparams("parallel"),
        name="hy_filter",
    )(*ins)


def _hyena(x, scale, shift, rows_per_group, length, p, j):
    m, d = x.shape
    assert m == 2 * length
    u = _mm(x, p["hy_in_w"][j].astype(BF16), bias=p["hy_in_b"][j], mod=(scale, shift),
            rows_per_group=rows_per_group)
    post = lambda x0, x1, v: (x0, v * x1)
    x0c, z = _conv3(u, p["hy_conv"][j], [0, d, 2 * d], 512, length, post, 2, name="hy_conv")
    filt = _hy_filter(length, p["hy_f_w1"][j], p["hy_f_b1"][j], p["hy_f_w2"][j], p["hy_f_b2"][j],
                      p["hy_f_w3"][j])
    n = 2 * length
    n1, n2 = (n, 1) if n <= 1024 else _split_len(n)
    hspec = _fft_real_spectrum(filt, n1, n2)
    if n2 == 1:
        z2 = jnp.pad(z.reshape(2, length, d), ((0, 0), (0, length), (0, 0)))
        y = _fft_conv_pair_single(z2, hspec, n)[:, :length]
    else:
        y = _fft_conv_pair(z.reshape(2, length, d), hspec, n1, n2)
    fn = lambda yt, zt, x0t, sk: ((yt + sk * zt) * x0t,)
    g = _rowwise(fn, [y.reshape(m, d), z, x0c], [p["hy_skip"][j].reshape(1, 1, d)], [d], name="hy_gate")[0]
    return _mm(g, p["hy_out_w"][j].astype(BF16), bias=p["hy_out_b"][j])


def _fft_conv_pair_single(z2, hspec, n, tcol=512):
    c = z2.shape[-1]
    w1 = _cplx_mat(_inner_phase(n, -1.0)).astype(BF16)
    w2 = (_cplx_mat(_inner_phase(n, 1.0)) * (1.0 / n)).astype(BF16)
    tcol = min(tcol, c)
    return pl.pallas_call(
        _spec_mul_body,
        grid=(c // tcol,),
        in_specs=[pl.BlockSpec((2 * n, 2 * n), lambda j: (0, 0)),
                  pl.BlockSpec((2 * n, 2 * n), lambda j: (0, 0)),
                  pl.BlockSpec((2, n, tcol), lambda j: (0, 0, j)),
                  pl.BlockSpec((2, n, tcol), lambda j: (0, 0, j))],
        out_specs=pl.BlockSpec((2, n, tcol), lambda j: (0, 0, j)),
        out_shape=jax.ShapeDtypeStruct((2, n, c), F32),
        compiler_params=_cparams("parallel"),
        name="fft_mid1",
    )(w1, w2, z2, hspec)


def _fnet(x, scale, shift, rows_per_group, batch, length, p, j):
    m, d = x.shape
    gc = d // FN_GROUPS
    ph = _inner_phase(gc, -1.0)
    eye = jnp.eye(FN_GROUPS, dtype=F32)
    w_c = jnp.concatenate([jnp.kron(eye, jnp.cos(ph)), jnp.kron(eye, jnp.sin(ph))], 1).astype(BF16)
    w = _mm(x, w_c, mod=(scale, shift), rows_per_group=rows_per_group)
    if length <= 1024:
        n1, n2 = length, 1
    else:
        n1, n2 = _split_len(length)
    norm = 1.0 / math.sqrt(length * gc)
    if n2 == 1:
        ph1 = _inner_phase(n1, -1.0)
        wr = (jnp.concatenate([jnp.cos(ph1), -jnp.sin(ph1)], 1) * norm).astype(BF16)
        y = _lmm(wr, pl.BlockSpec((n1, 2 * n1), lambda b, c: (0, 0)),
                 [w.reshape(batch, n1, 2 * d)] * 2,
                 [pl.BlockSpec((None, n1, d), lambda b, c: (b, 0, 0)),
                  pl.BlockSpec((None, n1, d), lambda b, c: (b, 0, 1))],
                 (batch, n1, d), pl.BlockSpec((None, n1, d), lambda b, c: (b, 0, 0)), (batch, 1), "fn_pos1")
        y = y.reshape(m, d)
    else:
        m_in = _cplx_mat(_outer_mats(n1, n2, n2, -1.0)).astype(BF16)
        wv = w.reshape(batch, n2, n1 * 2 * d)
        a = _lmm(m_in, pl.BlockSpec((1, 2 * n2, 2 * n2), lambda b, s: (s, 0, 0)),
                 [wv, wv],
                 [pl.BlockSpec((None, n2, d), lambda b, s: (b, 0, 2 * s)),
                  pl.BlockSpec((None, n2, d), lambda b, s: (b, 0, 2 * s + 1))],
                 (batch, 2, n1, n2, d), pl.BlockSpec((None, 2, 1, n2, d), lambda b, s: (b, 0, s, 0, 0)),
                 (batch, n1), "fn_pos_in", BF16)
        ph1 = _inner_phase(n1, -1.0)
        wr = (jnp.concatenate([jnp.cos(ph1), -jnp.sin(ph1)], 1) * norm).astype(BF16)
        cols = n2 * d
        tcol = 1024
        y = _lmm(wr, pl.BlockSpec((n1, 2 * n1), lambda b, c: (0, 0)),
                 [a.reshape(batch, 2, n1, cols)],
                 [pl.BlockSpec((None, 2, n1, tcol), lambda b, c: (b, 0, 0, c))],
                 (batch, n1, cols), pl.BlockSpec((None, n1, tcol), lambda b, c: (b, 0, c)),
                 (batch, cols // tcol), "fn_pos_mid")
        y = y.reshape(m, d)
    return _mm(y, p["fn_out_w"][j].astype(BF16), bias=p["fn_out_b"][j])


def _head_l2(t, extra):
    outs = []
    for h in range(t.shape[1] // GD_DK):
        th = t[:, h * GD_DK:(h + 1) * GD_DK]
        outs.append(th * (lax.rsqrt(jnp.sum(th * th, -1, keepdims=True) + 1e-6) * extra))
    return jnp.concatenate(outs, -1)


def _silu(v):
    return v * jax.nn.sigmoid(v)


def _gdn_scan_body(q_ref, k_ref, v_ref, gb_ref, gbt_ref, s0_ref, o_ref, sfin_ref, s_scr, *, n_chunks):
    direction = pl.program_id(0)
    c = pl.program_id(2)
    cs = GD_CHUNK

    @pl.when(c == 0)
    def _():
        s_scr[...] = s0_ref[...]

    ri = lax.broadcasted_iota(jnp.int32, (cs, cs), 0)
    ci = lax.broadcasted_iota(jnp.int32, (cs, cs), 1)
    lag = (ri - ci) * (1 - 2 * direction)
    incl = lag >= 0
    strict = lag > 0
    tri = incl.astype(F32)
    tri_t = (lag <= 0).astype(F32)
    eye = (ri == ci).astype(F32)
    pair_masks = []
    for lvl in range(int(math.log2(cs))):
        rb, cb = lax.shift_right_logical(ri, lvl), lax.shift_right_logical(ci, lvl)
        pair_masks.append((jnp.abs(rb - cb) == 1) & ((jnp.minimum(rb, cb) & 1) == 0))
    gb = gb_ref[...]
    gbt = gbt_ref[...]
    gc_cols = _dot_hi(tri, gb)
    gc_rows = _dot_hi(gbt, tri_t)
    tot = jnp.sum(gb, 0, keepdims=True)
    hs = range(GD_HEADS)
    nt = (((1,), (1,)), ((), ()))
    tn = (((0,), (0,)), ((), ()))
    sl = [slice(h * GD_DK, (h + 1) * GD_DK) for h in hs]
    k = [k_ref[:, sl[h]] for h in hs]
    gcol = [gc_cols[:, h:h + 1] for h in hs]
    beta = [gb[:, GD_HEADS + h:GD_HEADS + h + 1] for h in hs]
    gtot = [tot[:, h:h + 1] for h in hs]
    decay = [jnp.exp(jnp.where(incl, gcol[h] - gc_rows[h:h + 1, :], -jnp.inf)) for h in hs]
    eg = [jnp.exp(gcol[h]) for h in hs]
    kb = [k[h] * beta[h] for h in hs]
    kbf = [k[h].astype(BF16) for h in hs]
    kk = [lax.dot_general(kb[h].astype(BF16), kbf[h], nt, preferred_element_type=F32) for h in hs]
    a = [jnp.where(strict, kk[h] * decay[h], 0.0) for h in hs]
    inv = [eye - jnp.where(pair_masks[0], a[h], 0.0) for h in hs]
    for pm in pair_masks[1:]:
        tn_s = [_dot3(inv[h], jnp.where(pm, a[h], 0.0)) for h in hs]
        tnt = [_dot3(tn_s[h], inv[h]) for h in hs]
        inv = [inv[h] - tnt[h] for h in hs]
    rhs = [jnp.concatenate([v_ref[:, sl[h]] * beta[h], kb[h] * eg[h]], -1) for h in hs]
    sol = [_dot3(inv[h], rhs[h]) for h in hs]
    q = [q_ref[:, sl[h]] for h in hs]
    qk = [lax.dot_general(q[h].astype(BF16), kbf[h], nt, preferred_element_type=F32) for h in hs]
    sb = [s_scr[h].astype(BF16) for h in hs]
    ws = [_dot(sol[h][:, GD_DK:].astype(BF16), sb[h]) for h in hs]
    qs = [_dot((q[h] * eg[h]).astype(BF16), sb[h]) for h in hs]
    vnb = [(sol[h][:, :GD_DK] - ws[h]).astype(BF16) for h in hs]
    av = [_dot((qk[h] * decay[h]).astype(BF16), vnb[h]) for h in hs]
    kv = [lax.dot_general((k[h] * jnp.exp(gtot[h] - gcol[h])).astype(BF16), vnb[h], tn,
                          preferred_element_type=F32) for h in hs]
    for h in hs:
        s_scr[h] = s_scr[h] * jnp.exp(gtot[h]) + kv[h]
    o_ref[...] = jnp.concatenate([qs[h] + av[h] for h in hs], -1)

    @pl.when(c == n_chunks - 1)
    def _():
        sfin_ref[...] = s_scr[...]


def _gdn_scan(q, k, v, gb, gbt, s0):
    b, length, w = q.shape
    n_chunks = length // GD_CHUNK
    cidx = lambda d, c: c + d * (n_chunks - 1 - 2 * c)
    seq = lambda d, bi, c: (bi, cidx(d, c), 0)
    return pl.pallas_call(
        functools.partial(_gdn_scan_body, n_chunks=n_chunks),
        grid=(2, b, n_chunks),
        in_specs=[pl.BlockSpec((None, GD_CHUNK, w), seq),
                  pl.BlockSpec((None, GD_CHUNK, w), seq),
                  pl.BlockSpec((None, GD_CHUNK, w), seq),
                  pl.BlockSpec((None, None, GD_CHUNK, 128), lambda d, bi, c: (d, bi, cidx(d, c), 0)),
                  pl.BlockSpec((None, None, None, 16, GD_CHUNK), lambda d, bi, c: (d, bi, cidx(d, c), 0, 0)),
                  pl.BlockSpec((None, None, GD_HEADS, GD_DK, GD_DK), lambda d, bi, c: (d, bi, 0, 0, 0))],
        out_specs=[pl.BlockSpec((None, None, GD_CHUNK, w), lambda d, bi, c: (d, bi, cidx(d, c), 0)),
                   pl.BlockSpec((None, None, GD_HEADS, GD_DK, GD_DK), lambda d, bi, c: (d, bi, 0, 0, 0))],
        out_shape=[jax.ShapeDtypeStruct((2, b, length, w), F32),
                   jax.ShapeDtypeStruct((2, b, GD_HEADS, GD_DK, GD_DK), F32)],
        scratch_shapes=[pltpu.VMEM((GD_HEADS, GD_DK, GD_DK), F32)],
        compiler_params=_cparams("parallel", "parallel", "arbitrary"),
        name="gdn_scan",
    )(q, k, v, gb, gbt, s0)


def _gdn_inputs(x, scale, shift, rows_per_group, batch, length, p, j):
    m, d = x.shape
    wd = GD_HEADS * GD_DK
    in_w = p["gd_in_w"][j]
    u = _mm(x, in_w[:, :4 * wd].astype(BF16), mod=(scale, shift), rows_per_group=rows_per_group)
    w_ab = jnp.pad(in_w[:, 4 * wd:], ((0, 0), (0, 128 - 4 * GD_HEADS))).astype(BF16)
    ab = _mm(x, w_ab, mod=(scale, shift), rows_per_group=rows_per_group)
    qscale = GD_DK ** -0.5
    post = lambda qc, kc, vc: (_head_l2(_silu(qc), qscale), _head_l2(_silu(kc), 1.0), _silu(vc))
    q, k, v = _conv3(u, p["gd_conv"][j], [0, wd, 2 * wd], 512, length, post, 3, name="gd_conv")
    nh = GD_HEADS
    a_par = jnp.zeros((1, 1, 128), F32).at[0, 0, :2 * nh].set(-jnp.exp(p["gd_a_log"][j]).reshape(-1))
    dt_par = jnp.zeros((1, 1, 128), F32).at[0, 0, :2 * nh].set(p["gd_dt_bias"][j].reshape(-1))

    def gate_fn(abt, an, dtb):
        pre = abt + dtb
        sp = jnp.maximum(pre, 0.0) + jnp.log(1.0 + jnp.exp(-jnp.abs(pre)))
        lane = lax.broadcasted_iota(jnp.int32, abt.shape, 1)
        return (jnp.where(lane < 2 * nh, an * sp, jax.nn.sigmoid(abt)),)

    gall = _rowwise(gate_fn, [ab], [a_par, dt_par], [128], name="gd_gate")[0]
    pad = jnp.zeros((m, 128 - 2 * nh), F32)
    gb = jnp.stack([jnp.concatenate([gall[:, dr * nh:(dr + 1) * nh],
                                     gall[:, (2 + dr) * nh:(3 + dr) * nh], pad], -1) for dr in range(2)])
    gb = gb.reshape(2, batch, length, 128)
    gbt = jnp.swapaxes(gb[..., :2 * nh].reshape(2, batch, length // GD_CHUNK, GD_CHUNK, 2 * nh), -1, -2)
    rs = lambda t: t.reshape(batch, length, wd)
    return rs(q), rs(k), rs(v), gb, gbt, u


def _gdn_out(o2, u, norm_g, out_w):
    m, wd = o2.shape[1], o2.shape[2]
    ng = jnp.tile(norm_g, wd // norm_g.shape[0]).reshape(1, 1, wd)

    def body(of_ref, ob_ref, z_ref, ng_ref, o_ref):
        o = of_ref[...] + ob_ref[...]
        outs = []
        for h in range(wd // GD_DK):
            oh = o[:, h * GD_DK:(h + 1) * GD_DK]
            outs.append(oh * lax.rsqrt(jnp.mean(oh * oh, -1, keepdims=True) + 1e-6))
        o_ref[...] = jnp.concatenate(outs, -1) * ng_ref[...] * _silu(z_ref[...])

    tm = math.gcd(512, m)
    g = pl.pallas_call(
        body,
        grid=(m // tm,),
        in_specs=[pl.BlockSpec((None, tm, wd), lambda i: (0, i, 0)),
                  pl.BlockSpec((None, tm, wd), lambda i: (1, i, 0)),
                  pl.BlockSpec((tm, wd), lambda i: (i, 3)),
                  pl.BlockSpec((None, 1, wd), lambda i: (0, 0, 0))],
        out_specs=pl.BlockSpec((tm, wd), lambda i: (i, 0)),
        out_shape=jax.ShapeDtypeStruct((m, wd), F32),
        compiler_params=_cparams("parallel"),
        name="gd_norm",
    )(o2, o2, u, ng)
    return _mm(g, out_w.astype(BF16))


def _topk_rows(s, n_take, val_ref, idx_ref, base, rid=None):
    if rid is None:
        rid = lax.broadcasted_iota(jnp.int32, s.shape, 0)
    for t in range(n_take):
        mx = jnp.max(s, 0, keepdims=True)
        am = jnp.min(jnp.where(s == mx, rid, jnp.iinfo(jnp.int32).max), 0, keepdims=True)
        val_ref[base + t:base + t + 1, :] = mx
        idx_ref[base + t:base + t + 1, :] = am
        s = jnp.where(rid == am, -jnp.inf, s)


def _pk_topk_body(q_ref, keys_ref, i1_ref, i2_ref, gate_ref, sv_scr, si_scr, cv_scr, ci_scr,
                  i1p_scr, i2p_scr, gp_scr):
    kk = PK_TOPK
    tt = q_ref.shape[0]
    for h in range(PK_HEADS):
        for p in range(2):
            qh = q_ref[:, (2 * h + p) * PK_DH:(2 * h + p + 1) * PK_DH].astype(BF16)
            st = lax.dot_general(keys_ref[h, p], qh, (((1,), (1,)), ((), ())),
                                 preferred_element_type=F32)
            _topk_rows(st, kk, sv_scr, si_scr, p * kk)
        sv1, sv2 = sv_scr[0:kk, :], sv_scr[kk:2 * kk, :]
        si1, si2 = si_scr[0:kk, :], si_scr[kk:2 * kk, :]
        r8 = lax.broadcasted_iota(jnp.int32, (8, tt), 0)
        cand = jnp.concatenate([sv1[0:8, :] + sv2[r2:r2 + 1, :] for r2 in range(8)]
                               + [sv1[8:kk, :] + sv2[0:1, :], sv1[0:1, :] + sv2[8:kk, :]], 0)
        cid = jnp.concatenate([r8 * kk + r2 for r2 in range(8)] + [(r8 + 8) * kk, r8 + 8], 0)
        _topk_rows(cand, kk, cv_scr, ci_scr, 0, cid)
        cv, ci = cv_scr[...], ci_scr[...]
        a1, a2 = lax.shift_right_logical(ci, int(math.log2(kk))), ci & (kk - 1)
        i1 = jnp.zeros((kk, tt), jnp.int32)
        i2 = jnp.zeros((kk, tt), jnp.int32)
        for r in range(kk):
            i1 = i1 + jnp.where(a1 == r, si1[r:r + 1, :], 0)
            i2 = i2 + jnp.where(a2 == r, si2[r:r + 1, :], 0)
        i1p_scr[h * kk:(h + 1) * kk, :] = i1
        i2p_scr[h * kk:(h + 1) * kk, :] = i2
        e = jnp.exp(cv - jnp.max(cv, 0, keepdims=True))
        gp_scr[h * kk:(h + 1) * kk, :] = e / jnp.sum(e, 0, keepdims=True)
    i1_ref[...] = i1p_scr[...].T
    i2_ref[...] = i2p_scr[...].T
    gate_ref[...] = gp_scr[...].T


def _pk_topk(q, keys_bf16):
    m = q.shape[0]
    tt = PK_TOK
    hk = PK_HEADS * PK_TOPK
    assert hk == tt
    return pl.pallas_call(
        _pk_topk_body,
        grid=(m // tt,),
        in_specs=[pl.BlockSpec((tt, q.shape[1]), lambda i: (i, 0)),
                  pl.BlockSpec(keys_bf16.shape, lambda i: (0, 0, 0, 0))],
        out_specs=[pl.BlockSpec((tt, hk), lambda i: (i, 0))] * 3,
        out_shape=[jax.ShapeDtypeStruct((m, hk), jnp.int32), jax.ShapeDtypeStruct((m, hk), jnp.int32),
                   jax.ShapeDtypeStruct((m, hk), F32)],
        scratch_shapes=[pltpu.VMEM((2 * PK_TOPK, tt), F32), pltpu.VMEM((2 * PK_TOPK, tt), jnp.int32),
                        pltpu.VMEM((PK_TOPK, tt), F32), pltpu.VMEM((PK_TOPK, tt), jnp.int32),
                        pltpu.VMEM((hk, tt), jnp.int32), pltpu.VMEM((hk, tt), jnp.int32),
                        pltpu.VMEM((hk, tt), F32)],
        compiler_params=_cparams("parallel"),
        name="pk_topk",
    )(q, keys_bf16)


def _pk_dense_body(x_ref, sc_ref, sh_ref, gt_ref, lg_ref, lb_ref, i1_ref, i2_ref, gate_ref, ut_ref, v_ref,
                   o_ref, hb_scr, act_scr, w_scr, grid_scr, acc_scr, *, alpha, n_chunks):
    e = pl.program_id(1)
    tm = x_ref.shape[0]
    nk = PK_NKEYS
    per = ut_ref.shape[1] // nk

    @pl.when(e == 0)
    def _():
        hb_scr[...] = (x_ref[...] * (1.0 + sc_ref[...]) + sh_ref[...]).astype(BF16)
        act_scr[...] = jnp.zeros_like(act_scr)

    @pl.when(e < n_chunks)
    def _():
        hb = hb_scr[...]
        i1 = i1_ref[...]
        i2 = i2_ref[...]
        act = act_scr[...]
        for kp in range(per // 2):
            s = _dot(hb, ut_ref[:, 2 * kp * nk:2 * (kp + 1) * nk])
            for k in (2 * kp, 2 * kp + 1):
                got = jnp.take_along_axis(s[:, (k % 2) * nk:(k % 2 + 1) * nk], i2, axis=1)
                act = jnp.where(i1 == e * per + k, got, act)
        act_scr[...] = act

    @pl.when(e == n_chunks)
    def _():
        w_scr[...] = jax.nn.gelu(act_scr[...]) * gate_ref[...]
        acc_scr[...] = jnp.zeros_like(acc_scr)
        sub = lax.broadcasted_iota(jnp.int32, (nk, nk), 0)
        key1 = jnp.where(sub < nk // 2, 2 * sub, 2 * sub - (nk - 1))

        def tok(tg, carry):
            for u in range(PK_UNROLL):
                t = tg * PK_UNROLL + u
                wrow = w_scr[pl.ds(t, 1), :]
                at = jnp.where(key1 == i1_ref[pl.ds(t, 1), :], wrow, 0.0).astype(BF16)
                bt = (sub == i2_ref[pl.ds(t, 1), :]).astype(BF16)
                g = lax.dot_general(at, bt, (((1,), (1,)), ((), ())), preferred_element_type=F32)
                g = lax.bitcast_convert_type(g.astype(BF16).astype(F32), jnp.int32)
                grid_scr[t] = lax.shift_right_logical(g[:nk // 2], 16) | (g[nk // 2:] & jnp.int32(-65536))
            return carry

        lax.fori_loop(0, tm // PK_UNROLL, tok, 0)

    @pl.when(e >= n_chunks)
    def _():
        c = e - n_chunks
        parts = []
        for kp in range(per // 2):
            wd = grid_scr[:, c * (per // 2) + kp, :]
            parts.append(lax.bitcast_convert_type(lax.shift_left(wd, 16), F32))
            parts.append(lax.bitcast_convert_type(wd & jnp.int32(-65536), F32))
        acc_scr[...] += _dot(jnp.concatenate(parts, -1).astype(BF16), v_ref[...])

    @pl.when(e == 2 * n_chunks - 1)
    def _():
        o_ref[...] = _ln(alpha * x_ref[...] + gt_ref[...] * acc_scr[...], lg_ref[...], lb_ref[...])


def _pk_dense(x, scale, shift, gate_vec, ln_g, ln_b, i1, i2, gate, ut_bf16, v_bf16, rows_per_group, alpha):
    m, d = x.shape
    n_exp = v_bf16.shape[0]
    tm = math.gcd(PK_TM, m, rows_per_group)
    n_chunks = n_exp // PK_EC
    g = rows_per_group // tm
    assert n_exp % PK_EC == 0 and PK_EC % PK_NKEYS == 0
    row = lambda i, e: (i, 0)
    vec = lambda i, e: (i // g, 0, 0)
    one = lambda i, e: (0, 0, 0)
    npk = i1.shape[1]
    return pl.pallas_call(
        functools.partial(_pk_dense_body, alpha=alpha, n_chunks=n_chunks),
        grid=(m // tm, 2 * n_chunks),
        in_specs=[pl.BlockSpec((tm, d), row),
                  pl.BlockSpec((None, 1, d), vec), pl.BlockSpec((None, 1, d), vec), pl.BlockSpec((None, 1, d), vec),
                  pl.BlockSpec((None, 1, d), one), pl.BlockSpec((None, 1, d), one),
                  pl.BlockSpec((tm, npk), row), pl.BlockSpec((tm, npk), row), pl.BlockSpec((tm, npk), row),
                  pl.BlockSpec((d, PK_EC), lambda i, e: (0, jnp.minimum(e, n_chunks - 1))),
                  pl.BlockSpec((PK_EC, d), lambda i, e: (jnp.maximum(e - n_chunks, 0), 0))],
        out_specs=pl.BlockSpec((tm, d), row),
        out_shape=jax.ShapeDtypeStruct((m, d), F32),
        scratch_shapes=[pltpu.VMEM((tm, d), BF16), pltpu.VMEM((tm, npk), F32), pltpu.VMEM((tm, npk), F32),
                        pltpu.VMEM((tm, PK_NKEYS // 2, PK_NKEYS), jnp.int32), pltpu.VMEM((tm, d), F32)],
        compiler_params=_cparams("parallel", "arbitrary"),
        name="pk_dense",
    )(x, scale, shift, gate_vec, ln_g, ln_b, i1, i2, gate, ut_bf16, v_bf16)


def _peer_ln(x, scale, shift, gate_vec, ln_g, ln_b, rows_per_group, alpha, wq_bf16, keys_bf16, ut_bf16, v_bf16):
    q = _mm(x, wq_bf16, mod=(scale, shift), rows_per_group=rows_per_group)
    i1, i2, gate = _pk_topk(q, keys_bf16)
    return _pk_dense(x, scale, shift, gate_vec, ln_g, ln_b, i1, i2, gate, ut_bf16, v_bf16, rows_per_group, alpha)


def kernel(x, c, ctx, c_ctx, ada_w, ada_b, ln_g, ln_b, pk_wq, pk_keys, pk_u, pk_v, hy_in_w, hy_in_b, hy_conv, hy_f_w1, hy_f_b1, hy_f_w2, hy_f_b2, hy_f_w3, hy_skip, hy_out_w, hy_out_b, gd_in_w, gd_conv, gd_a_log, gd_dt_bias, gd_norm_g, gd_out_w, fn_out_w, fn_out_b):
    p = dict(hy_in_w=hy_in_w, hy_in_b=hy_in_b, hy_conv=hy_conv, hy_f_w1=hy_f_w1, hy_f_b1=hy_f_b1,
             hy_f_w2=hy_f_w2, hy_f_b2=hy_f_b2, hy_f_w3=hy_f_w3, hy_skip=hy_skip, hy_out_w=hy_out_w,
             hy_out_b=hy_out_b, gd_in_w=gd_in_w, gd_conv=gd_conv, gd_a_log=gd_a_log, gd_dt_bias=gd_dt_bias,
             fn_out_w=fn_out_w, fn_out_b=fn_out_b)
    b, length, d = x.shape
    lc = ctx.shape[1]
    depth = ada_w.shape[0]
    alpha = (2 * depth) ** 0.25
    xl = _pos_add(x)
    xc = ctx.reshape(b * lc, d)
    gdn_layers = [i for i in range(depth) if i % N_MIXERS == 1]
    ctx_until = gdn_layers[-1] if gdn_layers else -1
    cond = jnp.concatenate([c, c_ctx[None], jnp.zeros((8 - b - 1, d), F32)], 0)
    for i in range(depth):
        kind, j = i % N_MIXERS, i // N_MIXERS
        ctx_in, ctx_out = i <= ctx_until, i < ctx_until
        mod = _mm(cond, ada_w[i].astype(BF16), bias=ada_b[i], silu_in=True, tn=2048)
        mod = mod.reshape(8, N_MOD, 1, d)
        ml = [mod[:b, t] for t in range(N_MOD)]
        mc = [mod[b:b + 1, t] for t in range(N_MOD)]
        lg = [ln_g[i, t].reshape(1, 1, d) for t in range(2)]
        lb = [ln_b[i, t].reshape(1, 1, d) for t in range(2)]
        yc = None
        if kind == 0:
            yl = _hyena(xl, ml[1], ml[0], length, length, p, j)
            if ctx_out:
                yc = _hyena(xc, mc[1], mc[0], b * lc, lc, p, j)
        elif kind == 1:
            qc, kc, vc, gbc, gbtc, uc = _gdn_inputs(xc, mc[1], mc[0], b * lc, b, lc, p, j)
            ql, kl, vl, gbl, gbtl, ul = _gdn_inputs(xl, ml[1], ml[0], length, b, length, p, j)
            s0 = jnp.zeros((2, b, GD_HEADS, GD_DK, GD_DK), F32)
            oc, s_ctx = _gdn_scan(qc, kc, vc, gbc, gbtc, s0)
            ol, _ = _gdn_scan(ql, kl, vl, gbl, gbtl, s_ctx)
            yl = _gdn_out(ol.reshape(2, b * length, -1), ul, gd_norm_g[j], gd_out_w[j])
            if ctx_out:
                yc = _gdn_out(oc.reshape(2, b * lc, -1), uc, gd_norm_g[j], gd_out_w[j])
        else:
            yl = _fnet(xl, ml[1], ml[0], length, b, length, p, j)
            if ctx_out:
                yc = _fnet(xc, mc[1], mc[0], b * lc, b, lc, p, j)
        wq = pk_wq[i].astype(BF16)
        keys = pk_keys[i].astype(BF16)
        ut, vt = pk_u[i].astype(BF16).T, pk_v[i].astype(BF16)
        xl = _res_ln(xl, yl, ml[2], lg[0], lb[0], alpha, length)
        xl = _peer_ln(xl, ml[4], ml[3], ml[5], lg[1], lb[1], length, alpha, wq, keys, ut, vt)
        if ctx_out:
            xc = _res_ln(xc, yc, mc[2], lg[0], lb[0], alpha, b * lc)
            xc = _peer_ln(xc, mc[4], mc[3], mc[5], lg[1], lb[1], b * lc, alpha, wq, keys, ut, vt)
    return xl.reshape(b, length, d)
```

```python
import functools
import math

import numpy as np
import jax
import jax.numpy as jnp
from jax import lax
from jax.experimental import pallas as pl
from jax.experimental.pallas import tpu as pltpu

F32 = jnp.float32
BF16 = jnp.bfloat16

GRID_W = 64
N_MIXERS = 3
N_MOD = 6
LN_EPS = 1e-5
HY_EMB = 33
HY_BANDS = (HY_EMB - 1) // 2
HY_SHIFT = 0.05
HY_TARGET = 1e-2
HY_MIN_DECAY = math.log(HY_TARGET) / 1.5
HY_MAX_DECAY = math.log(HY_TARGET) / 0.3
GD_HEADS = 8
GD_DK = 128
GD_CHUNK = 64
FN_GROUPS = 4
PK_HEADS = 8
PK_NKEYS = 128
PK_DH = 128
PK_TOPK = 16
PK_TOK = 128
PK_TM = 512
PK_EC = 2048
PK_UNROLL = 16

VMEM_LIMIT_BYTES = 56 * 1024 * 1024


def _cparams(*sem):
    return pltpu.CompilerParams(dimension_semantics=sem, vmem_limit_bytes=VMEM_LIMIT_BYTES)


def _dot(a, b):
    return jnp.dot(a, b, preferred_element_type=F32)


def _dot_hi(a, b):
    return jnp.dot(a, b, preferred_element_type=F32, precision=lax.Precision.HIGHEST)


def _split(a):
    hi = a.astype(BF16)
    lo = (a - hi.astype(F32)).astype(BF16)
    return hi, lo


def _dot3(a, b):
    ah, al = _split(a)
    bh, bl = _split(b)
    return _dot(ah, bh) + (_dot(ah, bl) + _dot(al, bh))


def _mm_body(*refs, has_mod, has_bias, silu_in):
    a_ref, w_ref = refs[0], refs[1]
    k = 2
    a = a_ref[...]
    if has_mod:
        a = a * (1.0 + refs[k][...]) + refs[k + 1][...]
        k += 2
    if silu_in:
        a = a * jax.nn.sigmoid(a)
    o = _dot(a.astype(BF16), w_ref[...])
    if has_bias:
        o = o + refs[k][...]
        k += 1
    refs[k][...] = o


def _mm(a, w_bf16, bias=None, mod=None, rows_per_group=None, silu_in=False, tm=512, tn=None):
    m, k = a.shape
    n = w_bf16.shape[1]
    tm = math.gcd(tm, m, rows_per_group or m)
    if tn is None:
        tn = max(t for t in range(128, min(n, 2048) + 1, 128) if n % t == 0)
    assert m % tm == 0 and n % tn == 0
    ins = [a, w_bf16]
    specs = [pl.BlockSpec((tm, k), lambda j, i: (i, 0)), pl.BlockSpec((k, tn), lambda j, i: (0, j))]
    if mod is not None:
        assert rows_per_group % tm == 0
        g = rows_per_group // tm
        for v in mod:
            ins.append(v)
            specs.append(pl.BlockSpec((None, 1, k), lambda j, i: (i // g, 0, 0)))
    if bias is not None:
        ins.append(bias.reshape(1, n))
        specs.append(pl.BlockSpec((1, tn), lambda j, i: (0, j)))
    return pl.pallas_call(
        functools.partial(_mm_body, has_mod=mod is not None, has_bias=bias is not None, silu_in=silu_in),
        grid=(n // tn, m // tm),
        in_specs=specs,
        out_specs=pl.BlockSpec((tm, tn), lambda j, i: (i, j)),
        out_shape=jax.ShapeDtypeStruct((m, n), F32),
        compiler_params=_cparams("parallel", "parallel"),
        name="mm",
    )(*ins)


def _rowwise(fn, rows, vecs, out_cols, rows_per_group=None, tm=512, name="rowwise"):
    m = rows[0].shape[0]
    tm = math.gcd(tm, m, rows_per_group or m)
    assert m % tm == 0
    n_r, n_v, n_o = len(rows), len(vecs), len(out_cols)

    def body(*refs):
        outs = fn(*[r[...] for r in refs[:n_r + n_v]])
        for o_ref, o in zip(refs[n_r + n_v:], outs):
            o_ref[...] = o

    specs = [pl.BlockSpec((tm, r.shape[1]), lambda i: (i, 0)) for r in rows]
    for v in vecs:
        if v.shape[0] == 1:
            specs.append(pl.BlockSpec((None, 1, v.shape[2]), lambda i: (0, 0, 0)))
        else:
            assert rows_per_group % tm == 0
            g = rows_per_group // tm
            specs.append(pl.BlockSpec((None, 1, v.shape[2]), lambda i, g=g: (i // g, 0, 0)))
    return pl.pallas_call(
        body,
        grid=(m // tm,),
        in_specs=specs,
        out_specs=[pl.BlockSpec((tm, c), lambda i: (i, 0)) for c in out_cols],
        out_shape=[jax.ShapeDtypeStruct((m, c), F32) for c in out_cols],
        compiler_params=_cparams("parallel"),
        name=name,
    )(*rows, *vecs)


def _ln(v, g, b):
    mu = jnp.mean(v, -1, keepdims=True)
    d = v - mu
    var = jnp.mean(d * d, -1, keepdims=True)
    return d * lax.rsqrt(var + LN_EPS) * g + b


def _res_ln(x, y, gate, ln_g, ln_b, alpha, rows_per_group):
    fn = lambda xt, yt, gt, lg, lb: (_ln(alpha * xt + gt * yt, lg, lb),)
    return _rowwise(fn, [x, y], [gate, ln_g, ln_b], [x.shape[1]], rows_per_group, name="res_ln")[0]


def _pos_add_body(x_ref, er_ref, ec_ref, o_ref):
    half = er_ref.shape[-1]
    x = x_ref[...]
    er = jnp.broadcast_to(er_ref[...], x.shape[:2] + (half,))
    ec = jnp.broadcast_to(ec_ref[...][None], x.shape[:2] + (half,))
    o_ref[...] = x + jnp.concatenate([er, ec], -1)


def _pos_add(x):
    b, length, d = x.shape
    rows = length // GRID_W
    quarter = d // 4
    omega = 1.0 / (10000.0 ** (jnp.arange(quarter, dtype=F32) / quarter))
    er = jnp.arange(rows, dtype=F32)[:, None] * omega
    ec = jnp.arange(GRID_W, dtype=F32)[:, None] * omega
    emb_r = jnp.concatenate([jnp.sin(er), jnp.cos(er)], -1).reshape(rows, 1, d // 2)
    emb_c = jnp.concatenate([jnp.sin(ec), jnp.cos(ec)], -1)
    rt = 8
    x4 = x.reshape(b, rows, GRID_W, d)
    out = pl.pallas_call(
        _pos_add_body,
        grid=(b, rows // rt),
        in_specs=[pl.BlockSpec((None, rt, GRID_W, d), lambda i, j: (i, j, 0, 0)),
                  pl.BlockSpec((rt, 1, d // 2), lambda i, j: (j, 0, 0)),
                  pl.BlockSpec((GRID_W, d // 2), lambda i, j: (0, 0))],
        out_specs=pl.BlockSpec((None, rt, GRID_W, d), lambda i, j: (i, j, 0, 0)),
        out_shape=jax.ShapeDtypeStruct(x4.shape, F32),
        compiler_params=_cparams("parallel", "parallel"),
        name="pos_add",
    )(x4, emb_r, emb_c)
    return out.reshape(b * length, d)


def _conv3_body(*refs, n_parts, tm, seq_len, post, n_vec):
    i = pl.program_id(0)
    first = (i * tm) % seq_len == 0
    last = ((i + 1) * tm) % seq_len == 0
    row = lax.broadcasted_iota(jnp.int32, (tm, 1), 0)
    parts = []
    for p in range(n_parts):
        main_ref, prev_ref, next_ref, w_ref = refs[4 * p:4 * p + 4]
        u = main_ref[...]
        w = w_ref[...]
        prev = jnp.where(first, 0.0, prev_ref[7:8, :])
        nxt = jnp.where(last, 0.0, next_ref[0:1, :])
        up = jnp.where(row == 0, prev, pltpu.roll(u, 1, 0))
        dn = jnp.where(row == tm - 1, nxt, pltpu.roll(u, tm - 1, 0))
        parts.append(up * w[0:1, :] + u * w[1:2, :] + dn * w[2:3, :])
    k = 4 * n_parts
    vecs = [refs[k + j][...] for j in range(n_vec)]
    outs = post(*parts, *vecs)
    for o_ref, o in zip(refs[k + n_vec:], outs):
        o_ref[...] = o


def _conv3(u, w, col_parts, tc, seq_len, post, n_out, vecs=(), tm=256, name="conv3"):
    m = u.shape[0]
    tm = min(tm, seq_len)
    assert seq_len % tm == 0 and m % tm == 0 and tm % 8 == 0
    width = col_parts[1] - col_parts[0] if len(col_parts) > 1 else tc
    ncol = width // tc
    t8 = tm // 8
    nb8 = m // 8
    ins, specs = [], []
    for c0 in col_parts:
        cb = c0 // tc
        ins += [u, u, u, w]
        specs += [
            pl.BlockSpec((tm, tc), lambda i, j, cb=cb: (i, cb + j)),
            pl.BlockSpec((8, tc), lambda i, j, cb=cb: (jnp.maximum(i * t8 - 1, 0), cb + j)),
            pl.BlockSpec((8, tc), lambda i, j, cb=cb: (jnp.minimum((i + 1) * t8, nb8 - 1), cb + j)),
            pl.BlockSpec((3, tc), lambda i, j, cb=cb: (0, cb + j)),
        ]
    for v in vecs:
        ins.append(v)
        specs.append(pl.BlockSpec((1, tc), lambda i, j: (0, j)))
    return pl.pallas_call(
        functools.partial(_conv3_body, n_parts=len(col_parts), tm=tm, seq_len=seq_len, post=post,
                          n_vec=len(vecs)),
        grid=(m // tm, ncol),
        in_specs=specs,
        out_specs=[pl.BlockSpec((tm, tc), lambda i, j: (i, j)) for _ in range(n_out)],
        out_shape=[jax.ShapeDtypeStruct((m, width), F32) for _ in range(n_out)],
        compiler_params=_cparams("parallel", "parallel"),
        name=name,
    )(*ins)


def _lmm_body(*refs, n_x):
    w_ref, o_ref = refs[0], refs[1 + n_x]
    tc = o_ref.shape[-1]
    xs = [r[...].reshape(-1, tc) for r in refs[1:1 + n_x]]
    x = xs[0] if n_x == 1 else jnp.concatenate(xs, 0)
    w = w_ref[...]
    w = w.reshape(w.shape[-2], w.shape[-1])
    o_ref[...] = _dot(w, x.astype(BF16)).reshape(o_ref.shape).astype(o_ref.dtype)


def _lmm(w, w_spec, xs, x_specs, out_shape, out_spec, grid, name, out_dtype=F32):
    return pl.pallas_call(
        functools.partial(_lmm_body, n_x=len(xs)),
        grid=grid,
        in_specs=[w_spec] + list(x_specs),
        out_specs=out_spec,
        out_shape=jax.ShapeDtypeStruct(out_shape, out_dtype),
        compiler_params=_cparams(*(["parallel"] * len(grid))),
        name=name,
    )(w, *xs)


def _spec_mul_body(w1_ref, w2_ref, x_ref, h_ref, o_ref):
    tc = o_ref.shape[-1]
    x = x_ref[...].reshape(-1, tc)
    z = _dot(w1_ref[...], x.astype(BF16))
    half = z.shape[0] // 2
    zr, zi = z[:half], z[half:]
    hr, hi = h_ref[0], h_ref[1]
    y = jnp.concatenate([zr * hr - zi * hi, zr * hi + zi * hr], 0)
    o_ref[...] = _dot(w2_ref[...], y.astype(BF16)).reshape(o_ref.shape).astype(o_ref.dtype)


def _cplx_mat(ang):
    c, s = jnp.cos(ang), jnp.sin(ang)
    return jnp.concatenate([jnp.concatenate([c, -s], -1), jnp.concatenate([s, c], -1)], -2)


def _phase(k, n):
    return (k % n).astype(F32) * (2.0 * math.pi / n)


def _outer_mats(n1, n2, n_in, sign):
    n = n1 * n2
    s1 = jnp.arange(n1, dtype=jnp.int32)[:, None, None]
    f2 = jnp.arange(n2, dtype=jnp.int32)[None, :, None]
    s2 = jnp.arange(n_in, dtype=jnp.int32)[None, None, :]
    return sign * _phase(f2 * (s1 + n1 * s2), n)


def _inner_phase(n1, sign):
    a = jnp.arange(n1, dtype=jnp.int32)
    return sign * _phase(a[:, None] * a[None, :], n1)


def _split_len(n):
    n1 = 1 << (int(math.log2(n)) // 2)
    return n1, n // n1


def _fft_conv_pair(z2, hspec, n1, n2, tcol=512):
    _, length, c = z2.shape
    n = 2 * length
    assert n1 * n2 == n
    h2 = n2 // 2
    m_in = _cplx_mat(_outer_mats(n1, n2, h2, -1.0)).astype(BF16)
    a = _lmm(m_in, pl.BlockSpec((1, 2 * n2, n2), lambda s: (s, 0, 0)),
             [z2.reshape(2, h2, n1 * c)], [pl.BlockSpec((2, h2, c), lambda s: (0, 0, s))],
             (2, n1, n2, c), pl.BlockSpec((2, 1, n2, c), lambda s: (0, s, 0, 0)), (n1,), "fft_in", BF16)
    w1 = _cplx_mat(_inner_phase(n1, -1.0)).astype(BF16)
    w2 = (_cplx_mat(_inner_phase(n1, 1.0)) * (1.0 / n)).astype(BF16)
    cols = n2 * c
    tcol = min(tcol, cols)
    b = pl.pallas_call(
        _spec_mul_body,
        grid=(cols // tcol,),
        in_specs=[pl.BlockSpec((2 * n1, 2 * n1), lambda j: (0, 0)),
                  pl.BlockSpec((2 * n1, 2 * n1), lambda j: (0, 0)),
                  pl.BlockSpec((2, n1, tcol), lambda j: (0, 0, j)),
                  pl.BlockSpec((2, n1, tcol), lambda j: (0, 0, j))],
        out_specs=pl.BlockSpec((2, n1, tcol), lambda j: (0, 0, j)),
        out_shape=jax.ShapeDtypeStruct((2, n1, cols), BF16),
        compiler_params=_cparams("parallel"),
        name="fft_mid",
    )(w1, w2, a.reshape(2, n1, cols), hspec)
    m_out = _cplx_mat(jnp.swapaxes(_outer_mats(n1, n2, h2, 1.0), 1, 2)).astype(BF16)
    y = _lmm(m_out, pl.BlockSpec((1, n2, 2 * n2), lambda s: (s, 0, 0)),
             [b.reshape(2, n1, n2, c)], [pl.BlockSpec((2, 1, n2, c), lambda s: (0, s, 0, 0))],
             (2, h2, n1 * c), pl.BlockSpec((2, h2, c), lambda s: (0, 0, s)), (n1,), "fft_out")
    return y.reshape(2, length, c)


def _fft_real_spectrum(f, n1, n2, tcol=512):
    n, c = f.shape
    if n2 == 1:
        ph = _inner_phase(n1, -1.0)
        w = jnp.concatenate([jnp.cos(ph), jnp.sin(ph)], 0).astype(BF16)
        tcol = min(tcol, c)
        return _lmm(w, pl.BlockSpec((2 * n1, n1), lambda j: (0, 0)),
                    [f], [pl.BlockSpec((n1, tcol), lambda j: (0, j))],
                    (2, n1, c), pl.BlockSpec((2, n1, tcol), lambda j: (0, 0, j)), (c // tcol,), "fft_spec1")
    ph = _outer_mats(n1, n2, n2, -1.0)
    m_in = jnp.concatenate([jnp.cos(ph), jnp.sin(ph)], 1).astype(BF16)
    a = _lmm(m_in, pl.BlockSpec((1, 2 * n2, n2), lambda s: (s, 0, 0)),
             [f.reshape(n2, n1 * c)], [pl.BlockSpec((n2, c), lambda s: (0, s))],
             (2, n1, n2, c), pl.BlockSpec((2, 1, n2, c), lambda s: (0, s, 0, 0)), (n1,), "fft_spec_in", BF16)
    w1 = _cplx_mat(_inner_phase(n1, -1.0)).astype(BF16)
    cols = n2 * c
    tcol = min(tcol, cols)
    h = _lmm(w1, pl.BlockSpec((2 * n1, 2 * n1), lambda j: (0, 0)),
             [a.reshape(2, n1, cols)], [pl.BlockSpec((2, n1, tcol), lambda j: (0, 0, j))],
             (2, n1, cols), pl.BlockSpec((2, n1, tcol), lambda j: (0, 0, j)), (cols // tcol,), "fft_spec_mid")
    return h


def _hy_filter_body(wt_ref, wc_ref, ws_ref, b1_ref, w2_ref, b2_ref, w3_ref, bands_ref, dl_ref, o_ref,
                    *, length, tr):
    d = o_ref.shape[-1]
    j = pl.program_id(0) * tr + lax.broadcasted_iota(jnp.int32, (tr, 1), 0)
    k = jnp.where(j < length, j, 2 * length - j)
    t = k.astype(F32) / length
    ang = 2.0 * jnp.pi * t * bands_ref[...]
    pre = t * wt_ref[...] + _dot_hi(jnp.cos(ang), wc_ref[...]) + _dot_hi(-jnp.sin(ang), ws_ref[...])
    hdn = jnp.sin(pre + b1_ref[...])
    hdn = jnp.sin(_dot_hi(hdn, w2_ref[...]) + b2_ref[...])
    hf = _dot_hi(hdn, w3_ref[...])
    win = jnp.exp(-t * dl_ref[...]) + HY_SHIFT
    h = jnp.where(j < length, hf[:, :d], hf[:, d:]) * win
    o_ref[...] = jnp.where(j == length, 0.0, h)


def _hy_filter(length, f_w1, f_b1, f_w2, f_b2, f_w3):
    d = f_w3.shape[1] // 2
    ffn = f_w2.shape[0]
    tr = min(512, length)
    bands = jnp.linspace(1e-4, HY_BANDS - 1, HY_BANDS, dtype=F32).reshape(1, HY_BANDS)
    deltas = jnp.abs(jnp.linspace(HY_MIN_DECAY, HY_MAX_DECAY, d, dtype=F32)).reshape(1, d)
    ins = [f_w1[0:1], f_w1[1:1 + HY_BANDS], f_w1[1 + HY_BANDS:], f_b1.reshape(1, ffn), f_w2,
           f_b2.reshape(1, ffn), f_w3, bands, deltas]
    return pl.pallas_call(
        functools.partial(_hy_filter_body, length=length, tr=tr),
        grid=(2 * length // tr,),
        in_specs=[pl.BlockSpec(a.shape, lambda i: (0, 0)) for a in ins],
        out_specs=pl.BlockSpec((tr, d), lambda i: (i, 0)),
        out_shape=jax.ShapeDtypeStruct((2 * length, d), F32),
        compiler_params=_cparams("parallel"),
        name="hy_filter",
    )(*ins)


def _hyena(x, scale, shift, rows_per_group, length, p, j):
    m, d = x.shape
    assert m == 2 * length
    u = _mm(x, p["hy_in_w"][j].astype(BF16), bias=p["hy_in_b"][j], mod=(scale, shift),
            rows_per_group=rows_per_group)
    post = lambda x0, x1, v: (x0, v * x1)
    x0c, z = _conv3(u, p["hy_conv"][j], [0, d, 2 * d], 512, length, post, 2, name="hy_conv")
    filt = _hy_filter(length, p["hy_f_w1"][j], p["hy_f_b1"][j], p["hy_f_w2"][j], p["hy_f_b2"][j],
                      p["hy_f_w3"][j])
    n = 2 * length
    n1, n2 = (n, 1) if n <= 1024 else _split_len(n)
    hspec = _fft_real_spectrum(filt, n1, n2)
    if n2 == 1:
        z2 = jnp.pad(z.reshape(2, length, d), ((0, 0), (0, length), (0, 0)))
        y = _fft_conv_pair_single(z2, hspec, n)[:, :length]
    else:
        y = _fft_conv_pair(z.reshape(2, length, d), hspec, n1, n2)
    fn = lambda yt, zt, x0t, sk: ((yt + sk * zt) * x0t,)
    g = _rowwise(fn, [y.reshape(m, d), z, x0c], [p["hy_skip"][j].reshape(1, 1, d)], [d], name="hy_gate")[0]
    return _mm(g, p["hy_out_w"][j].astype(BF16), bias=p["hy_out_b"][j])


def _fft_conv_pair_single(z2, hspec, n, tcol=512):
    c = z2.shape[-1]
    w1 = _cplx_mat(_inner_phase(n, -1.0)).astype(BF16)
    w2 = (_cplx_mat(_inner_phase(n, 1.0)) * (1.0 / n)).astype(BF16)
    tcol = min(tcol, c)
    return pl.pallas_call(
        _spec_mul_body,
        grid=(c // tcol,),
        in_specs=[pl.BlockSpec((2 * n, 2 * n), lambda j: (0, 0)),
                  pl.BlockSpec((2 * n, 2 * n), lambda j: (0, 0)),
                  pl.BlockSpec((2, n, tcol), lambda j: (0, 0, j)),
                  pl.BlockSpec((2, n, tcol), lambda j: (0, 0, j))],
        out_specs=pl.BlockSpec((2, n, tcol), lambda j: (0, 0, j)),
        out_shape=jax.ShapeDtypeStruct((2, n, c), F32),
        compiler_params=_cparams("parallel"),
        name="fft_mid1",
    )(w1, w2, z2, hspec)


def _fnet(x, scale, shift, rows_per_group, batch, length, p, j):
    m, d = x.shape
    gc = d // FN_GROUPS
    ph = _inner_phase(gc, -1.0)
    eye = jnp.eye(FN_GROUPS, dtype=F32)
    w_c = jnp.concatenate([jnp.kron(eye, jnp.cos(ph)), jnp.kron(eye, jnp.sin(ph))], 1).astype(BF16)
    w = _mm(x, w_c, mod=(scale, shift), rows_per_group=rows_per_group)
    if length <= 1024:
        n1, n2 = length, 1
    else:
        n1, n2 = _split_len(length)
    norm = 1.0 / math.sqrt(length * gc)
    if n2 == 1:
        ph1 = _inner_phase(n1, -1.0)
        wr = (jnp.concatenate([jnp.cos(ph1), -jnp.sin(ph1)], 1) * norm).astype(BF16)
        y = _lmm(wr, pl.BlockSpec((n1, 2 * n1), lambda b, c: (0, 0)),
                 [w.reshape(batch, n1, 2 * d)] * 2,
                 [pl.BlockSpec((None, n1, d), lambda b, c: (b, 0, 0)),
                  pl.BlockSpec((None, n1, d), lambda b, c: (b, 0, 1))],
                 (batch, n1, d), pl.BlockSpec((None, n1, d), lambda b, c: (b, 0, 0)), (batch, 1), "fn_pos1")
        y = y.reshape(m, d)
    else:
        m_in = _cplx_mat(_outer_mats(n1, n2, n2, -1.0)).astype(BF16)
        wv = w.reshape(batch, n2, n1 * 2 * d)
        a = _lmm(m_in, pl.BlockSpec((1, 2 * n2, 2 * n2), lambda b, s: (s, 0, 0)),
                 [wv, wv],
                 [pl.BlockSpec((None, n2, d), lambda b, s: (b, 0, 2 * s)),
                  pl.BlockSpec((None, n2, d), lambda b, s: (b, 0, 2 * s + 1))],
                 (batch, 2, n1, n2, d), pl.BlockSpec((None, 2, 1, n2, d), lambda b, s: (b, 0, s, 0, 0)),
                 (batch, n1), "fn_pos_in", BF16)
        ph1 = _inner_phase(n1, -1.0)
        wr = (jnp.concatenate([jnp.cos(ph1), -jnp.sin(ph1)], 1) * norm).astype(BF16)
        cols = n2 * d
        tcol = 1024
        y = _lmm(wr, pl.BlockSpec((n1, 2 * n1), lambda b, c: (0, 0)),
                 [a.reshape(batch, 2, n1, cols)],
                 [pl.BlockSpec((None, 2, n1, tcol), lambda b, c: (b, 0, 0, c))],
                 (batch, n1, cols), pl.BlockSpec((None, n1, tcol), lambda b, c: (b, 0, c)),
                 (batch, cols // tcol), "fn_pos_mid")
        y = y.reshape(m, d)
    return _mm(y, p["fn_out_w"][j].astype(BF16), bias=p["fn_out_b"][j])


def _head_l2(t, extra):
    outs = []
    for h in range(t.shape[1] // GD_DK):
        th = t[:, h * GD_DK:(h + 1) * GD_DK]
        outs.append(th * (lax.rsqrt(jnp.sum(th * th, -1, keepdims=True) + 1e-6) * extra))
    return jnp.concatenate(outs, -1)


def _silu(v):
    return v * jax.nn.sigmoid(v)


def _gdn_scan_body(q_ref, k_ref, v_ref, gb_ref, gbt_ref, s0_ref, o_ref, sfin_ref, s_scr, *, n_chunks):
    direction = pl.program_id(0)
    c = pl.program_id(1)
    cs = GD_CHUNK
    nb = q_ref.shape[0]

    @pl.when(c == 0)
    def _():
        s_scr[...] = s0_ref[...]

    ri = lax.broadcasted_iota(jnp.int32, (cs, cs), 0)
    ci = lax.broadcasted_iota(jnp.int32, (cs, cs), 1)
    lag = (ri - ci) * (1 - 2 * direction)
    incl = lag >= 0
    strict = lag > 0
    tri = incl.astype(F32)
    tri_t = (lag <= 0).astype(F32)
    eye = (ri == ci).astype(F32)
    pair_masks = []
    for lvl in range(int(math.log2(cs))):
        rb, cb = lax.shift_right_logical(ri, lvl), lax.shift_right_logical(ci, lvl)
        pair_masks.append((jnp.abs(rb - cb) == 1) & ((jnp.minimum(rb, cb) & 1) == 0))
    gb = [gb_ref[b] for b in range(nb)]
    gc_cols = [_dot_hi(tri, gb[b]) for b in range(nb)]
    gc_rows = [_dot_hi(gbt_ref[b], tri_t) for b in range(nb)]
    tot = [jnp.sum(gb[b], 0, keepdims=True) for b in range(nb)]
    ch = [(b, h) for b in range(nb) for h in range(GD_HEADS)]
    hs = range(len(ch))
    nt = (((1,), (1,)), ((), ()))
    tn = (((0,), (0,)), ((), ()))
    sl = [slice(h * GD_DK, (h + 1) * GD_DK) for _, h in ch]
    k = [k_ref[b, :, sl[i]] for i, (b, h) in enumerate(ch)]
    gcol = [gc_cols[b][:, h:h + 1] for b, h in ch]
    beta = [gb[b][:, GD_HEADS + h:GD_HEADS + h + 1] for b, h in ch]
    gtot = [tot[b][:, h:h + 1] for b, h in ch]
    decay = [jnp.exp(jnp.where(incl, gcol[i] - gc_rows[b][h:h + 1, :], -jnp.inf)) for i, (b, h) in enumerate(ch)]
    eg = [jnp.exp(gcol[h]) for h in hs]
    kb = [k[h] * beta[h] for h in hs]
    kbf = [k[h].astype(BF16) for h in hs]
    kk = [lax.dot_general(kb[h].astype(BF16), kbf[h], nt, preferred_element_type=F32) for h in hs]
    a = [jnp.where(strict, kk[h] * decay[h], 0.0) for h in hs]
    inv = [eye - jnp.where(pair_masks[0], a[h], 0.0) for h in hs]
    for pm in pair_masks[1:]:
        tn_s = [_dot3(inv[h], jnp.where(pm, a[h], 0.0)) for h in hs]
        tnt = [_dot3(tn_s[h], inv[h]) for h in hs]
        inv = [inv[h] - tnt[h] for h in hs]
    rhs = [jnp.concatenate([v_ref[b, :, sl[i]] * beta[i], kb[i] * eg[i]], -1) for i, (b, h) in enumerate(ch)]
    sol = [_dot3(inv[h], rhs[h]) for h in hs]
    q = [q_ref[b, :, sl[i]] for i, (b, h) in enumerate(ch)]
    qk = [lax.dot_general(q[h].astype(BF16), kbf[h], nt, preferred_element_type=F32) for h in hs]
    sb = [s_scr[b, h].astype(BF16) for b, h in ch]
    ws = [_dot(sol[h][:, GD_DK:].astype(BF16), sb[h]) for h in hs]
    qs = [_dot((q[h] * eg[h]).astype(BF16), sb[h]) for h in hs]
    vnb = [(sol[h][:, :GD_DK] - ws[h]).astype(BF16) for h in hs]
    av = [_dot((qk[h] * decay[h]).astype(BF16), vnb[h]) for h in hs]
    kv = [lax.dot_general((k[h] * jnp.exp(gtot[h] - gcol[h])).astype(BF16), vnb[h], tn,
                          preferred_element_type=F32) for h in hs]
    for i, (b, h) in enumerate(ch):
        s_scr[b, h] = s_scr[b, h] * jnp.exp(gtot[i]) + kv[i]
    for b in range(nb):
        o_ref[b] = jnp.concatenate([qs[i] + av[i] for i, (bb, _) in enumerate(ch) if bb == b], -1)

    @pl.when(c == n_chunks - 1)
    def _():
        sfin_ref[...] = s_scr[...]


def _gdn_scan(q, k, v, gb, gbt, s0):
    b, length, w = q.shape
    n_chunks = length // GD_CHUNK
    cidx = lambda d, c: c + d * (n_chunks - 1 - 2 * c)
    seq = lambda d, c: (0, cidx(d, c), 0)
    state = pl.BlockSpec((None, b, GD_HEADS, GD_DK, GD_DK), lambda d, c: (d, 0, 0, 0, 0))
    return pl.pallas_call(
        functools.partial(_gdn_scan_body, n_chunks=n_chunks),
        grid=(2, n_chunks),
        in_specs=[pl.BlockSpec((b, GD_CHUNK, w), seq),
                  pl.BlockSpec((b, GD_CHUNK, w), seq),
                  pl.BlockSpec((b, GD_CHUNK, w), seq),
                  pl.BlockSpec((None, b, GD_CHUNK, 128), lambda d, c: (d, 0, cidx(d, c), 0)),
                  pl.BlockSpec((None, b, None, 16, GD_CHUNK), lambda d, c: (d, 0, cidx(d, c), 0, 0)),
                  state],
        out_specs=[pl.BlockSpec((None, b, GD_CHUNK, w), lambda d, c: (d, 0, cidx(d, c), 0)), state],
        out_shape=[jax.ShapeDtypeStruct((2, b, length, w), F32),
                   jax.ShapeDtypeStruct((2, b, GD_HEADS, GD_DK, GD_DK), F32)],
        scratch_shapes=[pltpu.VMEM((b, GD_HEADS, GD_DK, GD_DK), F32)],
        compiler_params=_cparams("parallel", "arbitrary"),
        name="gdn_scan",
    )(q, k, v, gb, gbt, s0)


def _gdn_inputs(x, scale, shift, rows_per_group, batch, length, p, j):
    m, d = x.shape
    wd = GD_HEADS * GD_DK
    in_w = p["gd_in_w"][j]
    u = _mm(x, in_w[:, :4 * wd].astype(BF16), mod=(scale, shift), rows_per_group=rows_per_group)
    w_ab = jnp.pad(in_w[:, 4 * wd:], ((0, 0), (0, 128 - 4 * GD_HEADS))).astype(BF16)
    ab = _mm(x, w_ab, mod=(scale, shift), rows_per_group=rows_per_group)
    qscale = GD_DK ** -0.5
    post = lambda qc, kc, vc: (_head_l2(_silu(qc), qscale), _head_l2(_silu(kc), 1.0), _silu(vc))
    q, k, v = _conv3(u, p["gd_conv"][j], [0, wd, 2 * wd], 512, length, post, 3, name="gd_conv")
    nh = GD_HEADS
    a_par = jnp.zeros((1, 1, 128), F32).at[0, 0, :2 * nh].set(-jnp.exp(p["gd_a_log"][j]).reshape(-1))
    dt_par = jnp.zeros((1, 1, 128), F32).at[0, 0, :2 * nh].set(p["gd_dt_bias"][j].reshape(-1))

    def gate_fn(abt, an, dtb):
        pre = abt + dtb
        sp = jnp.maximum(pre, 0.0) + jnp.log(1.0 + jnp.exp(-jnp.abs(pre)))
        lane = lax.broadcasted_iota(jnp.int32, abt.shape, 1)
        return (jnp.where(lane < 2 * nh, an * sp, jax.nn.sigmoid(abt)),)

    gall = _rowwise(gate_fn, [ab], [a_par, dt_par], [128], name="gd_gate")[0]
    pad = jnp.zeros((m, 128 - 2 * nh), F32)
    gb = jnp.stack([jnp.concatenate([gall[:, dr * nh:(dr + 1) * nh],
                                     gall[:, (2 + dr) * nh:(3 + dr) * nh], pad], -1) for dr in range(2)])
    gb = gb.reshape(2, batch, length, 128)
    gbt = jnp.swapaxes(gb[..., :2 * nh].reshape(2, batch, length // GD_CHUNK, GD_CHUNK, 2 * nh), -1, -2)
    rs = lambda t: t.reshape(batch, length, wd)
    return rs(q), rs(k), rs(v), gb, gbt, u


def _gdn_out(o2, u, norm_g, out_w):
    m, wd = o2.shape[1], o2.shape[2]
    ng = jnp.tile(norm_g, wd // norm_g.shape[0]).reshape(1, 1, wd)

    def body(of_ref, ob_ref, z_ref, ng_ref, o_ref):
        o = of_ref[...] + ob_ref[...]
        outs = []
        for h in range(wd // GD_DK):
            oh = o[:, h * GD_DK:(h + 1) * GD_DK]
            outs.append(oh * lax.rsqrt(jnp.mean(oh * oh, -1, keepdims=True) + 1e-6))
        o_ref[...] = jnp.concatenate(outs, -1) * ng_ref[...] * _silu(z_ref[...])

    tm = math.gcd(512, m)
    g = pl.pallas_call(
        body,
        grid=(m // tm,),
        in_specs=[pl.BlockSpec((None, tm, wd), lambda i: (0, i, 0)),
                  pl.BlockSpec((None, tm, wd), lambda i: (1, i, 0)),
                  pl.BlockSpec((tm, wd), lambda i: (i, 3)),
                  pl.BlockSpec((None, 1, wd), lambda i: (0, 0, 0))],
        out_specs=pl.BlockSpec((tm, wd), lambda i: (i, 0)),
        out_shape=jax.ShapeDtypeStruct((m, wd), F32),
        compiler_params=_cparams("parallel"),
        name="gd_norm",
    )(o2, o2, u, ng)
    return _mm(g, out_w.astype(BF16))


def _topk_rows(s, n_take, val_ref, idx_ref, base, rid=None):
    if rid is None:
        rid = lax.broadcasted_iota(jnp.int32, s.shape, 0)
    for t in range(n_take):
        mx = jnp.max(s, 0, keepdims=True)
        am = jnp.min(jnp.where(s == mx, rid, jnp.iinfo(jnp.int32).max), 0, keepdims=True)
        val_ref[base + t:base + t + 1, :] = mx
        idx_ref[base + t:base + t + 1, :] = am
        s = jnp.where(rid == am, -jnp.inf, s)


def _pk_topk_body(q_ref, keys_ref, i1_ref, i2_ref, gate_ref, sv_scr, si_scr, cv_scr, ci_scr,
                  i1p_scr, i2p_scr, gp_scr):
    kk = PK_TOPK
    tt = q_ref.shape[0]
    for h in range(PK_HEADS):
        for p in range(2):
            qh = q_ref[:, (2 * h + p) * PK_DH:(2 * h + p + 1) * PK_DH].astype(BF16)
            st = lax.dot_general(keys_ref[h, p], qh, (((1,), (1,)), ((), ())),
                                 preferred_element_type=F32)
            _topk_rows(st, kk, sv_scr, si_scr, p * kk)
        sv1, sv2 = sv_scr[0:kk, :], sv_scr[kk:2 * kk, :]
        si1, si2 = si_scr[0:kk, :], si_scr[kk:2 * kk, :]
        r8 = lax.broadcasted_iota(jnp.int32, (8, tt), 0)
        cand = jnp.concatenate([sv1[0:8, :] + sv2[r2:r2 + 1, :] for r2 in range(8)]
                               + [sv1[8:kk, :] + sv2[0:1, :], sv1[0:1, :] + sv2[8:kk, :]], 0)
        cid = jnp.concatenate([r8 * kk + r2 for r2 in range(8)] + [(r8 + 8) * kk, r8 + 8], 0)
        _topk_rows(cand, kk, cv_scr, ci_scr, 0, cid)
        cv, ci = cv_scr[...], ci_scr[...]
        a1, a2 = lax.shift_right_logical(ci, int(math.log2(kk))), ci & (kk - 1)
        i1 = jnp.zeros((kk, tt), jnp.int32)
        i2 = jnp.zeros((kk, tt), jnp.int32)
        for r in range(kk):
            i1 = i1 + jnp.where(a1 == r, si1[r:r + 1, :], 0)
            i2 = i2 + jnp.where(a2 == r, si2[r:r + 1, :], 0)
        i1p_scr[h * kk:(h + 1) * kk, :] = i1
        i2p_scr[h * kk:(h + 1) * kk, :] = i2
        e = jnp.exp(cv - jnp.max(cv, 0, keepdims=True))
        gp_scr[h * kk:(h + 1) * kk, :] = e / jnp.sum(e, 0, keepdims=True)
    i1_ref[...] = i1p_scr[...].T
    i2_ref[...] = i2p_scr[...].T
    gate_ref[...] = gp_scr[...].T


def _pk_topk(q, keys_bf16):
    m = q.shape[0]
    tt = PK_TOK
    hk = PK_HEADS * PK_TOPK
    assert hk == tt
    return pl.pallas_call(
        _pk_topk_body,
        grid=(m // tt,),
        in_specs=[pl.BlockSpec((tt, q.shape[1]), lambda i: (i, 0)),
                  pl.BlockSpec(keys_bf16.shape, lambda i: (0, 0, 0, 0))],
        out_specs=[pl.BlockSpec((tt, hk), lambda i: (i, 0))] * 3,
        out_shape=[jax.ShapeDtypeStruct((m, hk), jnp.int32), jax.ShapeDtypeStruct((m, hk), jnp.int32),
                   jax.ShapeDtypeStruct((m, hk), F32)],
        scratch_shapes=[pltpu.VMEM((2 * PK_TOPK, tt), F32), pltpu.VMEM((2 * PK_TOPK, tt), jnp.int32),
                        pltpu.VMEM((PK_TOPK, tt), F32), pltpu.VMEM((PK_TOPK, tt), jnp.int32),
                        pltpu.VMEM((hk, tt), jnp.int32), pltpu.VMEM((hk, tt), jnp.int32),
                        pltpu.VMEM((hk, tt), F32)],
        compiler_params=_cparams("parallel"),
        name="pk_topk",
    )(q, keys_bf16)


def _pk_dense_body(x_ref, sc_ref, sh_ref, gt_ref, lg_ref, lb_ref, i1_ref, i2_ref, gate_ref, ut_ref, v_ref,
                   o_ref, hb_scr, act_scr, w_scr, grid_scr, acc_scr, *, alpha, n_chunks):
    e = pl.program_id(1)
    tm = x_ref.shape[0]
    nk = PK_NKEYS
    per = ut_ref.shape[1] // nk

    @pl.when(e == 0)
    def _():
        hb_scr[...] = (x_ref[...] * (1.0 + sc_ref[...]) + sh_ref[...]).astype(BF16)
        act_scr[...] = jnp.zeros_like(act_scr)

    @pl.when(e < n_chunks)
    def _():
        hb = hb_scr[...]
        i1 = i1_ref[...]
        i2 = i2_ref[...]
        act = act_scr[...]
        for kp in range(per // 2):
            s = _dot(hb, ut_ref[:, 2 * kp * nk:2 * (kp + 1) * nk])
            for k in (2 * kp, 2 * kp + 1):
                got = jnp.take_along_axis(s[:, (k % 2) * nk:(k % 2 + 1) * nk], i2, axis=1)
                act = jnp.where(i1 == e * per + k, got, act)
        act_scr[...] = act

    @pl.when(e == n_chunks)
    def _():
        w_scr[...] = jax.nn.gelu(act_scr[...]) * gate_ref[...]
        acc_scr[...] = jnp.zeros_like(acc_scr)
        sub = lax.broadcasted_iota(jnp.int32, (nk, nk), 0)
        key1 = jnp.where(sub < nk // 2, 2 * sub, 2 * sub - (nk - 1))

        def tok(tg, carry):
            for t0 in range(0, PK_UNROLL, 8):
                words = []
                for u in range(8):
                    t = tg * PK_UNROLL + t0 + u
                    wrow = w_scr[pl.ds(t, 1), :]
                    at = jnp.where(key1 == i1_ref[pl.ds(t, 1), :], wrow, 0.0).astype(BF16)
                    bt = (sub == i2_ref[pl.ds(t, 1), :]).astype(BF16)
                    g = lax.dot_general(at, bt, (((1,), (1,)), ((), ())), preferred_element_type=F32)
                    g = lax.bitcast_convert_type(g.astype(BF16).astype(F32), jnp.int32)
                    words.append(lax.shift_right_logical(g[:nk // 2], 16) | (g[nk // 2:] & jnp.int32(-65536)))
                first = pl.multiple_of(tg * PK_UNROLL + t0, 8)
                grid_scr[:, pl.ds(first, 8), :] = jnp.swapaxes(jnp.stack(words, 0), 0, 1)
            return carry

        lax.fori_loop(0, tm // PK_UNROLL, tok, 0)

    @pl.when(e >= n_chunks)
    def _():
        c = e - n_chunks
        parts = []
        for kp in range(per // 2):
            wd = grid_scr[c * (per // 2) + kp]
            parts.append(lax.bitcast_convert_type(lax.shift_left(wd, 16), F32))
            parts.append(lax.bitcast_convert_type(wd & jnp.int32(-65536), F32))
        acc_scr[...] += _dot(jnp.concatenate(parts, -1).astype(BF16), v_ref[...])

    @pl.when(e == 2 * n_chunks - 1)
    def _():
        o_ref[...] = _ln(alpha * x_ref[...] + gt_ref[...] * acc_scr[...], lg_ref[...], lb_ref[...])


def _pk_dense(x, scale, shift, gate_vec, ln_g, ln_b, i1, i2, gate, ut_bf16, v_bf16, rows_per_group, alpha):
    m, d = x.shape
    n_exp = v_bf16.shape[0]
    tm = math.gcd(PK_TM, m, rows_per_group)
    n_chunks = n_exp // PK_EC
    g = rows_per_group // tm
    assert n_exp % PK_EC == 0 and PK_EC % PK_NKEYS == 0
    row = lambda i, e: (i, 0)
    vec = lambda i, e: (i // g, 0, 0)
    one = lambda i, e: (0, 0, 0)
    npk = i1.shape[1]
    return pl.pallas_call(
        functools.partial(_pk_dense_body, alpha=alpha, n_chunks=n_chunks),
        grid=(m // tm, 2 * n_chunks),
        in_specs=[pl.BlockSpec((tm, d), row),
                  pl.BlockSpec((None, 1, d), vec), pl.BlockSpec((None, 1, d), vec), pl.BlockSpec((None, 1, d), vec),
                  pl.BlockSpec((None, 1, d), one), pl.BlockSpec((None, 1, d), one),
                  pl.BlockSpec((tm, npk), row), pl.BlockSpec((tm, npk), row), pl.BlockSpec((tm, npk), row),
                  pl.BlockSpec((d, PK_EC), lambda i, e: (0, jnp.minimum(e, n_chunks - 1))),
                  pl.BlockSpec((PK_EC, d), lambda i, e: (jnp.maximum(e - n_chunks, 0), 0))],
        out_specs=pl.BlockSpec((tm, d), row),
        out_shape=jax.ShapeDtypeStruct((m, d), F32),
        scratch_shapes=[pltpu.VMEM((tm, d), BF16), pltpu.VMEM((tm, npk), F32), pltpu.VMEM((tm, npk), F32),
                        pltpu.VMEM((PK_NKEYS // 2, tm, PK_NKEYS), jnp.int32), pltpu.VMEM((tm, d), F32)],
        compiler_params=_cparams("parallel", "arbitrary"),
        name="pk_dense",
    )(x, scale, shift, gate_vec, ln_g, ln_b, i1, i2, gate, ut_bf16, v_bf16)


def _peer_ln(x, scale, shift, gate_vec, ln_g, ln_b, rows_per_group, alpha, wq_bf16, keys_bf16, ut_bf16, v_bf16):
    q = _mm(x, wq_bf16, mod=(scale, shift), rows_per_group=rows_per_group)
    i1, i2, gate = _pk_topk(q, keys_bf16)
    return _pk_dense(x, scale, shift, gate_vec, ln_g, ln_b, i1, i2, gate, ut_bf16, v_bf16, rows_per_group, alpha)


def kernel(x, c, ctx, c_ctx, ada_w, ada_b, ln_g, ln_b, pk_wq, pk_keys, pk_u, pk_v, hy_in_w, hy_in_b, hy_conv, hy_f_w1, hy_f_b1, hy_f_w2, hy_f_b2, hy_f_w3, hy_skip, hy_out_w, hy_out_b, gd_in_w, gd_conv, gd_a_log, gd_dt_bias, gd_norm_g, gd_out_w, fn_out_w, fn_out_b):
    p = dict(hy_in_w=hy_in_w, hy_in_b=hy_in_b, hy_conv=hy_conv, hy_f_w1=hy_f_w1, hy_f_b1=hy_f_b1,
             hy_f_w2=hy_f_w2, hy_f_b2=hy_f_b2, hy_f_w3=hy_f_w3, hy_skip=hy_skip, hy_out_w=hy_out_w,
             hy_out_b=hy_out_b, gd_in_w=gd_in_w, gd_conv=gd_conv, gd_a_log=gd_a_log, gd_dt_bias=gd_dt_bias,
             fn_out_w=fn_out_w, fn_out_b=fn_out_b)
    b, length, d = x.shape
    lc = ctx.shape[1]
    depth = ada_w.shape[0]
    alpha = (2 * depth) ** 0.25
    xl = _pos_add(x)
    xc = ctx.reshape(b * lc, d)
    gdn_layers = [i for i in range(depth) if i % N_MIXERS == 1]
    ctx_until = gdn_layers[-1] if gdn_layers else -1
    cond = jnp.concatenate([c, c_ctx[None], jnp.zeros((8 - b - 1, d), F32)], 0)
    for i in range(depth):
        kind, j = i % N_MIXERS, i // N_MIXERS
        ctx_in, ctx_out = i <= ctx_until, i < ctx_until
        mod = _mm(cond, ada_w[i].astype(BF16), bias=ada_b[i], silu_in=True, tn=2048)
        mod = mod.reshape(8, N_MOD, 1, d)
        ml = [mod[:b, t] for t in range(N_MOD)]
        mc = [mod[b:b + 1, t] for t in range(N_MOD)]
        lg = [ln_g[i, t].reshape(1, 1, d) for t in range(2)]
        lb = [ln_b[i, t].reshape(1, 1, d) for t in range(2)]
        yc = None
        if kind == 0:
            yl = _hyena(xl, ml[1], ml[0], length, length, p, j)
            if ctx_out:
                yc = _hyena(xc, mc[1], mc[0], b * lc, lc, p, j)
        elif kind == 1:
            qc, kc, vc, gbc, gbtc, uc = _gdn_inputs(xc, mc[1], mc[0], b * lc, b, lc, p, j)
            ql, kl, vl, gbl, gbtl, ul = _gdn_inputs(xl, ml[1], ml[0], length, b, length, p, j)
            s0 = jnp.zeros((2, b, GD_HEADS, GD_DK, GD_DK), F32)
            oc, s_ctx = _gdn_scan(qc, kc, vc, gbc, gbtc, s0)
            ol, _ = _gdn_scan(ql, kl, vl, gbl, gbtl, s_ctx)
            yl = _gdn_out(ol.reshape(2, b * length, -1), ul, gd_norm_g[j], gd_out_w[j])
            if ctx_out:
                yc = _gdn_out(oc.reshape(2, b * lc, -1), uc, gd_norm_g[j], gd_out_w[j])
        else:
            yl = _fnet(xl, ml[1], ml[0], length, b, length, p, j)
            if ctx_out:
                yc = _fnet(xc, mc[1], mc[0], b * lc, b, lc, p, j)
        wq = pk_wq[i].astype(BF16)
        keys = pk_keys[i].astype(BF16)
        ut, vt = pk_u[i].astype(BF16).T, pk_v[i].astype(BF16)
        xl = _res_ln(xl, yl, ml[2], lg[0], lb[0], alpha, length)
        xl = _peer_ln(xl, ml[4], ml[3], ml[5], lg[1], lb[1], length, alpha, wq, keys, ut, vt)
        if ctx_out:
            xc = _res_ln(xc, yc, mc[2], lg[0], lb[0], alpha, b * lc)
            xc = _peer_ln(xc, mc[4], mc[3], mc[5], lg[1], lb[1], b * lc, alpha, wq, keys, ut, vt)
    return xl.reshape(b, length, d)
```

```python
import functools
import math

import numpy as np
import jax
import jax.numpy as jnp
from jax import lax
from jax.experimental import pallas as pl
from jax.experimental.pallas import tpu as pltpu

F32 = jnp.float32
BF16 = jnp.bfloat16

GRID_W = 64
N_MIXERS = 3
N_MOD = 6
LN_EPS = 1e-5
HY_EMB = 33
HY_BANDS = (HY_EMB - 1) // 2
HY_SHIFT = 0.05
HY_TARGET = 1e-2
HY_MIN_DECAY = math.log(HY_TARGET) / 1.5
HY_MAX_DECAY = math.log(HY_TARGET) / 0.3
GD_HEADS = 8
GD_DK = 128
GD_CHUNK = 64
FN_GROUPS = 4
PK_HEADS = 8
PK_NKEYS = 128
PK_DH = 128
PK_TOPK = 16
PK_TOK = 128
PK_TM = 512
PK_EC = 2048
PK_UNROLL = 32

VMEM_LIMIT_BYTES = 56 * 1024 * 1024


def _cparams(*sem):
    return pltpu.CompilerParams(dimension_semantics=sem, vmem_limit_bytes=VMEM_LIMIT_BYTES)


def _dot(a, b):
    return jnp.dot(a, b, preferred_element_type=F32)


def _dot_hi(a, b):
    return jnp.dot(a, b, preferred_element_type=F32, precision=lax.Precision.HIGHEST)


def _split(a):
    hi = a.astype(BF16)
    lo = (a - hi.astype(F32)).astype(BF16)
    return hi, lo


def _dot3(a, b):
    ah, al = _split(a)
    bh, bl = _split(b)
    return _dot(ah, bh) + (_dot(ah, bl) + _dot(al, bh))


def _mm_body(*refs, has_mod, has_bias, silu_in):
    a_ref, w_ref = refs[0], refs[1]
    k = 2
    a = a_ref[...]
    if has_mod:
        a = a * (1.0 + refs[k][...]) + refs[k + 1][...]
        k += 2
    if silu_in:
        a = a * jax.nn.sigmoid(a)
    o = _dot(a.astype(BF16), w_ref[...])
    if has_bias:
        o = o + refs[k][...]
        k += 1
    refs[k][...] = o


def _mm(a, w_bf16, bias=None, mod=None, rows_per_group=None, silu_in=False, tm=1024, tn=None):
    m, k = a.shape
    n = w_bf16.shape[1]
    tm = math.gcd(tm, m, rows_per_group or m)
    if tn is None:
        tn = max(t for t in range(128, min(n, 2048) + 1, 128) if n % t == 0)
    assert m % tm == 0 and n % tn == 0
    ins = [a, w_bf16]
    specs = [pl.BlockSpec((tm, k), lambda j, i: (i, 0)), pl.BlockSpec((k, tn), lambda j, i: (0, j))]
    if mod is not None:
        assert rows_per_group % tm == 0
        g = rows_per_group // tm
        for v in mod:
            ins.append(v)
            specs.append(pl.BlockSpec((None, 1, k), lambda j, i: (i // g, 0, 0)))
    if bias is not None:
        ins.append(bias.reshape(1, n))
        specs.append(pl.BlockSpec((1, tn), lambda j, i: (0, j)))
    return pl.pallas_call(
        functools.partial(_mm_body, has_mod=mod is not None, has_bias=bias is not None, silu_in=silu_in),
        grid=(n // tn, m // tm),
        in_specs=specs,
        out_specs=pl.BlockSpec((tm, tn), lambda j, i: (i, j)),
        out_shape=jax.ShapeDtypeStruct((m, n), F32),
        compiler_params=_cparams("parallel", "parallel"),
        name="mm",
    )(*ins)


def _rowwise(fn, rows, vecs, out_cols, rows_per_group=None, tm=512, name="rowwise"):
    m = rows[0].shape[0]
    tm = math.gcd(tm, m, rows_per_group or m)
    assert m % tm == 0
    n_r, n_v, n_o = len(rows), len(vecs), len(out_cols)

    def body(*refs):
        outs = fn(*[r[...] for r in refs[:n_r + n_v]])
        for o_ref, o in zip(refs[n_r + n_v:], outs):
            o_ref[...] = o

    specs = [pl.BlockSpec((tm, r.shape[1]), lambda i: (i, 0)) for r in rows]
    for v in vecs:
        if v.shape[0] == 1:
            specs.append(pl.BlockSpec((None, 1, v.shape[2]), lambda i: (0, 0, 0)))
        else:
            assert rows_per_group % tm == 0
            g = rows_per_group // tm
            specs.append(pl.BlockSpec((None, 1, v.shape[2]), lambda i, g=g: (i // g, 0, 0)))
    return pl.pallas_call(
        body,
        grid=(m // tm,),
        in_specs=specs,
        out_specs=[pl.BlockSpec((tm, c), lambda i: (i, 0)) for c in out_cols],
        out_shape=[jax.ShapeDtypeStruct((m, c), F32) for c in out_cols],
        compiler_params=_cparams("parallel"),
        name=name,
    )(*rows, *vecs)


def _ln(v, g, b):
    mu = jnp.mean(v, -1, keepdims=True)
    d = v - mu
    var = jnp.mean(d * d, -1, keepdims=True)
    return d * lax.rsqrt(var + LN_EPS) * g + b


def _res_ln(x, y, gate, ln_g, ln_b, alpha, rows_per_group):
    fn = lambda xt, yt, gt, lg, lb: (_ln(alpha * xt + gt * yt, lg, lb),)
    return _rowwise(fn, [x, y], [gate, ln_g, ln_b], [x.shape[1]], rows_per_group, name="res_ln")[0]


def _pos_add_body(x_ref, er_ref, ec_ref, o_ref):
    half = er_ref.shape[-1]
    x = x_ref[...]
    er = jnp.broadcast_to(er_ref[...], x.shape[:2] + (half,))
    ec = jnp.broadcast_to(ec_ref[...][None], x.shape[:2] + (half,))
    o_ref[...] = x + jnp.concatenate([er, ec], -1)


def _pos_add(x):
    b, length, d = x.shape
    rows = length // GRID_W
    quarter = d // 4
    omega = 1.0 / (10000.0 ** (jnp.arange(quarter, dtype=F32) / quarter))
    er = jnp.arange(rows, dtype=F32)[:, None] * omega
    ec = jnp.arange(GRID_W, dtype=F32)[:, None] * omega
    emb_r = jnp.concatenate([jnp.sin(er), jnp.cos(er)], -1).reshape(rows, 1, d // 2)
    emb_c = jnp.concatenate([jnp.sin(ec), jnp.cos(ec)], -1)
    rt = 8
    x4 = x.reshape(b, rows, GRID_W, d)
    out = pl.pallas_call(
        _pos_add_body,
        grid=(b, rows // rt),
        in_specs=[pl.BlockSpec((None, rt, GRID_W, d), lambda i, j: (i, j, 0, 0)),
                  pl.BlockSpec((rt, 1, d // 2), lambda i, j: (j, 0, 0)),
                  pl.BlockSpec((GRID_W, d // 2), lambda i, j: (0, 0))],
        out_specs=pl.BlockSpec((None, rt, GRID_W, d), lambda i, j: (i, j, 0, 0)),
        out_shape=jax.ShapeDtypeStruct(x4.shape, F32),
        compiler_params=_cparams("parallel", "parallel"),
        name="pos_add",
    )(x4, emb_r, emb_c)
    return out.reshape(b * length, d)


def _conv3_body(*refs, n_parts, tm, seq_len, post, n_vec):
    i = pl.program_id(0)
    first = (i * tm) % seq_len == 0
    last = ((i + 1) * tm) % seq_len == 0
    row = lax.broadcasted_iota(jnp.int32, (tm, 1), 0)
    parts = []
    for p in range(n_parts):
        main_ref, prev_ref, next_ref, w_ref = refs[4 * p:4 * p + 4]
        u = main_ref[...]
        w = w_ref[...]
        prev = jnp.where(first, 0.0, prev_ref[7:8, :])
        nxt = jnp.where(last, 0.0, next_ref[0:1, :])
        up = jnp.where(row == 0, prev, pltpu.roll(u, 1, 0))
        dn = jnp.where(row == tm - 1, nxt, pltpu.roll(u, tm - 1, 0))
        parts.append(up * w[0:1, :] + u * w[1:2, :] + dn * w[2:3, :])
    k = 4 * n_parts
    vecs = [refs[k + j][...] for j in range(n_vec)]
    outs = post(*parts, *vecs)
    for o_ref, o in zip(refs[k + n_vec:], outs):
        o_ref[...] = o


def _conv3(u, w, col_parts, tc, seq_len, post, n_out, vecs=(), tm=256, name="conv3"):
    m = u.shape[0]
    tm = min(tm, seq_len)
    assert seq_len % tm == 0 and m % tm == 0 and tm % 8 == 0
    width = col_parts[1] - col_parts[0] if len(col_parts) > 1 else tc
    ncol = width // tc
    t8 = tm // 8
    nb8 = m // 8
    ins, specs = [], []
    for c0 in col_parts:
        cb = c0 // tc
        ins += [u, u, u, w]
        specs += [
            pl.BlockSpec((tm, tc), lambda i, j, cb=cb: (i, cb + j)),
            pl.BlockSpec((8, tc), lambda i, j, cb=cb: (jnp.maximum(i * t8 - 1, 0), cb + j)),
            pl.BlockSpec((8, tc), lambda i, j, cb=cb: (jnp.minimum((i + 1) * t8, nb8 - 1), cb + j)),
            pl.BlockSpec((3, tc), lambda i, j, cb=cb: (0, cb + j)),
        ]
    for v in vecs:
        ins.append(v)
        specs.append(pl.BlockSpec((1, tc), lambda i, j: (0, j)))
    return pl.pallas_call(
        functools.partial(_conv3_body, n_parts=len(col_parts), tm=tm, seq_len=seq_len, post=post,
                          n_vec=len(vecs)),
        grid=(m // tm, ncol),
        in_specs=specs,
        out_specs=[pl.BlockSpec((tm, tc), lambda i, j: (i, j)) for _ in range(n_out)],
        out_shape=[jax.ShapeDtypeStruct((m, width), F32) for _ in range(n_out)],
        compiler_params=_cparams("parallel", "parallel"),
        name=name,
    )(*ins)


def _lmm_body(*refs, n_x):
    w_ref, o_ref = refs[0], refs[1 + n_x]
    tc = o_ref.shape[-1]
    xs = [r[...].reshape(-1, tc) for r in refs[1:1 + n_x]]
    x = xs[0] if n_x == 1 else jnp.concatenate(xs, 0)
    w = w_ref[...]
    w = w.reshape(w.shape[-2], w.shape[-1])
    o_ref[...] = _dot(w, x.astype(BF16)).reshape(o_ref.shape).astype(o_ref.dtype)


def _lmm(w, w_spec, xs, x_specs, out_shape, out_spec, grid, name, out_dtype=F32):
    return pl.pallas_call(
        functools.partial(_lmm_body, n_x=len(xs)),
        grid=grid,
        in_specs=[w_spec] + list(x_specs),
        out_specs=out_spec,
        out_shape=jax.ShapeDtypeStruct(out_shape, out_dtype),
        compiler_params=_cparams(*(["parallel"] * len(grid))),
        name=name,
    )(w, *xs)


def _spec_mul_body(w1_ref, w2_ref, x_ref, h_ref, o_ref):
    tc = o_ref.shape[-1]
    x = x_ref[...].reshape(-1, tc)
    z = _dot(w1_ref[...], x.astype(BF16))
    half = z.shape[0] // 2
    zr, zi = z[:half], z[half:]
    hr, hi = h_ref[0], h_ref[1]
    y = jnp.concatenate([zr * hr - zi * hi, zr * hi + zi * hr], 0)
    o_ref[...] = _dot(w2_ref[...], y.astype(BF16)).reshape(o_ref.shape).astype(o_ref.dtype)


def _cplx_mat(ang):
    c, s = jnp.cos(ang), jnp.sin(ang)
    return jnp.concatenate([jnp.concatenate([c, -s], -1), jnp.concatenate([s, c], -1)], -2)


def _phase(k, n):
    return (k % n).astype(F32) * (2.0 * math.pi / n)


def _outer_mats(n1, n2, n_in, sign):
    n = n1 * n2
    s1 = jnp.arange(n1, dtype=jnp.int32)[:, None, None]
    f2 = jnp.arange(n2, dtype=jnp.int32)[None, :, None]
    s2 = jnp.arange(n_in, dtype=jnp.int32)[None, None, :]
    return sign * _phase(f2 * (s1 + n1 * s2), n)


def _inner_phase(n1, sign):
    a = jnp.arange(n1, dtype=jnp.int32)
    return sign * _phase(a[:, None] * a[None, :], n1)


def _split_len(n):
    n1 = 1 << (int(math.log2(n)) // 2)
    return n1, n // n1


def _fft_conv_pair(z2, hspec, n1, n2, tcol=512):
    _, length, c = z2.shape
    n = 2 * length
    assert n1 * n2 == n
    h2 = n2 // 2
    m_in = _cplx_mat(_outer_mats(n1, n2, h2, -1.0)).astype(BF16)
    a = _lmm(m_in, pl.BlockSpec((1, 2 * n2, n2), lambda s: (s, 0, 0)),
             [z2.reshape(2, h2, n1 * c)], [pl.BlockSpec((2, h2, c), lambda s: (0, 0, s))],
             (2, n1, n2, c), pl.BlockSpec((2, 1, n2, c), lambda s: (0, s, 0, 0)), (n1,), "fft_in", BF16)
    w1 = _cplx_mat(_inner_phase(n1, -1.0)).astype(BF16)
    w2 = (_cplx_mat(_inner_phase(n1, 1.0)) * (1.0 / n)).astype(BF16)
    cols = n2 * c
    tcol = min(tcol, cols)
    b = pl.pallas_call(
        _spec_mul_body,
        grid=(cols // tcol,),
        in_specs=[pl.BlockSpec((2 * n1, 2 * n1), lambda j: (0, 0)),
                  pl.BlockSpec((2 * n1, 2 * n1), lambda j: (0, 0)),
                  pl.BlockSpec((2, n1, tcol), lambda j: (0, 0, j)),
                  pl.BlockSpec((2, n1, tcol), lambda j: (0, 0, j))],
        out_specs=pl.BlockSpec((2, n1, tcol), lambda j: (0, 0, j)),
        out_shape=jax.ShapeDtypeStruct((2, n1, cols), BF16),
        compiler_params=_cparams("parallel"),
        name="fft_mid",
    )(w1, w2, a.reshape(2, n1, cols), hspec)
    m_out = _cplx_mat(jnp.swapaxes(_outer_mats(n1, n2, h2, 1.0), 1, 2)).astype(BF16)
    y = _lmm(m_out, pl.BlockSpec((1, n2, 2 * n2), lambda s: (s, 0, 0)),
             [b.reshape(2, n1, n2, c)], [pl.BlockSpec((2, 1, n2, c), lambda s: (0, s, 0, 0))],
             (2, h2, n1 * c), pl.BlockSpec((2, h2, c), lambda s: (0, 0, s)), (n1,), "fft_out")
    return y.reshape(2, length, c)


def _fft_real_spectrum(f, n1, n2, tcol=512):
    n, c = f.shape
    if n2 == 1:
        ph = _inner_phase(n1, -1.0)
        w = jnp.concatenate([jnp.cos(ph), jnp.sin(ph)], 0).astype(BF16)
        tcol = min(tcol, c)
        return _lmm(w, pl.BlockSpec((2 * n1, n1), lambda j: (0, 0)),
                    [f], [pl.BlockSpec((n1, tcol), lambda j: (0, j))],
                    (2, n1, c), pl.BlockSpec((2, n1, tcol), lambda j: (0, 0, j)), (c // tcol,), "fft_spec1")
    ph = _outer_mats(n1, n2, n2, -1.0)
    m_in = jnp.concatenate([jnp.cos(ph), jnp.sin(ph)], 1).astype(BF16)
    a = _lmm(m_in, pl.BlockSpec((1, 2 * n2, n2), lambda s: (s, 0, 0)),
             [f.reshape(n2, n1 * c)], [pl.BlockSpec((n2, c), lambda s: (0, s))],
             (2, n1, n2, c), pl.BlockSpec((2, 1, n2, c), lambda s: (0, s, 0, 0)), (n1,), "fft_spec_in", BF16)
    w1 = _cplx_mat(_inner_phase(n1, -1.0)).astype(BF16)
    cols = n2 * c
    tcol = min(tcol, cols)
    h = _lmm(w1, pl.BlockSpec((2 * n1, 2 * n1), lambda j: (0, 0)),
             [a.reshape(2, n1, cols)], [pl.BlockSpec((2, n1, tcol), lambda j: (0, 0, j))],
             (2, n1, cols), pl.BlockSpec((2, n1, tcol), lambda j: (0, 0, j)), (cols // tcol,), "fft_spec_mid")
    return h


def _hy_filter_body(wt_ref, wc_ref, ws_ref, b1_ref, w2_ref, b2_ref, w3_ref, bands_ref, dl_ref, o_ref,
                    *, length, tr):
    d = o_ref.shape[-1]
    j = pl.program_id(0) * tr + lax.broadcasted_iota(jnp.int32, (tr, 1), 0)
    k = jnp.where(j < length, j, 2 * length - j)
    t = k.astype(F32) / length
    ang = 2.0 * jnp.pi * t * bands_ref[...]
    pre = t * wt_ref[...] + _dot_hi(jnp.cos(ang), wc_ref[...]) + _dot_hi(-jnp.sin(ang), ws_ref[...])
    hdn = jnp.sin(pre + b1_ref[...])
    hdn = jnp.sin(_dot_hi(hdn, w2_ref[...]) + b2_ref[...])
    hf = _dot3(hdn, w3_ref[...])
    win = jnp.exp(-t * dl_ref[...]) + HY_SHIFT
    h = jnp.where(j < length, hf[:, :d], hf[:, d:]) * win
    o_ref[...] = jnp.where(j == length, 0.0, h)


def _hy_filter(length, f_w1, f_b1, f_w2, f_b2, f_w3):
    d = f_w3.shape[1] // 2
    ffn = f_w2.shape[0]
    tr = min(512, length)
    bands = jnp.linspace(1e-4, HY_BANDS - 1, HY_BANDS, dtype=F32).reshape(1, HY_BANDS)
    deltas = jnp.abs(jnp.linspace(HY_MIN_DECAY, HY_MAX_DECAY, d, dtype=F32)).reshape(1, d)
    ins = [f_w1[0:1], f_w1[1:1 + HY_BANDS], f_w1[1 + HY_BANDS:], f_b1.reshape(1, ffn), f_w2,
           f_b2.reshape(1, ffn), f_w3, bands, deltas]
    return pl.pallas_call(
        functools.partial(_hy_filter_body, length=length, tr=tr),
        grid=(2 * length // tr,),
        in_specs=[pl.BlockSpec(a.shape, lambda i: (0, 0)) for a in ins],
        out_specs=pl.BlockSpec((tr, d), lambda i: (i, 0)),
        out_shape=jax.ShapeDtypeStruct((2 * length, d), F32),
        compiler_params=_cparams("parallel"),
        name="hy_filter",
    )(*ins)


def _hyena(x, scale, shift, rows_per_group, length, p, j):
    m, d = x.shape
    assert m == 2 * length
    u = _mm(x, p["hy_in_w"][j].astype(BF16), bias=p["hy_in_b"][j], mod=(scale, shift),
            rows_per_group=rows_per_group)
    post = lambda x0, x1, v: (x0, v * x1)
    x0c, z = _conv3(u, p["hy_conv"][j], [0, d, 2 * d], 512, length, post, 2, name="hy_conv")
    filt = _hy_filter(length, p["hy_f_w1"][j], p["hy_f_b1"][j], p["hy_f_w2"][j], p["hy_f_b2"][j],
                      p["hy_f_w3"][j])
    n = 2 * length
    n1, n2 = (n, 1) if n <= 1024 else _split_len(n)
    hspec = _fft_real_spectrum(filt, n1, n2)
    if n2 == 1:
        z2 = jnp.pad(z.reshape(2, length, d), ((0, 0), (0, length), (0, 0)))
        y = _fft_conv_pair_single(z2, hspec, n)[:, :length]
    else:
        y = _fft_conv_pair(z.reshape(2, length, d), hspec, n1, n2)
    fn = lambda yt, zt, x0t, sk: ((yt + sk * zt) * x0t,)
    g = _rowwise(fn, [y.reshape(m, d), z, x0c], [p["hy_skip"][j].reshape(1, 1, d)], [d], name="hy_gate")[0]
    return _mm(g, p["hy_out_w"][j].astype(BF16), bias=p["hy_out_b"][j])


def _fft_conv_pair_single(z2, hspec, n, tcol=512):
    c = z2.shape[-1]
    w1 = _cplx_mat(_inner_phase(n, -1.0)).astype(BF16)
    w2 = (_cplx_mat(_inner_phase(n, 1.0)) * (1.0 / n)).astype(BF16)
    tcol = min(tcol, c)
    return pl.pallas_call(
        _spec_mul_body,
        grid=(c // tcol,),
        in_specs=[pl.BlockSpec((2 * n, 2 * n), lambda j: (0, 0)),
                  pl.BlockSpec((2 * n, 2 * n), lambda j: (0, 0)),
                  pl.BlockSpec((2, n, tcol), lambda j: (0, 0, j)),
                  pl.BlockSpec((2, n, tcol), lambda j: (0, 0, j))],
        out_specs=pl.BlockSpec((2, n, tcol), lambda j: (0, 0, j)),
        out_shape=jax.ShapeDtypeStruct((2, n, c), F32),
        compiler_params=_cparams("parallel"),
        name="fft_mid1",
    )(w1, w2, z2, hspec)


def _fnet(x, scale, shift, rows_per_group, batch, length, p, j):
    m, d = x.shape
    gc = d // FN_GROUPS
    ph = _inner_phase(gc, -1.0)
    eye = jnp.eye(FN_GROUPS, dtype=F32)
    w_c = jnp.concatenate([jnp.kron(eye, jnp.cos(ph)), jnp.kron(eye, jnp.sin(ph))], 1).astype(BF16)
    w = _mm(x, w_c, mod=(scale, shift), rows_per_group=rows_per_group)
    if length <= 1024:
        n1, n2 = length, 1
    else:
        n1, n2 = _split_len(length)
    norm = 1.0 / math.sqrt(length * gc)
    if n2 == 1:
        ph1 = _inner_phase(n1, -1.0)
        wr = (jnp.concatenate([jnp.cos(ph1), -jnp.sin(ph1)], 1) * norm).astype(BF16)
        y = _lmm(wr, pl.BlockSpec((n1, 2 * n1), lambda b, c: (0, 0)),
                 [w.reshape(batch, n1, 2 * d)] * 2,
                 [pl.BlockSpec((None, n1, d), lambda b, c: (b, 0, 0)),
                  pl.BlockSpec((None, n1, d), lambda b, c: (b, 0, 1))],
                 (batch, n1, d), pl.BlockSpec((None, n1, d), lambda b, c: (b, 0, 0)), (batch, 1), "fn_pos1")
        y = y.reshape(m, d)
    else:
        m_in = _cplx_mat(_outer_mats(n1, n2, n2, -1.0)).astype(BF16)
        wv = w.reshape(batch, n2, n1 * 2 * d)
        a = _lmm(m_in, pl.BlockSpec((1, 2 * n2, 2 * n2), lambda b, s: (s, 0, 0)),
                 [wv, wv],
                 [pl.BlockSpec((None, n2, d), lambda b, s: (b, 0, 2 * s)),
                  pl.BlockSpec((None, n2, d), lambda b, s: (b, 0, 2 * s + 1))],
                 (batch, 2, n1, n2, d), pl.BlockSpec((None, 2, 1, n2, d), lambda b, s: (b, 0, s, 0, 0)),
                 (batch, n1), "fn_pos_in", BF16)
        ph1 = _inner_phase(n1, -1.0)
        wr = (jnp.concatenate([jnp.cos(ph1), -jnp.sin(ph1)], 1) * norm).astype(BF16)
        cols = n2 * d
        tcol = 1024
        y = _lmm(wr, pl.BlockSpec((n1, 2 * n1), lambda b, c: (0, 0)),
                 [a.reshape(batch, 2, n1, cols)],
                 [pl.BlockSpec((None, 2, n1, tcol), lambda b, c: (b, 0, 0, c))],
                 (batch, n1, cols), pl.BlockSpec((None, n1, tcol), lambda b, c: (b, 0, c)),
                 (batch, cols // tcol), "fn_pos_mid")
        y = y.reshape(m, d)
    return _mm(y, p["fn_out_w"][j].astype(BF16), bias=p["fn_out_b"][j])


def _head_l2(t, extra):
    outs = []
    for h in range(t.shape[1] // GD_DK):
        th = t[:, h * GD_DK:(h + 1) * GD_DK]
        outs.append(th * (lax.rsqrt(jnp.sum(th * th, -1, keepdims=True) + 1e-6) * extra))
    return jnp.concatenate(outs, -1)


def _silu(v):
    return v * jax.nn.sigmoid(v)


def _gdn_scan_body(q_ref, k_ref, v_ref, gb_ref, gbt_ref, s0_ref, o_ref, sfin_ref, s_scr, *, n_chunks):
    direction = pl.program_id(0)
    c = pl.program_id(1)
    cs = GD_CHUNK
    nb = q_ref.shape[0]

    @pl.when(c == 0)
    def _():
        s_scr[...] = s0_ref[...]

    ri = lax.broadcasted_iota(jnp.int32, (cs, cs), 0)
    ci = lax.broadcasted_iota(jnp.int32, (cs, cs), 1)
    lag = (ri - ci) * (1 - 2 * direction)
    incl = lag >= 0
    strict = lag > 0
    tri = incl.astype(F32)
    tri_t = (lag <= 0).astype(F32)
    eye = (ri == ci).astype(F32)
    pair_masks = []
    for lvl in range(int(math.log2(cs))):
        rb, cb = lax.shift_right_logical(ri, lvl), lax.shift_right_logical(ci, lvl)
        pair_masks.append((jnp.abs(rb - cb) == 1) & ((jnp.minimum(rb, cb) & 1) == 0))
    gb = [gb_ref[b] for b in range(nb)]
    gc_cols = [_dot_hi(tri, gb[b]) for b in range(nb)]
    gc_rows = [_dot_hi(gbt_ref[b], tri_t) for b in range(nb)]
    tot = [jnp.sum(gb[b], 0, keepdims=True) for b in range(nb)]
    ch = [(b, h) for b in range(nb) for h in range(GD_HEADS)]
    hs = range(len(ch))
    nt = (((1,), (1,)), ((), ()))
    tn = (((0,), (0,)), ((), ()))
    sl = [slice(h * GD_DK, (h + 1) * GD_DK) for _, h in ch]
    k = [k_ref[b, :, sl[i]] for i, (b, h) in enumerate(ch)]
    gcol = [gc_cols[b][:, h:h + 1] for b, h in ch]
    beta = [gb[b][:, GD_HEADS + h:GD_HEADS + h + 1] for b, h in ch]
    gtot = [tot[b][:, h:h + 1] for b, h in ch]
    decay = [jnp.exp(jnp.where(incl, gcol[i] - gc_rows[b][h:h + 1, :], -jnp.inf)) for i, (b, h) in enumerate(ch)]
    eg = [jnp.exp(gcol[h]) for h in hs]
    kb = [k[h] * beta[h] for h in hs]
    kbf = [k[h].astype(BF16) for h in hs]
    kk = [lax.dot_general(kb[h].astype(BF16), kbf[h], nt, preferred_element_type=F32) for h in hs]
    a = [jnp.where(strict, kk[h] * decay[h], 0.0) for h in hs]
    inv = [eye - jnp.where(pair_masks[0], a[h], 0.0) for h in hs]
    for pm in pair_masks[1:]:
        tn_s = [_dot3(inv[h], jnp.where(pm, a[h], 0.0)) for h in hs]
        tnt = [_dot3(tn_s[h], inv[h]) for h in hs]
        inv = [inv[h] - tnt[h] for h in hs]
    rhs = [jnp.concatenate([v_ref[b, :, sl[i]] * beta[i], kb[i] * eg[i]], -1) for i, (b, h) in enumerate(ch)]
    sol = [_dot3(inv[h], rhs[h]) for h in hs]
    q = [q_ref[b, :, sl[i]] for i, (b, h) in enumerate(ch)]
    qk = [lax.dot_general(q[h].astype(BF16), kbf[h], nt, preferred_element_type=F32) for h in hs]
    sb = [s_scr[b, h].astype(BF16) for b, h in ch]
    ws = [_dot(sol[h][:, GD_DK:].astype(BF16), sb[h]) for h in hs]
    qs = [_dot((q[h] * eg[h]).astype(BF16), sb[h]) for h in hs]
    vnb = [(sol[h][:, :GD_DK] - ws[h]).astype(BF16) for h in hs]
    av = [_dot((qk[h] * decay[h]).astype(BF16), vnb[h]) for h in hs]
    kv = [lax.dot_general((k[h] * jnp.exp(gtot[h] - gcol[h])).astype(BF16), vnb[h], tn,
                          preferred_element_type=F32) for h in hs]
    for i, (b, h) in enumerate(ch):
        s_scr[b, h] = s_scr[b, h] * jnp.exp(gtot[i]) + kv[i]
    for b in range(nb):
        o_ref[b] = jnp.concatenate([qs[i] + av[i] for i, (bb, _) in enumerate(ch) if bb == b], -1)

    @pl.when(c == n_chunks - 1)
    def _():
        sfin_ref[...] = s_scr[...]


def _gdn_scan(q, k, v, gb, gbt, s0):
    b, length, w = q.shape
    n_chunks = length // GD_CHUNK
    cidx = lambda d, c: c + d * (n_chunks - 1 - 2 * c)
    seq = lambda d, c: (0, cidx(d, c), 0)
    state = pl.BlockSpec((None, b, GD_HEADS, GD_DK, GD_DK), lambda d, c: (d, 0, 0, 0, 0))
    return pl.pallas_call(
        functools.partial(_gdn_scan_body, n_chunks=n_chunks),
        grid=(2, n_chunks),
        in_specs=[pl.BlockSpec((b, GD_CHUNK, w), seq),
                  pl.BlockSpec((b, GD_CHUNK, w), seq),
                  pl.BlockSpec((b, GD_CHUNK, w), seq),
                  pl.BlockSpec((None, b, GD_CHUNK, 128), lambda d, c: (d, 0, cidx(d, c), 0)),
                  pl.BlockSpec((None, b, None, 16, GD_CHUNK), lambda d, c: (d, 0, cidx(d, c), 0, 0)),
                  state],
        out_specs=[pl.BlockSpec((None, b, GD_CHUNK, w), lambda d, c: (d, 0, cidx(d, c), 0)), state],
        out_shape=[jax.ShapeDtypeStruct((2, b, length, w), F32),
                   jax.ShapeDtypeStruct((2, b, GD_HEADS, GD_DK, GD_DK), F32)],
        scratch_shapes=[pltpu.VMEM((b, GD_HEADS, GD_DK, GD_DK), F32)],
        compiler_params=_cparams("parallel", "arbitrary"),
        name="gdn_scan",
    )(q, k, v, gb, gbt, s0)


def _gdn_inputs(x, scale, shift, rows_per_group, batch, length, p, j):
    m, d = x.shape
    wd = GD_HEADS * GD_DK
    in_w = p["gd_in_w"][j]
    u = _mm(x, in_w[:, :4 * wd].astype(BF16), mod=(scale, shift), rows_per_group=rows_per_group)
    w_ab = jnp.pad(in_w[:, 4 * wd:], ((0, 0), (0, 128 - 4 * GD_HEADS))).astype(BF16)
    ab = _mm(x, w_ab, mod=(scale, shift), rows_per_group=rows_per_group)
    qscale = GD_DK ** -0.5
    post = lambda qc, kc, vc: (_head_l2(_silu(qc), qscale), _head_l2(_silu(kc), 1.0), _silu(vc))
    q, k, v = _conv3(u, p["gd_conv"][j], [0, wd, 2 * wd], 512, length, post, 3, name="gd_conv")
    nh = GD_HEADS
    a_par = jnp.zeros((1, 1, 128), F32).at[0, 0, :2 * nh].set(-jnp.exp(p["gd_a_log"][j]).reshape(-1))
    dt_par = jnp.zeros((1, 1, 128), F32).at[0, 0, :2 * nh].set(p["gd_dt_bias"][j].reshape(-1))

    def gate_fn(abt, an, dtb):
        pre = abt + dtb
        sp = jnp.maximum(pre, 0.0) + jnp.log(1.0 + jnp.exp(-jnp.abs(pre)))
        lane = lax.broadcasted_iota(jnp.int32, abt.shape, 1)
        return (jnp.where(lane < 2 * nh, an * sp, jax.nn.sigmoid(abt)),)

    gall = _rowwise(gate_fn, [ab], [a_par, dt_par], [128], name="gd_gate")[0]
    pad = jnp.zeros((m, 128 - 2 * nh), F32)
    gb = jnp.stack([jnp.concatenate([gall[:, dr * nh:(dr + 1) * nh],
                                     gall[:, (2 + dr) * nh:(3 + dr) * nh], pad], -1) for dr in range(2)])
    gb = gb.reshape(2, batch, length, 128)
    gbt = jnp.swapaxes(gb[..., :2 * nh].reshape(2, batch, length // GD_CHUNK, GD_CHUNK, 2 * nh), -1, -2)
    rs = lambda t: t.reshape(batch, length, wd)
    return rs(q), rs(k), rs(v), gb, gbt, u


def _gdn_out(o2, u, norm_g, out_w):
    m, wd = o2.shape[1], o2.shape[2]
    ng = jnp.tile(norm_g, wd // norm_g.shape[0]).reshape(1, 1, wd)

    def body(of_ref, ob_ref, z_ref, ng_ref, o_ref):
        o = of_ref[...] + ob_ref[...]
        outs = []
        for h in range(wd // GD_DK):
            oh = o[:, h * GD_DK:(h + 1) * GD_DK]
            outs.append(oh * lax.rsqrt(jnp.mean(oh * oh, -1, keepdims=True) + 1e-6))
        o_ref[...] = jnp.concatenate(outs, -1) * ng_ref[...] * _silu(z_ref[...])

    tm = math.gcd(512, m)
    g = pl.pallas_call(
        body,
        grid=(m // tm,),
        in_specs=[pl.BlockSpec((None, tm, wd), lambda i: (0, i, 0)),
                  pl.BlockSpec((None, tm, wd), lambda i: (1, i, 0)),
                  pl.BlockSpec((tm, wd), lambda i: (i, 3)),
                  pl.BlockSpec((None, 1, wd), lambda i: (0, 0, 0))],
        out_specs=pl.BlockSpec((tm, wd), lambda i: (i, 0)),
        out_shape=jax.ShapeDtypeStruct((m, wd), F32),
        compiler_params=_cparams("parallel"),
        name="gd_norm",
    )(o2, o2, u, ng)
    return _mm(g, out_w.astype(BF16))


def _topk_rows(s, n_take, val_ref, idx_ref, base, rid=None):
    if rid is None:
        rid = lax.broadcasted_iota(jnp.int32, s.shape, 0)
    for t in range(n_take):
        mx = jnp.max(s, 0, keepdims=True)
        am = jnp.min(jnp.where(s == mx, rid, jnp.iinfo(jnp.int32).max), 0, keepdims=True)
        val_ref[base + t:base + t + 1, :] = mx
        idx_ref[base + t:base + t + 1, :] = am
        s = jnp.where(rid == am, -jnp.inf, s)


def _pk_topk_body(x_ref, sc_ref, sh_ref, wq_ref, keys_ref, i1_ref, i2_ref, gate_ref, sv_scr, si_scr,
                  cv_scr, ci_scr, i1p_scr, i2p_scr, gp_scr):
    kk = PK_TOPK
    tt = x_ref.shape[0]
    hb = (x_ref[...] * (1.0 + sc_ref[...]) + sh_ref[...]).astype(BF16)
    for h in range(PK_HEADS):
        for p in range(2):
            qh = _dot(hb, wq_ref[:, (2 * h + p) * PK_DH:(2 * h + p + 1) * PK_DH]).astype(BF16)
            st = lax.dot_general(keys_ref[h, p], qh, (((1,), (1,)), ((), ())),
                                 preferred_element_type=F32)
            _topk_rows(st, kk, sv_scr, si_scr, p * kk)
        sv1, sv2 = sv_scr[0:kk, :], sv_scr[kk:2 * kk, :]
        si1, si2 = si_scr[0:kk, :], si_scr[kk:2 * kk, :]
        r8 = lax.broadcasted_iota(jnp.int32, (8, tt), 0)
        cand = jnp.concatenate([sv1[0:8, :] + sv2[r2:r2 + 1, :] for r2 in range(8)]
                               + [sv1[8:kk, :] + sv2[0:1, :], sv1[0:1, :] + sv2[8:kk, :]], 0)
        cid = jnp.concatenate([r8 * kk + r2 for r2 in range(8)] + [(r8 + 8) * kk, r8 + 8], 0)
        _topk_rows(cand, kk, cv_scr, ci_scr, 0, cid)
        cv, ci = cv_scr[...], ci_scr[...]
        a1, a2 = lax.shift_right_logical(ci, int(math.log2(kk))), ci & (kk - 1)
        i1 = jnp.zeros((kk, tt), jnp.int32)
        i2 = jnp.zeros((kk, tt), jnp.int32)
        for r in range(kk):
            i1 = i1 + jnp.where(a1 == r, si1[r:r + 1, :], 0)
            i2 = i2 + jnp.where(a2 == r, si2[r:r + 1, :], 0)
        i1p_scr[h * kk:(h + 1) * kk, :] = i1
        i2p_scr[h * kk:(h + 1) * kk, :] = i2
        e = jnp.exp(cv - jnp.max(cv, 0, keepdims=True))
        gp_scr[h * kk:(h + 1) * kk, :] = e / jnp.sum(e, 0, keepdims=True)
    i1_ref[...] = i1p_scr[...].T
    i2_ref[...] = i2p_scr[...].T
    gate_ref[...] = gp_scr[...].T


def _pk_topk(x, scale, shift, rows_per_group, wq_bf16, keys_bf16):
    m, d = x.shape
    tt = PK_TOK
    hk = PK_HEADS * PK_TOPK
    assert hk == tt and rows_per_group % tt == 0
    g = rows_per_group // tt
    vec = pl.BlockSpec((None, 1, d), lambda i: (i // g, 0, 0))
    return pl.pallas_call(
        _pk_topk_body,
        grid=(m // tt,),
        in_specs=[pl.BlockSpec((tt, d), lambda i: (i, 0)), vec, vec,
                  pl.BlockSpec(wq_bf16.shape, lambda i: (0, 0)),
                  pl.BlockSpec(keys_bf16.shape, lambda i: (0, 0, 0, 0))],
        out_specs=[pl.BlockSpec((tt, hk), lambda i: (i, 0))] * 3,
        out_shape=[jax.ShapeDtypeStruct((m, hk), jnp.int32), jax.ShapeDtypeStruct((m, hk), jnp.int32),
                   jax.ShapeDtypeStruct((m, hk), F32)],
        scratch_shapes=[pltpu.VMEM((2 * PK_TOPK, tt), F32), pltpu.VMEM((2 * PK_TOPK, tt), jnp.int32),
                        pltpu.VMEM((PK_TOPK, tt), F32), pltpu.VMEM((PK_TOPK, tt), jnp.int32),
                        pltpu.VMEM((hk, tt), jnp.int32), pltpu.VMEM((hk, tt), jnp.int32),
                        pltpu.VMEM((hk, tt), F32)],
        compiler_params=_cparams("parallel"),
        name="pk_topk",
    )(x, scale, shift, wq_bf16, keys_bf16)


def _pk_dense_body(x_ref, sc_ref, sh_ref, gt_ref, lg_ref, lb_ref, i1_ref, i2_ref, gate_ref, ut_ref, v_ref,
                   o_ref, hb_scr, act_scr, w_scr, grid_scr, acc_scr, *, alpha, n_chunks):
    e = pl.program_id(1)
    tm = x_ref.shape[0]
    nk = PK_NKEYS
    per = ut_ref.shape[1] // nk

    @pl.when(e == 0)
    def _():
        hb_scr[...] = (x_ref[...] * (1.0 + sc_ref[...]) + sh_ref[...]).astype(BF16)
        act_scr[...] = jnp.zeros_like(act_scr)

    @pl.when(e < n_chunks)
    def _():
        hb = hb_scr[...]
        i1 = i1_ref[...]
        i2 = i2_ref[...]
        act = act_scr[...]
        for kp in range(per // 2):
            s = _dot(hb, ut_ref[:, 2 * kp * nk:2 * (kp + 1) * nk])
            for k in (2 * kp, 2 * kp + 1):
                got = jnp.take_along_axis(s[:, (k % 2) * nk:(k % 2 + 1) * nk], i2, axis=1)
                act = jnp.where(i1 == e * per + k, got, act)
        act_scr[...] = act

    @pl.when(e == n_chunks)
    def _():
        w_scr[...] = jax.nn.gelu(act_scr[...]) * gate_ref[...]
        acc_scr[...] = jnp.zeros_like(acc_scr)
        sub = lax.broadcasted_iota(jnp.int32, (nk, nk), 0)
        key1 = jnp.where(sub < nk // 2, 2 * sub, 2 * sub - (nk - 1))

        def tok(tg, carry):
            for t0 in range(0, PK_UNROLL, 8):
                words = []
                for u in range(8):
                    t = tg * PK_UNROLL + t0 + u
                    wrow = w_scr[pl.ds(t, 1), :]
                    at = jnp.where(key1 == i1_ref[pl.ds(t, 1), :], wrow, 0.0).astype(BF16)
                    bt = (sub == i2_ref[pl.ds(t, 1), :]).astype(BF16)
                    g = lax.dot_general(at, bt, (((1,), (1,)), ((), ())), preferred_element_type=F32)
                    g = lax.bitcast_convert_type(g.astype(BF16).astype(F32), jnp.int32)
                    words.append(lax.shift_right_logical(g[:nk // 2], 16) | (g[nk // 2:] & jnp.int32(-65536)))
                first = pl.multiple_of(tg * PK_UNROLL + t0, 8)
                grid_scr[:, pl.ds(first, 8), :] = jnp.swapaxes(jnp.stack(words, 0), 0, 1)
            return carry

        lax.fori_loop(0, tm // PK_UNROLL, tok, 0)

    @pl.when(e >= n_chunks)
    def _():
        c = e - n_chunks
        parts = []
        for kp in range(per // 2):
            wd = grid_scr[c * (per // 2) + kp]
            parts.append(lax.bitcast_convert_type(lax.shift_left(wd, 16), F32))
            parts.append(lax.bitcast_convert_type(wd & jnp.int32(-65536), F32))
        acc_scr[...] += _dot(jnp.concatenate(parts, -1).astype(BF16), v_ref[...])

    @pl.when(e == 2 * n_chunks - 1)
    def _():
        o_ref[...] = _ln(alpha * x_ref[...] + gt_ref[...] * acc_scr[...], lg_ref[...], lb_ref[...])


def _pk_dense(x, scale, shift, gate_vec, ln_g, ln_b, i1, i2, gate, ut_bf16, v_bf16, rows_per_group, alpha):
    m, d = x.shape
    n_exp = v_bf16.shape[0]
    tm = math.gcd(PK_TM, m, rows_per_group)
    n_chunks = n_exp // PK_EC
    g = rows_per_group // tm
    assert n_exp % PK_EC == 0 and PK_EC % PK_NKEYS == 0
    row = lambda i, e: (i, 0)
    vec = lambda i, e: (i // g, 0, 0)
    one = lambda i, e: (0, 0, 0)
    npk = i1.shape[1]
    return pl.pallas_call(
        functools.partial(_pk_dense_body, alpha=alpha, n_chunks=n_chunks),
        grid=(m // tm, 2 * n_chunks),
        in_specs=[pl.BlockSpec((tm, d), row),
                  pl.BlockSpec((None, 1, d), vec), pl.BlockSpec((None, 1, d), vec), pl.BlockSpec((None, 1, d), vec),
                  pl.BlockSpec((None, 1, d), one), pl.BlockSpec((None, 1, d), one),
                  pl.BlockSpec((tm, npk), row), pl.BlockSpec((tm, npk), row), pl.BlockSpec((tm, npk), row),
                  pl.BlockSpec((d, PK_EC), lambda i, e: (0, jnp.minimum(e, n_chunks - 1))),
                  pl.BlockSpec((PK_EC, d), lambda i, e: (jnp.maximum(e - n_chunks, 0), 0))],
        out_specs=pl.BlockSpec((tm, d), row),
        out_shape=jax.ShapeDtypeStruct((m, d), F32),
        scratch_shapes=[pltpu.VMEM((tm, d), BF16), pltpu.VMEM((tm, npk), F32), pltpu.VMEM((tm, npk), F32),
                        pltpu.VMEM((PK_NKEYS // 2, tm, PK_NKEYS), jnp.int32), pltpu.VMEM((tm, d), F32)],
        compiler_params=_cparams("parallel", "arbitrary"),
        name="pk_dense",
    )(x, scale, shift, gate_vec, ln_g, ln_b, i1, i2, gate, ut_bf16, v_bf16)


def _peer_ln(x, scale, shift, gate_vec, ln_g, ln_b, rows_per_group, alpha, wq_bf16, keys_bf16, ut_bf16, v_bf16):
    i1, i2, gate = _pk_topk(x, scale, shift, rows_per_group, wq_bf16, keys_bf16)
    return _pk_dense(x, scale, shift, gate_vec, ln_g, ln_b, i1, i2, gate, ut_bf16, v_bf16, rows_per_group, alpha)


def kernel(x, c, ctx, c_ctx, ada_w, ada_b, ln_g, ln_b, pk_wq, pk_keys, pk_u, pk_v, hy_in_w, hy_in_b, hy_conv, hy_f_w1, hy_f_b1, hy_f_w2, hy_f_b2, hy_f_w3, hy_skip, hy_out_w, hy_out_b, gd_in_w, gd_conv, gd_a_log, gd_dt_bias, gd_norm_g, gd_out_w, fn_out_w, fn_out_b):
    p = dict(hy_in_w=hy_in_w, hy_in_b=hy_in_b, hy_conv=hy_conv, hy_f_w1=hy_f_w1, hy_f_b1=hy_f_b1,
             hy_f_w2=hy_f_w2, hy_f_b2=hy_f_b2, hy_f_w3=hy_f_w3, hy_skip=hy_skip, hy_out_w=hy_out_w,
             hy_out_b=hy_out_b, gd_in_w=gd_in_w, gd_conv=gd_conv, gd_a_log=gd_a_log, gd_dt_bias=gd_dt_bias,
             fn_out_w=fn_out_w, fn_out_b=fn_out_b)
    b, length, d = x.shape
    lc = ctx.shape[1]
    depth = ada_w.shape[0]
    alpha = (2 * depth) ** 0.25
    xl = _pos_add(x)
    xc = ctx.reshape(b * lc, d)
    gdn_layers = [i for i in range(depth) if i % N_MIXERS == 1]
    ctx_until = gdn_layers[-1] if gdn_layers else -1
    cond = jnp.concatenate([c, c_ctx[None], jnp.zeros((8 - b - 1, d), F32)], 0)
    for i in range(depth):
        kind, j = i % N_MIXERS, i // N_MIXERS
        ctx_in, ctx_out = i <= ctx_until, i < ctx_until
        mod = _mm(cond, ada_w[i].astype(BF16), bias=ada_b[i], silu_in=True, tn=2048)
        mod = mod.reshape(8, N_MOD, 1, d)
        ml = [mod[:b, t] for t in range(N_MOD)]
        mc = [mod[b:b + 1, t] for t in range(N_MOD)]
        lg = [ln_g[i, t].reshape(1, 1, d) for t in range(2)]
        lb = [ln_b[i, t].reshape(1, 1, d) for t in range(2)]
        yc = None
        if kind == 0:
            yl = _hyena(xl, ml[1], ml[0], length, length, p, j)
            if ctx_out:
                yc = _hyena(xc, mc[1], mc[0], b * lc, lc, p, j)
        elif kind == 1:
            qc, kc, vc, gbc, gbtc, uc = _gdn_inputs(xc, mc[1], mc[0], b * lc, b, lc, p, j)
            ql, kl, vl, gbl, gbtl, ul = _gdn_inputs(xl, ml[1], ml[0], length, b, length, p, j)
            s0 = jnp.zeros((2, b, GD_HEADS, GD_DK, GD_DK), F32)
            oc, s_ctx = _gdn_scan(qc, kc, vc, gbc, gbtc, s0)
            ol, _ = _gdn_scan(ql, kl, vl, gbl, gbtl, s_ctx)
            yl = _gdn_out(ol.reshape(2, b * length, -1), ul, gd_norm_g[j], gd_out_w[j])
            if ctx_out:
                yc = _gdn_out(oc.reshape(2, b * lc, -1), uc, gd_norm_g[j], gd_out_w[j])
        else:
            yl = _fnet(xl, ml[1], ml[0], length, b, length, p, j)
            if ctx_out:
                yc = _fnet(xc, mc[1], mc[0], b * lc, b, lc, p, j)
        wq = pk_wq[i].astype(BF16)
        keys = pk_keys[i].astype(BF16)
        ut, vt = pk_u[i].astype(BF16).T, pk_v[i].astype(BF16)
        xl = _res_ln(xl, yl, ml[2], lg[0], lb[0], alpha, length)
        xl = _peer_ln(xl, ml[4], ml[3], ml[5], lg[1], lb[1], length, alpha, wq, keys, ut, vt)
        if ctx_out:
            xc = _res_ln(xc, yc, mc[2], lg[0], lb[0], alpha, b * lc)
            xc = _peer_ln(xc, mc[4], mc[3], mc[5], lg[1], lb[1], b * lc, alpha, wq, keys, ut, vt)
    return xl.reshape(b, length, d)
```

```python
import functools
import math

import numpy as np
import jax
import jax.numpy as jnp
from jax import lax
from jax.experimental import pallas as pl
from jax.experimental.pallas import tpu as pltpu

F32 = jnp.float32
BF16 = jnp.bfloat16

GRID_W = 64
N_MIXERS = 3
N_MOD = 6
LN_EPS = 1e-5
HY_EMB = 33
HY_BANDS = (HY_EMB - 1) // 2
HY_SHIFT = 0.05
HY_TARGET = 1e-2
HY_MIN_DECAY = math.log(HY_TARGET) / 1.5
HY_MAX_DECAY = math.log(HY_TARGET) / 0.3
GD_HEADS = 8
GD_DK = 128
GD_CHUNK = 64
FN_GROUPS = 4
PK_HEADS = 8
PK_NKEYS = 128
PK_DH = 128
PK_TOPK = 16
PK_TOK = 128
PK_TM = 512
PK_EC = 2048
PK_UNROLL = 32

VMEM_LIMIT_BYTES = 56 * 1024 * 1024


def _cparams(*sem):
    return pltpu.CompilerParams(dimension_semantics=sem, vmem_limit_bytes=VMEM_LIMIT_BYTES)


def _dot(a, b):
    return jnp.dot(a, b, preferred_element_type=F32)


def _dot_hi(a, b):
    return jnp.dot(a, b, preferred_element_type=F32, precision=lax.Precision.HIGHEST)


def _split(a):
    hi = a.astype(BF16)
    lo = (a - hi.astype(F32)).astype(BF16)
    return hi, lo


def _dot3(a, b):
    ah, al = _split(a)
    bh, bl = _split(b)
    return _dot(ah, bh) + (_dot(ah, bl) + _dot(al, bh))


def _mm_body(*refs, has_mod, has_bias, silu_in):
    a_ref, w_ref = refs[0], refs[1]
    k = 2
    a = a_ref[...]
    if has_mod:
        a = a * (1.0 + refs[k][...]) + refs[k + 1][...]
        k += 2
    if silu_in:
        a = a * jax.nn.sigmoid(a)
    o = _dot(a.astype(BF16), w_ref[...])
    if has_bias:
        o = o + refs[k][...]
        k += 1
    refs[k][...] = o


def _mm(a, w_bf16, bias=None, mod=None, rows_per_group=None, silu_in=False, tm=1024, tn=None):
    m, k = a.shape
    n = w_bf16.shape[1]
    tm = math.gcd(tm, m, rows_per_group or m)
    if tn is None:
        tn = max(t for t in range(128, min(n, 2048) + 1, 128) if n % t == 0)
    assert m % tm == 0 and n % tn == 0
    ins = [a, w_bf16]
    specs = [pl.BlockSpec((tm, k), lambda j, i: (i, 0)), pl.BlockSpec((k, tn), lambda j, i: (0, j))]
    if mod is not None:
        assert rows_per_group % tm == 0
        g = rows_per_group // tm
        for v in mod:
            ins.append(v)
            specs.append(pl.BlockSpec((None, 1, k), lambda j, i: (i // g, 0, 0)))
    if bias is not None:
        ins.append(bias.reshape(1, n))
        specs.append(pl.BlockSpec((1, tn), lambda j, i: (0, j)))
    return pl.pallas_call(
        functools.partial(_mm_body, has_mod=mod is not None, has_bias=bias is not None, silu_in=silu_in),
        grid=(n // tn, m // tm),
        in_specs=specs,
        out_specs=pl.BlockSpec((tm, tn), lambda j, i: (i, j)),
        out_shape=jax.ShapeDtypeStruct((m, n), F32),
        compiler_params=_cparams("parallel", "parallel"),
        name="mm",
    )(*ins)


def _rowwise(fn, rows, vecs, out_cols, rows_per_group=None, tm=512, name="rowwise"):
    m = rows[0].shape[0]
    tm = math.gcd(tm, m, rows_per_group or m)
    assert m % tm == 0
    n_r, n_v, n_o = len(rows), len(vecs), len(out_cols)

    def body(*refs):
        outs = fn(*[r[...] for r in refs[:n_r + n_v]])
        for o_ref, o in zip(refs[n_r + n_v:], outs):
            o_ref[...] = o

    specs = [pl.BlockSpec((tm, r.shape[1]), lambda i: (i, 0)) for r in rows]
    for v in vecs:
        if v.shape[0] == 1:
            specs.append(pl.BlockSpec((None, 1, v.shape[2]), lambda i: (0, 0, 0)))
        else:
            assert rows_per_group % tm == 0
            g = rows_per_group // tm
            specs.append(pl.BlockSpec((None, 1, v.shape[2]), lambda i, g=g: (i // g, 0, 0)))
    return pl.pallas_call(
        body,
        grid=(m // tm,),
        in_specs=specs,
        out_specs=[pl.BlockSpec((tm, c), lambda i: (i, 0)) for c in out_cols],
        out_shape=[jax.ShapeDtypeStruct((m, c), F32) for c in out_cols],
        compiler_params=_cparams("parallel"),
        name=name,
    )(*rows, *vecs)


def _ln(v, g, b):
    mu = jnp.mean(v, -1, keepdims=True)
    d = v - mu
    var = jnp.mean(d * d, -1, keepdims=True)
    return d * lax.rsqrt(var + LN_EPS) * g + b


def _res_ln(x, y, gate, ln_g, ln_b, alpha, rows_per_group):
    fn = lambda xt, yt, gt, lg, lb: (_ln(alpha * xt + gt * yt, lg, lb),)
    return _rowwise(fn, [x, y], [gate, ln_g, ln_b], [x.shape[1]], rows_per_group, name="res_ln")[0]


def _pos_add_body(x_ref, er_ref, ec_ref, o_ref):
    half = er_ref.shape[-1]
    x = x_ref[...]
    er = jnp.broadcast_to(er_ref[...], x.shape[:2] + (half,))
    ec = jnp.broadcast_to(ec_ref[...][None], x.shape[:2] + (half,))
    o_ref[...] = x + jnp.concatenate([er, ec], -1)


def _pos_add(x):
    b, length, d = x.shape
    rows = length // GRID_W
    quarter = d // 4
    omega = 1.0 / (10000.0 ** (jnp.arange(quarter, dtype=F32) / quarter))
    er = jnp.arange(rows, dtype=F32)[:, None] * omega
    ec = jnp.arange(GRID_W, dtype=F32)[:, None] * omega
    emb_r = jnp.concatenate([jnp.sin(er), jnp.cos(er)], -1).reshape(rows, 1, d // 2)
    emb_c = jnp.concatenate([jnp.sin(ec), jnp.cos(ec)], -1)
    rt = 8
    x4 = x.reshape(b, rows, GRID_W, d)
    out = pl.pallas_call(
        _pos_add_body,
        grid=(b, rows // rt),
        in_specs=[pl.BlockSpec((None, rt, GRID_W, d), lambda i, j: (i, j, 0, 0)),
                  pl.BlockSpec((rt, 1, d // 2), lambda i, j: (j, 0, 0)),
                  pl.BlockSpec((GRID_W, d // 2), lambda i, j: (0, 0))],
        out_specs=pl.BlockSpec((None, rt, GRID_W, d), lambda i, j: (i, j, 0, 0)),
        out_shape=jax.ShapeDtypeStruct(x4.shape, F32),
        compiler_params=_cparams("parallel", "parallel"),
        name="pos_add",
    )(x4, emb_r, emb_c)
    return out.reshape(b * length, d)


def _conv3_body(*refs, n_parts, tm, seq_len, post, n_vec):
    i = pl.program_id(0)
    first = (i * tm) % seq_len == 0
    last = ((i + 1) * tm) % seq_len == 0
    row = lax.broadcasted_iota(jnp.int32, (tm, 1), 0)
    parts = []
    for p in range(n_parts):
        main_ref, prev_ref, next_ref, w_ref = refs[4 * p:4 * p + 4]
        u = main_ref[...]
        w = w_ref[...]
        prev = jnp.where(first, 0.0, prev_ref[7:8, :])
        nxt = jnp.where(last, 0.0, next_ref[0:1, :])
        up = jnp.where(row == 0, prev, pltpu.roll(u, 1, 0))
        dn = jnp.where(row == tm - 1, nxt, pltpu.roll(u, tm - 1, 0))
        parts.append(up * w[0:1, :] + u * w[1:2, :] + dn * w[2:3, :])
    k = 4 * n_parts
    vecs = [refs[k + j][...] for j in range(n_vec)]
    outs = post(*parts, *vecs)
    for o_ref, o in zip(refs[k + n_vec:], outs):
        o_ref[...] = o


def _conv3(u, w, col_parts, tc, seq_len, post, n_out, vecs=(), tm=256, name="conv3"):
    m = u.shape[0]
    tm = min(tm, seq_len)
    assert seq_len % tm == 0 and m % tm == 0 and tm % 8 == 0
    width = col_parts[1] - col_parts[0] if len(col_parts) > 1 else tc
    ncol = width // tc
    t8 = tm // 8
    nb8 = m // 8
    ins, specs = [], []
    for c0 in col_parts:
        cb = c0 // tc
        ins += [u, u, u, w]
        specs += [
            pl.BlockSpec((tm, tc), lambda i, j, cb=cb: (i, cb + j)),
            pl.BlockSpec((8, tc), lambda i, j, cb=cb: (jnp.maximum(i * t8 - 1, 0), cb + j)),
            pl.BlockSpec((8, tc), lambda i, j, cb=cb: (jnp.minimum((i + 1) * t8, nb8 - 1), cb + j)),
            pl.BlockSpec((3, tc), lambda i, j, cb=cb: (0, cb + j)),
        ]
    for v in vecs:
        ins.append(v)
        specs.append(pl.BlockSpec((1, tc), lambda i, j: (0, j)))
    return pl.pallas_call(
        functools.partial(_conv3_body, n_parts=len(col_parts), tm=tm, seq_len=seq_len, post=post,
                          n_vec=len(vecs)),
        grid=(m // tm, ncol),
        in_specs=specs,
        out_specs=[pl.BlockSpec((tm, tc), lambda i, j: (i, j)) for _ in range(n_out)],
        out_shape=[jax.ShapeDtypeStruct((m, width), F32) for _ in range(n_out)],
        compiler_params=_cparams("parallel", "parallel"),
        name=name,
    )(*ins)


def _lmm_body(*refs, n_x):
    w_ref, o_ref = refs[0], refs[1 + n_x]
    tc = o_ref.shape[-1]
    xs = [r[...].reshape(-1, tc) for r in refs[1:1 + n_x]]
    x = xs[0] if n_x == 1 else jnp.concatenate(xs, 0)
    w = w_ref[...]
    w = w.reshape(w.shape[-2], w.shape[-1])
    o_ref[...] = _dot(w, x.astype(BF16)).reshape(o_ref.shape).astype(o_ref.dtype)


def _lmm(w, w_spec, xs, x_specs, out_shape, out_spec, grid, name, out_dtype=F32):
    return pl.pallas_call(
        functools.partial(_lmm_body, n_x=len(xs)),
        grid=grid,
        in_specs=[w_spec] + list(x_specs),
        out_specs=out_spec,
        out_shape=jax.ShapeDtypeStruct(out_shape, out_dtype),
        compiler_params=_cparams(*(["parallel"] * len(grid))),
        name=name,
    )(w, *xs)


def _spec_mul_body(w1_ref, w2_ref, x_ref, h_ref, o_ref):
    tc = o_ref.shape[-1]
    x = x_ref[...].reshape(-1, tc)
    z = _dot(w1_ref[...], x.astype(BF16))
    half = z.shape[0] // 2
    zr, zi = z[:half], z[half:]
    hr, hi = h_ref[0], h_ref[1]
    y = jnp.concatenate([zr * hr - zi * hi, zr * hi + zi * hr], 0)
    o_ref[...] = _dot(w2_ref[...], y.astype(BF16)).reshape(o_ref.shape).astype(o_ref.dtype)


def _cplx_mat(ang):
    c, s = jnp.cos(ang), jnp.sin(ang)
    return jnp.concatenate([jnp.concatenate([c, -s], -1), jnp.concatenate([s, c], -1)], -2)


def _phase(k, n):
    return (k % n).astype(F32) * (2.0 * math.pi / n)


def _outer_mats(n1, n2, n_in, sign):
    n = n1 * n2
    s1 = jnp.arange(n1, dtype=jnp.int32)[:, None, None]
    f2 = jnp.arange(n2, dtype=jnp.int32)[None, :, None]
    s2 = jnp.arange(n_in, dtype=jnp.int32)[None, None, :]
    return sign * _phase(f2 * (s1 + n1 * s2), n)


def _inner_phase(n1, sign):
    a = jnp.arange(n1, dtype=jnp.int32)
    return sign * _phase(a[:, None] * a[None, :], n1)


def _split_len(n):
    n1 = 1 << (int(math.log2(n)) // 2)
    return n1, n // n1


def _fft_conv_pair(z2, hspec, n1, n2, tcol=512):
    _, length, c = z2.shape
    n = 2 * length
    assert n1 * n2 == n
    h2 = n2 // 2
    m_in = _cplx_mat(_outer_mats(n1, n2, h2, -1.0)).astype(BF16)
    a = _lmm(m_in, pl.BlockSpec((1, 2 * n2, n2), lambda s: (s, 0, 0)),
             [z2.reshape(2, h2, n1 * c)], [pl.BlockSpec((2, h2, c), lambda s: (0, 0, s))],
             (2, n1, n2, c), pl.BlockSpec((2, 1, n2, c), lambda s: (0, s, 0, 0)), (n1,), "fft_in", BF16)
    w1 = _cplx_mat(_inner_phase(n1, -1.0)).astype(BF16)
    w2 = (_cplx_mat(_inner_phase(n1, 1.0)) * (1.0 / n)).astype(BF16)
    cols = n2 * c
    tcol = min(tcol, cols)
    b = pl.pallas_call(
        _spec_mul_body,
        grid=(cols // tcol,),
        in_specs=[pl.BlockSpec((2 * n1, 2 * n1), lambda j: (0, 0)),
                  pl.BlockSpec((2 * n1, 2 * n1), lambda j: (0, 0)),
                  pl.BlockSpec((2, n1, tcol), lambda j: (0, 0, j)),
                  pl.BlockSpec((2, n1, tcol), lambda j: (0, 0, j))],
        out_specs=pl.BlockSpec((2, n1, tcol), lambda j: (0, 0, j)),
        out_shape=jax.ShapeDtypeStruct((2, n1, cols), BF16),
        compiler_params=_cparams("parallel"),
        name="fft_mid",
    )(w1, w2, a.reshape(2, n1, cols), hspec)
    m_out = _cplx_mat(jnp.swapaxes(_outer_mats(n1, n2, h2, 1.0), 1, 2)).astype(BF16)
    y = _lmm(m_out, pl.BlockSpec((1, n2, 2 * n2), lambda s: (s, 0, 0)),
             [b.reshape(2, n1, n2, c)], [pl.BlockSpec((2, 1, n2, c), lambda s: (0, s, 0, 0))],
             (2, h2, n1 * c), pl.BlockSpec((2, h2, c), lambda s: (0, 0, s)), (n1,), "fft_out")
    return y.reshape(2, length, c)


def _fft_real_spectrum(f, n1, n2, tcol=512):
    n, c = f.shape
    if n2 == 1:
        ph = _inner_phase(n1, -1.0)
        w = jnp.concatenate([jnp.cos(ph), jnp.sin(ph)], 0).astype(BF16)
        tcol = min(tcol, c)
        return _lmm(w, pl.BlockSpec((2 * n1, n1), lambda j: (0, 0)),
                    [f], [pl.BlockSpec((n1, tcol), lambda j: (0, j))],
                    (2, n1, c), pl.BlockSpec((2, n1, tcol), lambda j: (0, 0, j)), (c // tcol,), "fft_spec1")
    ph = _outer_mats(n1, n2, n2, -1.0)
    m_in = jnp.concatenate([jnp.cos(ph), jnp.sin(ph)], 1).astype(BF16)
    a = _lmm(m_in, pl.BlockSpec((1, 2 * n2, n2), lambda s: (s, 0, 0)),
             [f.reshape(n2, n1 * c)], [pl.BlockSpec((n2, c), lambda s: (0, s))],
             (2, n1, n2, c), pl.BlockSpec((2, 1, n2, c), lambda s: (0, s, 0, 0)), (n1,), "fft_spec_in", BF16)
    w1 = _cplx_mat(_inner_phase(n1, -1.0)).astype(BF16)
    cols = n2 * c
    tcol = min(tcol, cols)
    h = _lmm(w1, pl.BlockSpec((2 * n1, 2 * n1), lambda j: (0, 0)),
             [a.reshape(2, n1, cols)], [pl.BlockSpec((2, n1, tcol), lambda j: (0, 0, j))],
             (2, n1, cols), pl.BlockSpec((2, n1, tcol), lambda j: (0, 0, j)), (cols // tcol,), "fft_spec_mid")
    return h


def _hy_filter_body(wt_ref, wc_ref, ws_ref, b1_ref, w2_ref, b2_ref, w3_ref, bands_ref, dl_ref, o_ref,
                    *, length, tr):
    d = o_ref.shape[-1]
    j = pl.program_id(0) * tr + lax.broadcasted_iota(jnp.int32, (tr, 1), 0)
    k = jnp.where(j < length, j, 2 * length - j)
    t = k.astype(F32) / length
    ang = 2.0 * jnp.pi * t * bands_ref[...]
    pre = t * wt_ref[...] + _dot_hi(jnp.cos(ang), wc_ref[...]) + _dot_hi(-jnp.sin(ang), ws_ref[...])
    hdn = jnp.sin(pre + b1_ref[...])
    hdn = jnp.sin(_dot_hi(hdn, w2_ref[...]) + b2_ref[...])
    hf = _dot3(hdn, w3_ref[...])
    win = jnp.exp(-t * dl_ref[...]) + HY_SHIFT
    h = jnp.where(j < length, hf[:, :d], hf[:, d:]) * win
    o_ref[...] = jnp.where(j == length, 0.0, h)


def _hy_filter(length, f_w1, f_b1, f_w2, f_b2, f_w3):
    d = f_w3.shape[1] // 2
    ffn = f_w2.shape[0]
    tr = min(512, length)
    bands = jnp.linspace(1e-4, HY_BANDS - 1, HY_BANDS, dtype=F32).reshape(1, HY_BANDS)
    deltas = jnp.abs(jnp.linspace(HY_MIN_DECAY, HY_MAX_DECAY, d, dtype=F32)).reshape(1, d)
    ins = [f_w1[0:1], f_w1[1:1 + HY_BANDS], f_w1[1 + HY_BANDS:], f_b1.reshape(1, ffn), f_w2,
           f_b2.reshape(1, ffn), f_w3, bands, deltas]
    return pl.pallas_call(
        functools.partial(_hy_filter_body, length=length, tr=tr),
        grid=(2 * length // tr,),
        in_specs=[pl.BlockSpec(a.shape, lambda i: (0, 0)) for a in ins],
        out_specs=pl.BlockSpec((tr, d), lambda i: (i, 0)),
        out_shape=jax.ShapeDtypeStruct((2 * length, d), F32),
        compiler_params=_cparams("parallel"),
        name="hy_filter",
    )(*ins)


def _hyena(x, scale, shift, rows_per_group, length, p, j):
    m, d = x.shape
    assert m == 2 * length
    u = _mm(x, p["hy_in_w"][j].astype(BF16), bias=p["hy_in_b"][j], mod=(scale, shift),
            rows_per_group=rows_per_group)
    post = lambda x0, x1, v: (x0, v * x1)
    x0c, z = _conv3(u, p["hy_conv"][j], [0, d, 2 * d], 512, length, post, 2, name="hy_conv")
    filt = _hy_filter(length, p["hy_f_w1"][j], p["hy_f_b1"][j], p["hy_f_w2"][j], p["hy_f_b2"][j],
                      p["hy_f_w3"][j])
    n = 2 * length
    n1, n2 = (n, 1) if n <= 1024 else _split_len(n)
    hspec = _fft_real_spectrum(filt, n1, n2)
    if n2 == 1:
        z2 = jnp.pad(z.reshape(2, length, d), ((0, 0), (0, length), (0, 0)))
        y = _fft_conv_pair_single(z2, hspec, n)[:, :length]
    else:
        y = _fft_conv_pair(z.reshape(2, length, d), hspec, n1, n2)
    fn = lambda yt, zt, x0t, sk: ((yt + sk * zt) * x0t,)
    g = _rowwise(fn, [y.reshape(m, d), z, x0c], [p["hy_skip"][j].reshape(1, 1, d)], [d], name="hy_gate")[0]
    return _mm(g, p["hy_out_w"][j].astype(BF16), bias=p["hy_out_b"][j])


def _fft_conv_pair_single(z2, hspec, n, tcol=512):
    c = z2.shape[-1]
    w1 = _cplx_mat(_inner_phase(n, -1.0)).astype(BF16)
    w2 = (_cplx_mat(_inner_phase(n, 1.0)) * (1.0 / n)).astype(BF16)
    tcol = min(tcol, c)
    return pl.pallas_call(
        _spec_mul_body,
        grid=(c // tcol,),
        in_specs=[pl.BlockSpec((2 * n, 2 * n), lambda j: (0, 0)),
                  pl.BlockSpec((2 * n, 2 * n), lambda j: (0, 0)),
                  pl.BlockSpec((2, n, tcol), lambda j: (0, 0, j)),
                  pl.BlockSpec((2, n, tcol), lambda j: (0, 0, j))],
        out_specs=pl.BlockSpec((2, n, tcol), lambda j: (0, 0, j)),
        out_shape=jax.ShapeDtypeStruct((2, n, c), F32),
        compiler_params=_cparams("parallel"),
        name="fft_mid1",
    )(w1, w2, z2, hspec)


def _fnet(x, scale, shift, rows_per_group, batch, length, p, j):
    m, d = x.shape
    gc = d // FN_GROUPS
    ph = _inner_phase(gc, -1.0)
    eye = jnp.eye(FN_GROUPS, dtype=F32)
    w_c = jnp.concatenate([jnp.kron(eye, jnp.cos(ph)), jnp.kron(eye, jnp.sin(ph))], 1).astype(BF16)
    w = _mm(x, w_c, mod=(scale, shift), rows_per_group=rows_per_group)
    if length <= 1024:
        n1, n2 = length, 1
    else:
        n1, n2 = _split_len(length)
    norm = 1.0 / math.sqrt(length * gc)
    if n2 == 1:
        ph1 = _inner_phase(n1, -1.0)
        wr = (jnp.concatenate([jnp.cos(ph1), -jnp.sin(ph1)], 1) * norm).astype(BF16)
        y = _lmm(wr, pl.BlockSpec((n1, 2 * n1), lambda b, c: (0, 0)),
                 [w.reshape(batch, n1, 2 * d)] * 2,
                 [pl.BlockSpec((None, n1, d), lambda b, c: (b, 0, 0)),
                  pl.BlockSpec((None, n1, d), lambda b, c: (b, 0, 1))],
                 (batch, n1, d), pl.BlockSpec((None, n1, d), lambda b, c: (b, 0, 0)), (batch, 1), "fn_pos1")
        y = y.reshape(m, d)
    else:
        m_in = _cplx_mat(_outer_mats(n1, n2, n2, -1.0)).astype(BF16)
        wv = w.reshape(batch, n2, n1 * 2 * d)
        a = _lmm(m_in, pl.BlockSpec((1, 2 * n2, 2 * n2), lambda b, s: (s, 0, 0)),
                 [wv, wv],
                 [pl.BlockSpec((None, n2, d), lambda b, s: (b, 0, 2 * s)),
                  pl.BlockSpec((None, n2, d), lambda b, s: (b, 0, 2 * s + 1))],
                 (batch, 2, n1, n2, d), pl.BlockSpec((None, 2, 1, n2, d), lambda b, s: (b, 0, s, 0, 0)),
                 (batch, n1), "fn_pos_in", BF16)
        ph1 = _inner_phase(n1, -1.0)
        wr = (jnp.concatenate([jnp.cos(ph1), -jnp.sin(ph1)], 1) * norm).astype(BF16)
        cols = n2 * d
        tcol = 1024
        y = _lmm(wr, pl.BlockSpec((n1, 2 * n1), lambda b, c: (0, 0)),
                 [a.reshape(batch, 2, n1, cols)],
                 [pl.BlockSpec((None, 2, n1, tcol), lambda b, c: (b, 0, 0, c))],
                 (batch, n1, cols), pl.BlockSpec((None, n1, tcol), lambda b, c: (b, 0, c)),
                 (batch, cols // tcol), "fn_pos_mid")
        y = y.reshape(m, d)
    return _mm(y, p["fn_out_w"][j].astype(BF16), bias=p["fn_out_b"][j])


def _head_l2(t, extra):
    outs = []
    for h in range(t.shape[1] // GD_DK):
        th = t[:, h * GD_DK:(h + 1) * GD_DK]
        outs.append(th * (lax.rsqrt(jnp.sum(th * th, -1, keepdims=True) + 1e-6) * extra))
    return jnp.concatenate(outs, -1)


def _silu(v):
    return v * jax.nn.sigmoid(v)


def _gdn_scan_body(q_ref, k_ref, v_ref, gb_ref, gbt_ref, s0_ref, o_ref, sfin_ref, s_scr, *, n_chunks):
    direction = pl.program_id(0)
    c = pl.program_id(1)
    cs = GD_CHUNK
    nb = q_ref.shape[0]

    @pl.when(c == 0)
    def _():
        s_scr[...] = s0_ref[...]

    ri = lax.broadcasted_iota(jnp.int32, (cs, cs), 0)
    ci = lax.broadcasted_iota(jnp.int32, (cs, cs), 1)
    lag = (ri - ci) * (1 - 2 * direction)
    incl = lag >= 0
    strict = lag > 0
    tri = incl.astype(F32)
    tri_t = (lag <= 0).astype(F32)
    eye = (ri == ci).astype(F32)
    pair_masks = []
    for lvl in range(int(math.log2(cs))):
        rb, cb = lax.shift_right_logical(ri, lvl), lax.shift_right_logical(ci, lvl)
        pair_masks.append((jnp.abs(rb - cb) == 1) & ((jnp.minimum(rb, cb) & 1) == 0))
    gb = [gb_ref[b] for b in range(nb)]
    gc_cols = [_dot_hi(tri, gb[b]) for b in range(nb)]
    gc_rows = [_dot_hi(gbt_ref[b], tri_t) for b in range(nb)]
    tot = [jnp.sum(gb[b], 0, keepdims=True) for b in range(nb)]
    ch = [(b, h) for b in range(nb) for h in range(GD_HEADS)]
    hs = range(len(ch))
    nt = (((1,), (1,)), ((), ()))
    tn = (((0,), (0,)), ((), ()))
    sl = [slice(h * GD_DK, (h + 1) * GD_DK) for _, h in ch]
    k = [k_ref[b, :, sl[i]] for i, (b, h) in enumerate(ch)]
    gcol = [gc_cols[b][:, h:h + 1] for b, h in ch]
    beta = [gb[b][:, GD_HEADS + h:GD_HEADS + h + 1] for b, h in ch]
    gtot = [tot[b][:, h:h + 1] for b, h in ch]
    decay = [jnp.exp(jnp.where(incl, gcol[i] - gc_rows[b][h:h + 1, :], -jnp.inf)) for i, (b, h) in enumerate(ch)]
    eg = [jnp.exp(gcol[h]) for h in hs]
    kb = [k[h] * beta[h] for h in hs]
    kbf = [k[h].astype(BF16) for h in hs]
    kk = [lax.dot_general(kb[h].astype(BF16), kbf[h], nt, preferred_element_type=F32) for h in hs]
    a = [jnp.where(strict, kk[h] * decay[h], 0.0) for h in hs]
    inv = [eye - jnp.where(pair_masks[0], a[h], 0.0) for h in hs]
    for pm in pair_masks[1:]:
        tn_s = [_dot3(inv[h], jnp.where(pm, a[h], 0.0)) for h in hs]
        tnt = [_dot3(tn_s[h], inv[h]) for h in hs]
        inv = [inv[h] - tnt[h] for h in hs]
    rhs = [jnp.concatenate([v_ref[b, :, sl[i]] * beta[i], kb[i] * eg[i]], -1) for i, (b, h) in enumerate(ch)]
    sol = [_dot3(inv[h], rhs[h]) for h in hs]
    q = [q_ref[b, :, sl[i]] for i, (b, h) in enumerate(ch)]
    qk = [lax.dot_general(q[h].astype(BF16), kbf[h], nt, preferred_element_type=F32) for h in hs]
    sb = [s_scr[b, h].astype(BF16) for b, h in ch]
    ws = [_dot(sol[h][:, GD_DK:].astype(BF16), sb[h]) for h in hs]
    qs = [_dot((q[h] * eg[h]).astype(BF16), sb[h]) for h in hs]
    vnb = [(sol[h][:, :GD_DK] - ws[h]).astype(BF16) for h in hs]
    av = [_dot((qk[h] * decay[h]).astype(BF16), vnb[h]) for h in hs]
    kv = [lax.dot_general((k[h] * jnp.exp(gtot[h] - gcol[h])).astype(BF16), vnb[h], tn,
                          preferred_element_type=F32) for h in hs]
    for i, (b, h) in enumerate(ch):
        s_scr[b, h] = s_scr[b, h] * jnp.exp(gtot[i]) + kv[i]
    for b in range(nb):
        o_ref[b] = jnp.concatenate([qs[i] + av[i] for i, (bb, _) in enumerate(ch) if bb == b], -1)

    @pl.when(c == n_chunks - 1)
    def _():
        sfin_ref[...] = s_scr[...]


def _gdn_scan(q, k, v, gb, gbt, s0):
    b, length, w = q.shape
    n_chunks = length // GD_CHUNK
    cidx = lambda d, c: c + d * (n_chunks - 1 - 2 * c)
    seq = lambda d, c: (0, cidx(d, c), 0)
    state = pl.BlockSpec((None, b, GD_HEADS, GD_DK, GD_DK), lambda d, c: (d, 0, 0, 0, 0))
    return pl.pallas_call(
        functools.partial(_gdn_scan_body, n_chunks=n_chunks),
        grid=(2, n_chunks),
        in_specs=[pl.BlockSpec((b, GD_CHUNK, w), seq),
                  pl.BlockSpec((b, GD_CHUNK, w), seq),
                  pl.BlockSpec((b, GD_CHUNK, w), seq),
                  pl.BlockSpec((None, b, GD_CHUNK, 128), lambda d, c: (d, 0, cidx(d, c), 0)),
                  pl.BlockSpec((None, b, None, 16, GD_CHUNK), lambda d, c: (d, 0, cidx(d, c), 0, 0)),
                  state],
        out_specs=[pl.BlockSpec((None, b, GD_CHUNK, w), lambda d, c: (d, 0, cidx(d, c), 0)), state],
        out_shape=[jax.ShapeDtypeStruct((2, b, length, w), F32),
                   jax.ShapeDtypeStruct((2, b, GD_HEADS, GD_DK, GD_DK), F32)],
        scratch_shapes=[pltpu.VMEM((b, GD_HEADS, GD_DK, GD_DK), F32)],
        compiler_params=_cparams("parallel", "arbitrary"),
        name="gdn_scan",
    )(q, k, v, gb, gbt, s0)


def _gdn_inputs(x, scale, shift, rows_per_group, batch, length, p, j):
    m, d = x.shape
    wd = GD_HEADS * GD_DK
    in_w = p["gd_in_w"][j]
    u = _mm(x, in_w[:, :4 * wd].astype(BF16), mod=(scale, shift), rows_per_group=rows_per_group)
    w_ab = jnp.pad(in_w[:, 4 * wd:], ((0, 0), (0, 128 - 4 * GD_HEADS))).astype(BF16)
    ab = _mm(x, w_ab, mod=(scale, shift), rows_per_group=rows_per_group)
    qscale = GD_DK ** -0.5
    post = lambda qc, kc, vc: (_head_l2(_silu(qc), qscale), _head_l2(_silu(kc), 1.0), _silu(vc))
    q, k, v = _conv3(u, p["gd_conv"][j], [0, wd, 2 * wd], 512, length, post, 3, name="gd_conv")
    nh = GD_HEADS
    a_par = jnp.zeros((1, 1, 128), F32).at[0, 0, :2 * nh].set(-jnp.exp(p["gd_a_log"][j]).reshape(-1))
    dt_par = jnp.zeros((1, 1, 128), F32).at[0, 0, :2 * nh].set(p["gd_dt_bias"][j].reshape(-1))

    def gate_fn(abt, an, dtb):
        pre = abt + dtb
        sp = jnp.maximum(pre, 0.0) + jnp.log(1.0 + jnp.exp(-jnp.abs(pre)))
        lane = lax.broadcasted_iota(jnp.int32, abt.shape, 1)
        return (jnp.where(lane < 2 * nh, an * sp, jax.nn.sigmoid(abt)),)

    gall = _rowwise(gate_fn, [ab], [a_par, dt_par], [128], name="gd_gate")[0]
    pad = jnp.zeros((m, 128 - 2 * nh), F32)
    gb = jnp.stack([jnp.concatenate([gall[:, dr * nh:(dr + 1) * nh],
                                     gall[:, (2 + dr) * nh:(3 + dr) * nh], pad], -1) for dr in range(2)])
    gb = gb.reshape(2, batch, length, 128)
    gbt = jnp.swapaxes(gb[..., :2 * nh].reshape(2, batch, length // GD_CHUNK, GD_CHUNK, 2 * nh), -1, -2)
    rs = lambda t: t.reshape(batch, length, wd)
    return rs(q), rs(k), rs(v), gb, gbt, u


def _gdn_out(o2, u, norm_g, out_w):
    m, wd = o2.shape[1], o2.shape[2]
    ng = jnp.tile(norm_g, wd // norm_g.shape[0]).reshape(1, 1, wd)

    def body(of_ref, ob_ref, z_ref, ng_ref, o_ref):
        o = of_ref[...] + ob_ref[...]
        outs = []
        for h in range(wd // GD_DK):
            oh = o[:, h * GD_DK:(h + 1) * GD_DK]
            outs.append(oh * lax.rsqrt(jnp.mean(oh * oh, -1, keepdims=True) + 1e-6))
        o_ref[...] = jnp.concatenate(outs, -1) * ng_ref[...] * _silu(z_ref[...])

    tm = math.gcd(512, m)
    g = pl.pallas_call(
        body,
        grid=(m // tm,),
        in_specs=[pl.BlockSpec((None, tm, wd), lambda i: (0, i, 0)),
                  pl.BlockSpec((None, tm, wd), lambda i: (1, i, 0)),
                  pl.BlockSpec((tm, wd), lambda i: (i, 3)),
                  pl.BlockSpec((None, 1, wd), lambda i: (0, 0, 0))],
        out_specs=pl.BlockSpec((tm, wd), lambda i: (i, 0)),
        out_shape=jax.ShapeDtypeStruct((m, wd), F32),
        compiler_params=_cparams("parallel"),
        name="gd_norm",
    )(o2, o2, u, ng)
    return _mm(g, out_w.astype(BF16))


def _topk_rows(s, n_take, val_ref, idx_ref, base, rid=None):
    big = jnp.iinfo(jnp.int32).max
    rows_in_order = rid is None
    if rows_in_order:
        rid = lax.broadcasted_iota(jnp.int32, s.shape, 0)
    for t in range(n_take):
        mx = jnp.max(s, 0, keepdims=True)
        if rows_in_order:
            first = jnp.full((8, s.shape[1]), big, jnp.int32)
            for j in reversed(range(s.shape[0] // 8)):
                first = jnp.where(s[8 * j:8 * j + 8] == mx, rid[8 * j:8 * j + 8], first)
            am = jnp.min(first, 0, keepdims=True)
        else:
            am = jnp.min(jnp.where(s == mx, rid, big), 0, keepdims=True)
        val_ref[base + t:base + t + 1, :] = mx
        idx_ref[base + t:base + t + 1, :] = am
        s = jnp.where(rid == am, -jnp.inf, s)


def _pk_topk_body(q_ref, keys_ref, i1_ref, i2_ref, gate_ref, sv_scr, si_scr, cv_scr, ci_scr,
                  i1p_scr, i2p_scr, gp_scr):
    kk = PK_TOPK
    tt = q_ref.shape[0]
    for h in range(PK_HEADS):
        for p in range(2):
            qh = q_ref[:, (2 * h + p) * PK_DH:(2 * h + p + 1) * PK_DH].astype(BF16)
            st = lax.dot_general(keys_ref[h, p], qh, (((1,), (1,)), ((), ())),
                                 preferred_element_type=F32)
            _topk_rows(st, kk, sv_scr, si_scr, p * kk)
        sv1, sv2 = sv_scr[0:kk, :], sv_scr[kk:2 * kk, :]
        si1, si2 = si_scr[0:kk, :], si_scr[kk:2 * kk, :]
        r8 = lax.broadcasted_iota(jnp.int32, (8, tt), 0)
        cand = jnp.concatenate([sv1[0:8, :] + sv2[r2:r2 + 1, :] for r2 in range(8)]
                               + [sv1[8:kk, :] + sv2[0:1, :], sv1[0:1, :] + sv2[8:kk, :]], 0)
        cid = jnp.concatenate([r8 * kk + r2 for r2 in range(8)] + [(r8 + 8) * kk, r8 + 8], 0)
        _topk_rows(cand, kk, cv_scr, ci_scr, 0, cid)
        cv, ci = cv_scr[...], ci_scr[...]
        a1, a2 = lax.shift_right_logical(ci, int(math.log2(kk))), ci & (kk - 1)
        i1 = jnp.zeros((kk, tt), jnp.int32)
        i2 = jnp.zeros((kk, tt), jnp.int32)
        for r in range(kk):
            i1 = i1 + jnp.where(a1 == r, si1[r:r + 1, :], 0)
            i2 = i2 + jnp.where(a2 == r, si2[r:r + 1, :], 0)
        i1p_scr[h * kk:(h + 1) * kk, :] = i1
        i2p_scr[h * kk:(h + 1) * kk, :] = i2
        e = jnp.exp(cv - jnp.max(cv, 0, keepdims=True))
        gp_scr[h * kk:(h + 1) * kk, :] = e / jnp.sum(e, 0, keepdims=True)
    i1_ref[...] = i1p_scr[...].T
    i2_ref[...] = i2p_scr[...].T
    gate_ref[...] = gp_scr[...].T


def _pk_topk(q, keys_bf16):
    m = q.shape[0]
    tt = PK_TOK
    hk = PK_HEADS * PK_TOPK
    assert hk == tt
    return pl.pallas_call(
        _pk_topk_body,
        grid=(m // tt,),
        in_specs=[pl.BlockSpec((tt, q.shape[1]), lambda i: (i, 0)),
                  pl.BlockSpec(keys_bf16.shape, lambda i: (0, 0, 0, 0))],
        out_specs=[pl.BlockSpec((tt, hk), lambda i: (i, 0))] * 3,
        out_shape=[jax.ShapeDtypeStruct((m, hk), jnp.int32), jax.ShapeDtypeStruct((m, hk), jnp.int32),
                   jax.ShapeDtypeStruct((m, hk), F32)],
        scratch_shapes=[pltpu.VMEM((2 * PK_TOPK, tt), F32), pltpu.VMEM((2 * PK_TOPK, tt), jnp.int32),
                        pltpu.VMEM((PK_TOPK, tt), F32), pltpu.VMEM((PK_TOPK, tt), jnp.int32),
                        pltpu.VMEM((hk, tt), jnp.int32), pltpu.VMEM((hk, tt), jnp.int32),
                        pltpu.VMEM((hk, tt), F32)],
        compiler_params=_cparams("parallel"),
        name="pk_topk",
    )(q, keys_bf16)


def _pk_dense_body(x_ref, sc_ref, sh_ref, gt_ref, lg_ref, lb_ref, i1_ref, i2_ref, gate_ref, ut_ref, v_ref,
                   o_ref, hb_scr, act_scr, w_scr, grid_scr, acc_scr, *, alpha, n_chunks):
    e = pl.program_id(1)
    tm = x_ref.shape[0]
    nk = PK_NKEYS
    per = ut_ref.shape[1] // nk

    @pl.when(e == 0)
    def _():
        hb_scr[...] = (x_ref[...] * (1.0 + sc_ref[...]) + sh_ref[...]).astype(BF16)
        act_scr[...] = jnp.zeros_like(act_scr)

    @pl.when(e < n_chunks)
    def _():
        hb = hb_scr[...]
        i1 = i1_ref[...]
        i2 = i2_ref[...]
        act = act_scr[...]
        for kp in range(per // 2):
            s = _dot(hb, ut_ref[:, 2 * kp * nk:2 * (kp + 1) * nk])
            for k in (2 * kp, 2 * kp + 1):
                got = jnp.take_along_axis(s[:, (k % 2) * nk:(k % 2 + 1) * nk], i2, axis=1)
                act = jnp.where(i1 == e * per + k, got, act)
        act_scr[...] = act

    @pl.when(e == n_chunks)
    def _():
        w_scr[...] = jax.nn.gelu(act_scr[...]) * gate_ref[...]
        acc_scr[...] = jnp.zeros_like(acc_scr)
        sub = lax.broadcasted_iota(jnp.int32, (nk, nk), 0)
        key1 = jnp.where(sub < nk // 2, 2 * sub, 2 * sub - (nk - 1))

        def tok(tg, carry):
            for t0 in range(0, PK_UNROLL, 8):
                words = []
                for u in range(8):
                    t = tg * PK_UNROLL + t0 + u
                    wrow = w_scr[pl.ds(t, 1), :]
                    at = jnp.where(key1 == i1_ref[pl.ds(t, 1), :], wrow, 0.0).astype(BF16)
                    bt = (sub == i2_ref[pl.ds(t, 1), :]).astype(BF16)
                    g = lax.dot_general(at, bt, (((1,), (1,)), ((), ())), preferred_element_type=F32)
                    g = lax.bitcast_convert_type(g.astype(BF16).astype(F32), jnp.int32)
                    words.append(lax.shift_right_logical(g[:nk // 2], 16) | (g[nk // 2:] & jnp.int32(-65536)))
                first = pl.multiple_of(tg * PK_UNROLL + t0, 8)
                grid_scr[:, pl.ds(first, 8), :] = jnp.swapaxes(jnp.stack(words, 0), 0, 1)
            return carry

        lax.fori_loop(0, tm // PK_UNROLL, tok, 0)

    @pl.when(e >= n_chunks)
    def _():
        c = e - n_chunks
        parts = []
        for kp in range(per // 2):
            wd = grid_scr[c * (per // 2) + kp]
            parts.append(lax.bitcast_convert_type(lax.shift_left(wd, 16), F32))
            parts.append(lax.bitcast_convert_type(wd & jnp.int32(-65536), F32))
        acc_scr[...] += _dot(jnp.concatenate(parts, -1).astype(BF16), v_ref[...])

    @pl.when(e == 2 * n_chunks - 1)
    def _():
        o_ref[...] = _ln(alpha * x_ref[...] + gt_ref[...] * acc_scr[...], lg_ref[...], lb_ref[...])


def _pk_dense(x, scale, shift, gate_vec, ln_g, ln_b, i1, i2, gate, ut_bf16, v_bf16, rows_per_group, alpha):
    m, d = x.shape
    n_exp = v_bf16.shape[0]
    tm = math.gcd(PK_TM, m, rows_per_group)
    n_chunks = n_exp // PK_EC
    g = rows_per_group // tm
    assert n_exp % PK_EC == 0 and PK_EC % PK_NKEYS == 0
    row = lambda i, e: (i, 0)
    vec = lambda i, e: (i // g, 0, 0)
    one = lambda i, e: (0, 0, 0)
    npk = i1.shape[1]
    return pl.pallas_call(
        functools.partial(_pk_dense_body, alpha=alpha, n_chunks=n_chunks),
        grid=(m // tm, 2 * n_chunks),
        in_specs=[pl.BlockSpec((tm, d), row),
                  pl.BlockSpec((None, 1, d), vec), pl.BlockSpec((None, 1, d), vec), pl.BlockSpec((None, 1, d), vec),
                  pl.BlockSpec((None, 1, d), one), pl.BlockSpec((None, 1, d), one),
                  pl.BlockSpec((tm, npk), row), pl.BlockSpec((tm, npk), row), pl.BlockSpec((tm, npk), row),
                  pl.BlockSpec((d, PK_EC), lambda i, e: (0, jnp.minimum(e, n_chunks - 1))),
                  pl.BlockSpec((PK_EC, d), lambda i, e: (jnp.maximum(e - n_chunks, 0), 0))],
        out_specs=pl.BlockSpec((tm, d), row),
        out_shape=jax.ShapeDtypeStruct((m, d), F32),
        scratch_shapes=[pltpu.VMEM((tm, d), BF16), pltpu.VMEM((tm, npk), F32), pltpu.VMEM((tm, npk), F32),
                        pltpu.VMEM((PK_NKEYS // 2, tm, PK_NKEYS), jnp.int32), pltpu.VMEM((tm, d), F32)],
        compiler_params=_cparams("parallel", "arbitrary"),
        name="pk_dense",
    )(x, scale, shift, gate_vec, ln_g, ln_b, i1, i2, gate, ut_bf16, v_bf16)


def _peer_ln(x, scale, shift, gate_vec, ln_g, ln_b, rows_per_group, alpha, wq_bf16, keys_bf16, ut_bf16, v_bf16):
    q = _mm(x, wq_bf16, mod=(scale, shift), rows_per_group=rows_per_group)
    i1, i2, gate = _pk_topk(q, keys_bf16)
    return _pk_dense(x, scale, shift, gate_vec, ln_g, ln_b, i1, i2, gate, ut_bf16, v_bf16, rows_per_group, alpha)


def kernel(x, c, ctx, c_ctx, ada_w, ada_b, ln_g, ln_b, pk_wq, pk_keys, pk_u, pk_v, hy_in_w, hy_in_b, hy_conv, hy_f_w1, hy_f_b1, hy_f_w2, hy_f_b2, hy_f_w3, hy_skip, hy_out_w, hy_out_b, gd_in_w, gd_conv, gd_a_log, gd_dt_bias, gd_norm_g, gd_out_w, fn_out_w, fn_out_b):
    p = dict(hy_in_w=hy_in_w, hy_in_b=hy_in_b, hy_conv=hy_conv, hy_f_w1=hy_f_w1, hy_f_b1=hy_f_b1,
             hy_f_w2=hy_f_w2, hy_f_b2=hy_f_b2, hy_f_w3=hy_f_w3, hy_skip=hy_skip, hy_out_w=hy_out_w,
             hy_out_b=hy_out_b, gd_in_w=gd_in_w, gd_conv=gd_conv, gd_a_log=gd_a_log, gd_dt_bias=gd_dt_bias,
             fn_out_w=fn_out_w, fn_out_b=fn_out_b)
    b, length, d = x.shape
    lc = ctx.shape[1]
    depth = ada_w.shape[0]
    alpha = (2 * depth) ** 0.25
    xl = _pos_add(x)
    xc = ctx.reshape(b * lc, d)
    gdn_layers = [i for i in range(depth) if i % N_MIXERS == 1]
    ctx_until = gdn_layers[-1] if gdn_layers else -1
    cond = jnp.concatenate([c, c_ctx[None], jnp.zeros((8 - b - 1, d), F32)], 0)
    for i in range(depth):
        kind, j = i % N_MIXERS, i // N_MIXERS
        ctx_in, ctx_out = i <= ctx_until, i < ctx_until
        mod = _mm(cond, ada_w[i].astype(BF16), bias=ada_b[i], silu_in=True, tn=2048)
        mod = mod.reshape(8, N_MOD, 1, d)
        ml = [mod[:b, t] for t in range(N_MOD)]
        mc = [mod[b:b + 1, t] for t in range(N_MOD)]
        lg = [ln_g[i, t].reshape(1, 1, d) for t in range(2)]
        lb = [ln_b[i, t].reshape(1, 1, d) for t in range(2)]
        yc = None
        if kind == 0:
            yl = _hyena(xl, ml[1], ml[0], length, length, p, j)
            if ctx_out:
                yc = _hyena(xc, mc[1], mc[0], b * lc, lc, p, j)
        elif kind == 1:
            qc, kc, vc, gbc, gbtc, uc = _gdn_inputs(xc, mc[1], mc[0], b * lc, b, lc, p, j)
            ql, kl, vl, gbl, gbtl, ul = _gdn_inputs(xl, ml[1], ml[0], length, b, length, p, j)
            s0 = jnp.zeros((2, b, GD_HEADS, GD_DK, GD_DK), F32)
            oc, s_ctx = _gdn_scan(qc, kc, vc, gbc, gbtc, s0)
            ol, _ = _gdn_scan(ql, kl, vl, gbl, gbtl, s_ctx)
            yl = _gdn_out(ol.reshape(2, b * length, -1), ul, gd_norm_g[j], gd_out_w[j])
            if ctx_out:
                yc = _gdn_out(oc.reshape(2, b * lc, -1), uc, gd_norm_g[j], gd_out_w[j])
        else:
            yl = _fnet(xl, ml[1], ml[0], length, b, length, p, j)
            if ctx_out:
                yc = _fnet(xc, mc[1], mc[0], b * lc, b, lc, p, j)
        wq = pk_wq[i].astype(BF16)
        keys = pk_keys[i].astype(BF16)
        ut, vt = pk_u[i].astype(BF16).T, pk_v[i].astype(BF16)
        xl = _res_ln(xl, yl, ml[2], lg[0], lb[0], alpha, length)
        xl = _peer_ln(xl, ml[4], ml[3], ml[5], lg[1], lb[1], length, alpha, wq, keys, ut, vt)
        if ctx_out:
            xc = _res_ln(xc, yc, mc[2], lg[0], lb[0], alpha, b * lc)
            xc = _peer_ln(xc, mc[4], mc[3], mc[5], lg[1], lb[1], b * lc, alpha, wq, keys, ut, vt)
    return xl.reshape(b, length, d)
```

```python
import functools
import math

import numpy as np
import jax
import jax.numpy as jnp
from jax import lax
from jax.experimental import pallas as pl
from jax.experimental.pallas import tpu as pltpu

F32 = jnp.float32
BF16 = jnp.bfloat16

GRID_W = 64
N_MIXERS = 3
N_MOD = 6
LN_EPS = 1e-5
HY_EMB = 33
HY_BANDS = (HY_EMB - 1) // 2
HY_SHIFT = 0.05
HY_TARGET = 1e-2
HY_MIN_DECAY = math.log(HY_TARGET) / 1.5
HY_MAX_DECAY = math.log(HY_TARGET) / 0.3
GD_HEADS = 8
GD_DK = 128
GD_CHUNK = 64
FN_GROUPS = 4
PK_HEADS = 8
PK_NKEYS = 128
PK_DH = 128
PK_TOPK = 16
PK_TOK = 128
PK_TM_SCORE = 1024
PK_TM = 512
PK_EC = 2048
PK_UNROLL = 32

VMEM_LIMIT_BYTES = 56 * 1024 * 1024


def _cparams(*sem):
    return pltpu.CompilerParams(dimension_semantics=sem, vmem_limit_bytes=VMEM_LIMIT_BYTES)


def _dot(a, b):
    return jnp.dot(a, b, preferred_element_type=F32)


def _dot_hi(a, b):
    return jnp.dot(a, b, preferred_element_type=F32, precision=lax.Precision.HIGHEST)


def _split(a):
    hi = a.astype(BF16)
    lo = (a - hi.astype(F32)).astype(BF16)
    return hi, lo


def _dot3(a, b):
    ah, al = _split(a)
    bh, bl = _split(b)
    return _dot(ah, bh) + (_dot(ah, bl) + _dot(al, bh))


def _mm_body(*refs, has_mod, has_bias, silu_in):
    a_ref, w_ref = refs[0], refs[1]
    k = 2
    a = a_ref[...]
    if has_mod:
        a = a * (1.0 + refs[k][...]) + refs[k + 1][...]
        k += 2
    if silu_in:
        a = a * jax.nn.sigmoid(a)
    o = _dot(a.astype(BF16), w_ref[...])
    if has_bias:
        o = o + refs[k][...]
        k += 1
    refs[k][...] = o


def _mm(a, w_bf16, bias=None, mod=None, rows_per_group=None, silu_in=False, tm=1024, tn=None):
    m, k = a.shape
    n = w_bf16.shape[1]
    tm = math.gcd(tm, m, rows_per_group or m)
    if tn is None:
        tn = max(t for t in range(128, min(n, 2048) + 1, 128) if n % t == 0)
    assert m % tm == 0 and n % tn == 0
    ins = [a, w_bf16]
    specs = [pl.BlockSpec((tm, k), lambda j, i: (i, 0)), pl.BlockSpec((k, tn), lambda j, i: (0, j))]
    if mod is not None:
        assert rows_per_group % tm == 0
        g = rows_per_group // tm
        for v in mod:
            ins.append(v)
            specs.append(pl.BlockSpec((None, 1, k), lambda j, i: (i // g, 0, 0)))
    if bias is not None:
        ins.append(bias.reshape(1, n))
        specs.append(pl.BlockSpec((1, tn), lambda j, i: (0, j)))
    return pl.pallas_call(
        functools.partial(_mm_body, has_mod=mod is not None, has_bias=bias is not None, silu_in=silu_in),
        grid=(n // tn, m // tm),
        in_specs=specs,
        out_specs=pl.BlockSpec((tm, tn), lambda j, i: (i, j)),
        out_shape=jax.ShapeDtypeStruct((m, n), F32),
        compiler_params=_cparams("parallel", "parallel"),
        name="mm",
    )(*ins)


def _rowwise(fn, rows, vecs, out_cols, rows_per_group=None, tm=512, name="rowwise"):
    m = rows[0].shape[0]
    tm = math.gcd(tm, m, rows_per_group or m)
    assert m % tm == 0
    n_r, n_v, n_o = len(rows), len(vecs), len(out_cols)

    def body(*refs):
        outs = fn(*[r[...] for r in refs[:n_r + n_v]])
        for o_ref, o in zip(refs[n_r + n_v:], outs):
            o_ref[...] = o

    specs = [pl.BlockSpec((tm, r.shape[1]), lambda i: (i, 0)) for r in rows]
    for v in vecs:
        if v.shape[0] == 1:
            specs.append(pl.BlockSpec((None, 1, v.shape[2]), lambda i: (0, 0, 0)))
        else:
            assert rows_per_group % tm == 0
            g = rows_per_group // tm
            specs.append(pl.BlockSpec((None, 1, v.shape[2]), lambda i, g=g: (i // g, 0, 0)))
    return pl.pallas_call(
        body,
        grid=(m // tm,),
        in_specs=specs,
        out_specs=[pl.BlockSpec((tm, c), lambda i: (i, 0)) for c in out_cols],
        out_shape=[jax.ShapeDtypeStruct((m, c), F32) for c in out_cols],
        compiler_params=_cparams("parallel"),
        name=name,
    )(*rows, *vecs)


def _ln(v, g, b):
    mu = jnp.mean(v, -1, keepdims=True)
    d = v - mu
    var = jnp.mean(d * d, -1, keepdims=True)
    return d * lax.rsqrt(var + LN_EPS) * g + b


def _res_ln(x, y, gate, ln_g, ln_b, alpha, rows_per_group):
    fn = lambda xt, yt, gt, lg, lb: (_ln(alpha * xt + gt * yt, lg, lb),)
    return _rowwise(fn, [x, y], [gate, ln_g, ln_b], [x.shape[1]], rows_per_group, name="res_ln")[0]


def _pos_add_body(x_ref, er_ref, ec_ref, o_ref):
    half = er_ref.shape[-1]
    x = x_ref[...]
    er = jnp.broadcast_to(er_ref[...], x.shape[:2] + (half,))
    ec = jnp.broadcast_to(ec_ref[...][None], x.shape[:2] + (half,))
    o_ref[...] = x + jnp.concatenate([er, ec], -1)


def _pos_add(x):
    b, length, d = x.shape
    rows = length // GRID_W
    quarter = d // 4
    omega = 1.0 / (10000.0 ** (jnp.arange(quarter, dtype=F32) / quarter))
    er = jnp.arange(rows, dtype=F32)[:, None] * omega
    ec = jnp.arange(GRID_W, dtype=F32)[:, None] * omega
    emb_r = jnp.concatenate([jnp.sin(er), jnp.cos(er)], -1).reshape(rows, 1, d // 2)
    emb_c = jnp.concatenate([jnp.sin(ec), jnp.cos(ec)], -1)
    rt = 8
    x4 = x.reshape(b, rows, GRID_W, d)
    out = pl.pallas_call(
        _pos_add_body,
        grid=(b, rows // rt),
        in_specs=[pl.BlockSpec((None, rt, GRID_W, d), lambda i, j: (i, j, 0, 0)),
                  pl.BlockSpec((rt, 1, d // 2), lambda i, j: (j, 0, 0)),
                  pl.BlockSpec((GRID_W, d // 2), lambda i, j: (0, 0))],
        out_specs=pl.BlockSpec((None, rt, GRID_W, d), lambda i, j: (i, j, 0, 0)),
        out_shape=jax.ShapeDtypeStruct(x4.shape, F32),
        compiler_params=_cparams("parallel", "parallel"),
        name="pos_add",
    )(x4, emb_r, emb_c)
    return out.reshape(b * length, d)


def _conv3_body(*refs, n_parts, tm, seq_len, post, n_vec):
    i = pl.program_id(0)
    first = (i * tm) % seq_len == 0
    last = ((i + 1) * tm) % seq_len == 0
    row = lax.broadcasted_iota(jnp.int32, (tm, 1), 0)
    parts = []
    for p in range(n_parts):
        main_ref, prev_ref, next_ref, w_ref = refs[4 * p:4 * p + 4]
        u = main_ref[...]
        w = w_ref[...]
        prev = jnp.where(first, 0.0, prev_ref[7:8, :])
        nxt = jnp.where(last, 0.0, next_ref[0:1, :])
        up = jnp.where(row == 0, prev, pltpu.roll(u, 1, 0))
        dn = jnp.where(row == tm - 1, nxt, pltpu.roll(u, tm - 1, 0))
        parts.append(up * w[0:1, :] + u * w[1:2, :] + dn * w[2:3, :])
    k = 4 * n_parts
    vecs = [refs[k + j][...] for j in range(n_vec)]
    outs = post(*parts, *vecs)
    for o_ref, o in zip(refs[k + n_vec:], outs):
        o_ref[...] = o


def _conv3(u, w, col_parts, tc, seq_len, post, n_out, vecs=(), tm=256, name="conv3"):
    m = u.shape[0]
    tm = min(tm, seq_len)
    assert seq_len % tm == 0 and m % tm == 0 and tm % 8 == 0
    width = col_parts[1] - col_parts[0] if len(col_parts) > 1 else tc
    ncol = width // tc
    t8 = tm // 8
    nb8 = m // 8
    ins, specs = [], []
    for c0 in col_parts:
        cb = c0 // tc
        ins += [u, u, u, w]
        specs += [
            pl.BlockSpec((tm, tc), lambda i, j, cb=cb: (i, cb + j)),
            pl.BlockSpec((8, tc), lambda i, j, cb=cb: (jnp.maximum(i * t8 - 1, 0), cb + j)),
            pl.BlockSpec((8, tc), lambda i, j, cb=cb: (jnp.minimum((i + 1) * t8, nb8 - 1), cb + j)),
            pl.BlockSpec((3, tc), lambda i, j, cb=cb: (0, cb + j)),
        ]
    for v in vecs:
        ins.append(v)
        specs.append(pl.BlockSpec((1, tc), lambda i, j: (0, j)))
    return pl.pallas_call(
        functools.partial(_conv3_body, n_parts=len(col_parts), tm=tm, seq_len=seq_len, post=post,
                          n_vec=len(vecs)),
        grid=(m // tm, ncol),
        in_specs=specs,
        out_specs=[pl.BlockSpec((tm, tc), lambda i, j: (i, j)) for _ in range(n_out)],
        out_shape=[jax.ShapeDtypeStruct((m, width), F32) for _ in range(n_out)],
        compiler_params=_cparams("parallel", "parallel"),
        name=name,
    )(*ins)


def _lmm_body(*refs, n_x):
    w_ref, o_ref = refs[0], refs[1 + n_x]
    tc = o_ref.shape[-1]
    xs = [r[...].reshape(-1, tc) for r in refs[1:1 + n_x]]
    x = xs[0] if n_x == 1 else jnp.concatenate(xs, 0)
    w = w_ref[...]
    w = w.reshape(w.shape[-2], w.shape[-1])
    o_ref[...] = _dot(w, x.astype(BF16)).reshape(o_ref.shape).astype(o_ref.dtype)


def _lmm(w, w_spec, xs, x_specs, out_shape, out_spec, grid, name, out_dtype=F32):
    return pl.pallas_call(
        functools.partial(_lmm_body, n_x=len(xs)),
        grid=grid,
        in_specs=[w_spec] + list(x_specs),
        out_specs=out_spec,
        out_shape=jax.ShapeDtypeStruct(out_shape, out_dtype),
        compiler_params=_cparams(*(["parallel"] * len(grid))),
        name=name,
    )(w, *xs)


def _spec_mul_body(w1_ref, w2_ref, x_ref, h_ref, o_ref):
    tc = o_ref.shape[-1]
    x = x_ref[...].reshape(-1, tc)
    z = _dot(w1_ref[...], x.astype(BF16))
    half = z.shape[0] // 2
    zr, zi = z[:half], z[half:]
    hr, hi = h_ref[0], h_ref[1]
    y = jnp.concatenate([zr * hr - zi * hi, zr * hi + zi * hr], 0)
    o_ref[...] = _dot(w2_ref[...], y.astype(BF16)).reshape(o_ref.shape).astype(o_ref.dtype)


def _cplx_mat(ang):
    c, s = jnp.cos(ang), jnp.sin(ang)
    return jnp.concatenate([jnp.concatenate([c, -s], -1), jnp.concatenate([s, c], -1)], -2)


def _phase(k, n):
    return (k % n).astype(F32) * (2.0 * math.pi / n)


def _outer_mats(n1, n2, n_in, sign):
    n = n1 * n2
    s1 = jnp.arange(n1, dtype=jnp.int32)[:, None, None]
    f2 = jnp.arange(n2, dtype=jnp.int32)[None, :, None]
    s2 = jnp.arange(n_in, dtype=jnp.int32)[None, None, :]
    return sign * _phase(f2 * (s1 + n1 * s2), n)


def _inner_phase(n1, sign):
    a = jnp.arange(n1, dtype=jnp.int32)
    return sign * _phase(a[:, None] * a[None, :], n1)


def _split_len(n):
    n1 = 1 << (int(math.log2(n)) // 2)
    return n1, n // n1


def _fft_conv_pair(z2, hspec, n1, n2, tcol=512):
    _, length, c = z2.shape
    n = 2 * length
    assert n1 * n2 == n
    h2 = n2 // 2
    m_in = _cplx_mat(_outer_mats(n1, n2, h2, -1.0)).astype(BF16)
    a = _lmm(m_in, pl.BlockSpec((1, 2 * n2, n2), lambda s: (s, 0, 0)),
             [z2.reshape(2, h2, n1 * c)], [pl.BlockSpec((2, h2, c), lambda s: (0, 0, s))],
             (2, n1, n2, c), pl.BlockSpec((2, 1, n2, c), lambda s: (0, s, 0, 0)), (n1,), "fft_in", BF16)
    w1 = _cplx_mat(_inner_phase(n1, -1.0)).astype(BF16)
    w2 = (_cplx_mat(_inner_phase(n1, 1.0)) * (1.0 / n)).astype(BF16)
    cols = n2 * c
    tcol = min(tcol, cols)
    b = pl.pallas_call(
        _spec_mul_body,
        grid=(cols // tcol,),
        in_specs=[pl.BlockSpec((2 * n1, 2 * n1), lambda j: (0, 0)),
                  pl.BlockSpec((2 * n1, 2 * n1), lambda j: (0, 0)),
                  pl.BlockSpec((2, n1, tcol), lambda j: (0, 0, j)),
                  pl.BlockSpec((2, n1, tcol), lambda j: (0, 0, j))],
        out_specs=pl.BlockSpec((2, n1, tcol), lambda j: (0, 0, j)),
        out_shape=jax.ShapeDtypeStruct((2, n1, cols), BF16),
        compiler_params=_cparams("parallel"),
        name="fft_mid",
    )(w1, w2, a.reshape(2, n1, cols), hspec)
    m_out = _cplx_mat(jnp.swapaxes(_outer_mats(n1, n2, h2, 1.0), 1, 2)).astype(BF16)
    y = _lmm(m_out, pl.BlockSpec((1, n2, 2 * n2), lambda s: (s, 0, 0)),
             [b.reshape(2, n1, n2, c)], [pl.BlockSpec((2, 1, n2, c), lambda s: (0, s, 0, 0))],
             (2, h2, n1 * c), pl.BlockSpec((2, h2, c), lambda s: (0, 0, s)), (n1,), "fft_out")
    return y.reshape(2, length, c)


def _fft_real_spectrum(f, n1, n2, tcol=512):
    n, c = f.shape
    if n2 == 1:
        ph = _inner_phase(n1, -1.0)
        w = jnp.concatenate([jnp.cos(ph), jnp.sin(ph)], 0).astype(BF16)
        tcol = min(tcol, c)
        return _lmm(w, pl.BlockSpec((2 * n1, n1), lambda j: (0, 0)),
                    [f], [pl.BlockSpec((n1, tcol), lambda j: (0, j))],
                    (2, n1, c), pl.BlockSpec((2, n1, tcol), lambda j: (0, 0, j)), (c // tcol,), "fft_spec1")
    ph = _outer_mats(n1, n2, n2, -1.0)
    m_in = jnp.concatenate([jnp.cos(ph), jnp.sin(ph)], 1).astype(BF16)
    a = _lmm(m_in, pl.BlockSpec((1, 2 * n2, n2), lambda s: (s, 0, 0)),
             [f.reshape(n2, n1 * c)], [pl.BlockSpec((n2, c), lambda s: (0, s))],
             (2, n1, n2, c), pl.BlockSpec((2, 1, n2, c), lambda s: (0, s, 0, 0)), (n1,), "fft_spec_in", BF16)
    w1 = _cplx_mat(_inner_phase(n1, -1.0)).astype(BF16)
    cols = n2 * c
    tcol = min(tcol, cols)
    h = _lmm(w1, pl.BlockSpec((2 * n1, 2 * n1), lambda j: (0, 0)),
             [a.reshape(2, n1, cols)], [pl.BlockSpec((2, n1, tcol), lambda j: (0, 0, j))],
             (2, n1, cols), pl.BlockSpec((2, n1, tcol), lambda j: (0, 0, j)), (cols // tcol,), "fft_spec_mid")
    return h


def _hy_filter_body(wt_ref, wc_ref, ws_ref, b1_ref, w2_ref, b2_ref, w3_ref, bands_ref, dl_ref, o_ref,
                    *, length, tr):
    d = o_ref.shape[-1]
    j = pl.program_id(0) * tr + lax.broadcasted_iota(jnp.int32, (tr, 1), 0)
    k = jnp.where(j < length, j, 2 * length - j)
    t = k.astype(F32) / length
    ang = 2.0 * jnp.pi * t * bands_ref[...]
    pre = t * wt_ref[...] + _dot_hi(jnp.cos(ang), wc_ref[...]) + _dot_hi(-jnp.sin(ang), ws_ref[...])
    hdn = jnp.sin(pre + b1_ref[...])
    hdn = jnp.sin(_dot_hi(hdn, w2_ref[...]) + b2_ref[...])
    hf = _dot3(hdn, w3_ref[...])
    win = jnp.exp(-t * dl_ref[...]) + HY_SHIFT
    h = jnp.where(j < length, hf[:, :d], hf[:, d:]) * win
    o_ref[...] = jnp.where(j == length, 0.0, h)


def _hy_filter(length, f_w1, f_b1, f_w2, f_b2, f_w3):
    d = f_w3.shape[1] // 2
    ffn = f_w2.shape[0]
    tr = min(512, length)
    bands = jnp.linspace(1e-4, HY_BANDS - 1, HY_BANDS, dtype=F32).reshape(1, HY_BANDS)
    deltas = jnp.abs(jnp.linspace(HY_MIN_DECAY, HY_MAX_DECAY, d, dtype=F32)).reshape(1, d)
    ins = [f_w1[0:1], f_w1[1:1 + HY_BANDS], f_w1[1 + HY_BANDS:], f_b1.reshape(1, ffn), f_w2,
           f_b2.reshape(1, ffn), f_w3, bands, deltas]
    return pl.pallas_call(
        functools.partial(_hy_filter_body, length=length, tr=tr),
        grid=(2 * length // tr,),
        in_specs=[pl.BlockSpec(a.shape, lambda i: (0, 0)) for a in ins],
        out_specs=pl.BlockSpec((tr, d), lambda i: (i, 0)),
        out_shape=jax.ShapeDtypeStruct((2 * length, d), F32),
        compiler_params=_cparams("parallel"),
        name="hy_filter",
    )(*ins)


def _hyena(x, scale, shift, rows_per_group, length, p, j):
    m, d = x.shape
    assert m == 2 * length
    u = _mm(x, p["hy_in_w"][j].astype(BF16), bias=p["hy_in_b"][j], mod=(scale, shift),
            rows_per_group=rows_per_group)
    post = lambda x0, x1, v: (x0, v * x1)
    x0c, z = _conv3(u, p["hy_conv"][j], [0, d, 2 * d], 512, length, post, 2, name="hy_conv")
    filt = _hy_filter(length, p["hy_f_w1"][j], p["hy_f_b1"][j], p["hy_f_w2"][j], p["hy_f_b2"][j],
                      p["hy_f_w3"][j])
    n = 2 * length
    n1, n2 = (n, 1) if n <= 1024 else _split_len(n)
    hspec = _fft_real_spectrum(filt, n1, n2)
    if n2 == 1:
        z2 = jnp.pad(z.reshape(2, length, d), ((0, 0), (0, length), (0, 0)))
        y = _fft_conv_pair_single(z2, hspec, n)[:, :length]
    else:
        y = _fft_conv_pair(z.reshape(2, length, d), hspec, n1, n2)
    fn = lambda yt, zt, x0t, sk: ((yt + sk * zt) * x0t,)
    g = _rowwise(fn, [y.reshape(m, d), z, x0c], [p["hy_skip"][j].reshape(1, 1, d)], [d], name="hy_gate")[0]
    return _mm(g, p["hy_out_w"][j].astype(BF16), bias=p["hy_out_b"][j])


def _fft_conv_pair_single(z2, hspec, n, tcol=512):
    c = z2.shape[-1]
    w1 = _cplx_mat(_inner_phase(n, -1.0)).astype(BF16)
    w2 = (_cplx_mat(_inner_phase(n, 1.0)) * (1.0 / n)).astype(BF16)
    tcol = min(tcol, c)
    return pl.pallas_call(
        _spec_mul_body,
        grid=(c // tcol,),
        in_specs=[pl.BlockSpec((2 * n, 2 * n), lambda j: (0, 0)),
                  pl.BlockSpec((2 * n, 2 * n), lambda j: (0, 0)),
                  pl.BlockSpec((2, n, tcol), lambda j: (0, 0, j)),
                  pl.BlockSpec((2, n, tcol), lambda j: (0, 0, j))],
        out_specs=pl.BlockSpec((2, n, tcol), lambda j: (0, 0, j)),
        out_shape=jax.ShapeDtypeStruct((2, n, c), F32),
        compiler_params=_cparams("parallel"),
        name="fft_mid1",
    )(w1, w2, z2, hspec)


def _fnet(x, scale, shift, rows_per_group, batch, length, p, j):
    m, d = x.shape
    gc = d // FN_GROUPS
    ph = _inner_phase(gc, -1.0)
    eye = jnp.eye(FN_GROUPS, dtype=F32)
    w_c = jnp.concatenate([jnp.kron(eye, jnp.cos(ph)), jnp.kron(eye, jnp.sin(ph))], 1).astype(BF16)
    w = _mm(x, w_c, mod=(scale, shift), rows_per_group=rows_per_group)
    if length <= 1024:
        n1, n2 = length, 1
    else:
        n1, n2 = _split_len(length)
    norm = 1.0 / math.sqrt(length * gc)
    if n2 == 1:
        ph1 = _inner_phase(n1, -1.0)
        wr = (jnp.concatenate([jnp.cos(ph1), -jnp.sin(ph1)], 1) * norm).astype(BF16)
        y = _lmm(wr, pl.BlockSpec((n1, 2 * n1), lambda b, c: (0, 0)),
                 [w.reshape(batch, n1, 2 * d)] * 2,
                 [pl.BlockSpec((None, n1, d), lambda b, c: (b, 0, 0)),
                  pl.BlockSpec((None, n1, d), lambda b, c: (b, 0, 1))],
                 (batch, n1, d), pl.BlockSpec((None, n1, d), lambda b, c: (b, 0, 0)), (batch, 1), "fn_pos1")
        y = y.reshape(m, d)
    else:
        m_in = _cplx_mat(_outer_mats(n1, n2, n2, -1.0)).astype(BF16)
        wv = w.reshape(batch, n2, n1 * 2 * d)
        a = _lmm(m_in, pl.BlockSpec((1, 2 * n2, 2 * n2), lambda b, s: (s, 0, 0)),
                 [wv, wv],
                 [pl.BlockSpec((None, n2, d), lambda b, s: (b, 0, 2 * s)),
                  pl.BlockSpec((None, n2, d), lambda b, s: (b, 0, 2 * s + 1))],
                 (batch, 2, n1, n2, d), pl.BlockSpec((None, 2, 1, n2, d), lambda b, s: (b, 0, s, 0, 0)),
                 (batch, n1), "fn_pos_in", BF16)
        ph1 = _inner_phase(n1, -1.0)
        wr = (jnp.concatenate([jnp.cos(ph1), -jnp.sin(ph1)], 1) * norm).astype(BF16)
        cols = n2 * d
        tcol = 1024
        y = _lmm(wr, pl.BlockSpec((n1, 2 * n1), lambda b, c: (0, 0)),
                 [a.reshape(batch, 2, n1, cols)],
                 [pl.BlockSpec((None, 2, n1, tcol), lambda b, c: (b, 0, 0, c))],
                 (batch, n1, cols), pl.BlockSpec((None, n1, tcol), lambda b, c: (b, 0, c)),
                 (batch, cols // tcol), "fn_pos_mid")
        y = y.reshape(m, d)
    return _mm(y, p["fn_out_w"][j].astype(BF16), bias=p["fn_out_b"][j])


def _head_l2(t, extra):
    outs = []
    for h in range(t.shape[1] // GD_DK):
        th = t[:, h * GD_DK:(h + 1) * GD_DK]
        outs.append(th * (lax.rsqrt(jnp.sum(th * th, -1, keepdims=True) + 1e-6) * extra))
    return jnp.concatenate(outs, -1)


def _silu(v):
    return v * jax.nn.sigmoid(v)


def _gdn_scan_body(q_ref, k_ref, v_ref, gb_ref, gbt_ref, s0_ref, o_ref, sfin_ref, s_scr, *, n_chunks):
    direction = pl.program_id(0)
    c = pl.program_id(1)
    cs = GD_CHUNK
    nb = q_ref.shape[0]

    @pl.when(c == 0)
    def _():
        s_scr[...] = s0_ref[...]

    ri = lax.broadcasted_iota(jnp.int32, (cs, cs), 0)
    ci = lax.broadcasted_iota(jnp.int32, (cs, cs), 1)
    lag = (ri - ci) * (1 - 2 * direction)
    incl = lag >= 0
    strict = lag > 0
    tri = incl.astype(F32)
    tri_t = (lag <= 0).astype(F32)
    eye = (ri == ci).astype(F32)
    pair_masks = []
    for lvl in range(int(math.log2(cs))):
        rb, cb = lax.shift_right_logical(ri, lvl), lax.shift_right_logical(ci, lvl)
        pair_masks.append((jnp.abs(rb - cb) == 1) & ((jnp.minimum(rb, cb) & 1) == 0))
    gb = [gb_ref[b] for b in range(nb)]
    gc_cols = [_dot_hi(tri, gb[b]) for b in range(nb)]
    gc_rows = [_dot_hi(gbt_ref[b], tri_t) for b in range(nb)]
    tot = [jnp.sum(gb[b], 0, keepdims=True) for b in range(nb)]
    ch = [(b, h) for b in range(nb) for h in range(GD_HEADS)]
    hs = range(len(ch))
    nt = (((1,), (1,)), ((), ()))
    tn = (((0,), (0,)), ((), ()))
    sl = [slice(h * GD_DK, (h + 1) * GD_DK) for _, h in ch]
    k = [k_ref[b, :, sl[i]] for i, (b, h) in enumerate(ch)]
    gcol = [gc_cols[b][:, h:h + 1] for b, h in ch]
    beta = [gb[b][:, GD_HEADS + h:GD_HEADS + h + 1] for b, h in ch]
    gtot = [tot[b][:, h:h + 1] for b, h in ch]
    decay = [jnp.exp(jnp.where(incl, gcol[i] - gc_rows[b][h:h + 1, :], -jnp.inf)) for i, (b, h) in enumerate(ch)]
    eg = [jnp.exp(gcol[h]) for h in hs]
    kb = [k[h] * beta[h] for h in hs]
    kbf = [k[h].astype(BF16) for h in hs]
    kk = [lax.dot_general(kb[h].astype(BF16), kbf[h], nt, preferred_element_type=F32) for h in hs]
    a = [jnp.where(strict, kk[h] * decay[h], 0.0) for h in hs]
    inv = [eye - jnp.where(pair_masks[0], a[h], 0.0) for h in hs]
    for pm in pair_masks[1:]:
        tn_s = [_dot3(inv[h], jnp.where(pm, a[h], 0.0)) for h in hs]
        tnt = [_dot3(tn_s[h], inv[h]) for h in hs]
        inv = [inv[h] - tnt[h] for h in hs]
    rhs = [jnp.concatenate([v_ref[b, :, sl[i]] * beta[i], kb[i] * eg[i]], -1) for i, (b, h) in enumerate(ch)]
    sol = [_dot3(inv[h], rhs[h]) for h in hs]
    q = [q_ref[b, :, sl[i]] for i, (b, h) in enumerate(ch)]
    qk = [lax.dot_general(q[h].astype(BF16), kbf[h], nt, preferred_element_type=F32) for h in hs]
    sb = [s_scr[b, h].astype(BF16) for b, h in ch]
    ws = [_dot(sol[h][:, GD_DK:].astype(BF16), sb[h]) for h in hs]
    qs = [_dot((q[h] * eg[h]).astype(BF16), sb[h]) for h in hs]
    vnb = [(sol[h][:, :GD_DK] - ws[h]).astype(BF16) for h in hs]
    av = [_dot((qk[h] * decay[h]).astype(BF16), vnb[h]) for h in hs]
    kv = [lax.dot_general((k[h] * jnp.exp(gtot[h] - gcol[h])).astype(BF16), vnb[h], tn,
                          preferred_element_type=F32) for h in hs]
    for i, (b, h) in enumerate(ch):
        s_scr[b, h] = s_scr[b, h] * jnp.exp(gtot[i]) + kv[i]
    for b in range(nb):
        o_ref[b] = jnp.concatenate([qs[i] + av[i] for i, (bb, _) in enumerate(ch) if bb == b], -1)

    @pl.when(c == n_chunks - 1)
    def _():
        sfin_ref[...] = s_scr[...]


def _gdn_scan(q, k, v, gb, gbt, s0):
    b, length, w = q.shape
    n_chunks = length // GD_CHUNK
    cidx = lambda d, c: c + d * (n_chunks - 1 - 2 * c)
    seq = lambda d, c: (0, cidx(d, c), 0)
    state = pl.BlockSpec((None, b, GD_HEADS, GD_DK, GD_DK), lambda d, c: (d, 0, 0, 0, 0))
    return pl.pallas_call(
        functools.partial(_gdn_scan_body, n_chunks=n_chunks),
        grid=(2, n_chunks),
        in_specs=[pl.BlockSpec((b, GD_CHUNK, w), seq),
                  pl.BlockSpec((b, GD_CHUNK, w), seq),
                  pl.BlockSpec((b, GD_CHUNK, w), seq),
                  pl.BlockSpec((None, b, GD_CHUNK, 128), lambda d, c: (d, 0, cidx(d, c), 0)),
                  pl.BlockSpec((None, b, None, 16, GD_CHUNK), lambda d, c: (d, 0, cidx(d, c), 0, 0)),
                  state],
        out_specs=[pl.BlockSpec((None, b, GD_CHUNK, w), lambda d, c: (d, 0, cidx(d, c), 0)), state],
        out_shape=[jax.ShapeDtypeStruct((2, b, length, w), F32),
                   jax.ShapeDtypeStruct((2, b, GD_HEADS, GD_DK, GD_DK), F32)],
        scratch_shapes=[pltpu.VMEM((b, GD_HEADS, GD_DK, GD_DK), F32)],
        compiler_params=_cparams("parallel", "arbitrary"),
        name="gdn_scan",
    )(q, k, v, gb, gbt, s0)


def _gdn_inputs(x, scale, shift, rows_per_group, batch, length, p, j):
    m, d = x.shape
    wd = GD_HEADS * GD_DK
    in_w = p["gd_in_w"][j]
    u = _mm(x, in_w[:, :4 * wd].astype(BF16), mod=(scale, shift), rows_per_group=rows_per_group)
    w_ab = jnp.pad(in_w[:, 4 * wd:], ((0, 0), (0, 128 - 4 * GD_HEADS))).astype(BF16)
    ab = _mm(x, w_ab, mod=(scale, shift), rows_per_group=rows_per_group)
    qscale = GD_DK ** -0.5
    post = lambda qc, kc, vc: (_head_l2(_silu(qc), qscale), _head_l2(_silu(kc), 1.0), _silu(vc))
    q, k, v = _conv3(u, p["gd_conv"][j], [0, wd, 2 * wd], 512, length, post, 3, name="gd_conv")
    nh = GD_HEADS
    a_par = jnp.zeros((1, 1, 128), F32).at[0, 0, :2 * nh].set(-jnp.exp(p["gd_a_log"][j]).reshape(-1))
    dt_par = jnp.zeros((1, 1, 128), F32).at[0, 0, :2 * nh].set(p["gd_dt_bias"][j].reshape(-1))

    def gate_fn(abt, an, dtb):
        pre = abt + dtb
        sp = jnp.maximum(pre, 0.0) + jnp.log(1.0 + jnp.exp(-jnp.abs(pre)))
        lane = lax.broadcasted_iota(jnp.int32, abt.shape, 1)
        return (jnp.where(lane < 2 * nh, an * sp, jax.nn.sigmoid(abt)),)

    gall = _rowwise(gate_fn, [ab], [a_par, dt_par], [128], name="gd_gate")[0]
    pad = jnp.zeros((m, 128 - 2 * nh), F32)
    gb = jnp.stack([jnp.concatenate([gall[:, dr * nh:(dr + 1) * nh],
                                     gall[:, (2 + dr) * nh:(3 + dr) * nh], pad], -1) for dr in range(2)])
    gb = gb.reshape(2, batch, length, 128)
    gbt = jnp.swapaxes(gb[..., :2 * nh].reshape(2, batch, length // GD_CHUNK, GD_CHUNK, 2 * nh), -1, -2)
    rs = lambda t: t.reshape(batch, length, wd)
    return rs(q), rs(k), rs(v), gb, gbt, u


def _gdn_out(o2, u, norm_g, out_w):
    m, wd = o2.shape[1], o2.shape[2]
    ng = jnp.tile(norm_g, wd // norm_g.shape[0]).reshape(1, 1, wd)

    def body(of_ref, ob_ref, z_ref, ng_ref, o_ref):
        o = of_ref[...] + ob_ref[...]
        outs = []
        for h in range(wd // GD_DK):
            oh = o[:, h * GD_DK:(h + 1) * GD_DK]
            outs.append(oh * lax.rsqrt(jnp.mean(oh * oh, -1, keepdims=True) + 1e-6))
        o_ref[...] = jnp.concatenate(outs, -1) * ng_ref[...] * _silu(z_ref[...])

    tm = math.gcd(512, m)
    g = pl.pallas_call(
        body,
        grid=(m // tm,),
        in_specs=[pl.BlockSpec((None, tm, wd), lambda i: (0, i, 0)),
                  pl.BlockSpec((None, tm, wd), lambda i: (1, i, 0)),
                  pl.BlockSpec((tm, wd), lambda i: (i, 3)),
                  pl.BlockSpec((None, 1, wd), lambda i: (0, 0, 0))],
        out_specs=pl.BlockSpec((tm, wd), lambda i: (i, 0)),
        out_shape=jax.ShapeDtypeStruct((m, wd), F32),
        compiler_params=_cparams("parallel"),
        name="gd_norm",
    )(o2, o2, u, ng)
    return _mm(g, out_w.astype(BF16))


def _topk_rows(s, n_take, val_ref, idx_ref, base, rid=None):
    big = jnp.iinfo(jnp.int32).max
    rows_in_order = rid is None
    if rows_in_order:
        rid = lax.broadcasted_iota(jnp.int32, s.shape, 0)
    for t in range(n_take):
        mx = jnp.max(s, 0, keepdims=True)
        if rows_in_order:
            first = jnp.full((8, s.shape[1]), big, jnp.int32)
            for j in reversed(range(s.shape[0] // 8)):
                first = jnp.where(s[8 * j:8 * j + 8] == mx, rid[8 * j:8 * j + 8], first)
            am = jnp.min(first, 0, keepdims=True)
        else:
            am = jnp.min(jnp.where(s == mx, rid, big), 0, keepdims=True)
        val_ref[base + t:base + t + 1, :] = mx
        idx_ref[base + t:base + t + 1, :] = am
        s = jnp.where(rid == am, -jnp.inf, s)


def _pk_topk_body(q_ref, keys_ref, i1_ref, i2_ref, gate_ref, sv_scr, si_scr, cv_scr, ci_scr,
                  i1p_scr, i2p_scr, gp_scr):
    kk = PK_TOPK
    tt = q_ref.shape[0]
    for h in range(PK_HEADS):
        for p in range(2):
            qh = q_ref[:, (2 * h + p) * PK_DH:(2 * h + p + 1) * PK_DH].astype(BF16)
            st = lax.dot_general(keys_ref[h, p], qh, (((1,), (1,)), ((), ())),
                                 preferred_element_type=F32)
            _topk_rows(st, kk, sv_scr, si_scr, p * kk)
        sv1, sv2 = sv_scr[0:kk, :], sv_scr[kk:2 * kk, :]
        si1, si2 = si_scr[0:kk, :], si_scr[kk:2 * kk, :]
        r8 = lax.broadcasted_iota(jnp.int32, (8, tt), 0)
        cand = jnp.concatenate([sv1[0:8, :] + sv2[r2:r2 + 1, :] for r2 in range(8)]
                               + [sv1[8:kk, :] + sv2[0:1, :], sv1[0:1, :] + sv2[8:kk, :]], 0)
        cid = jnp.concatenate([r8 * kk + r2 for r2 in range(8)] + [(r8 + 8) * kk, r8 + 8], 0)
        _topk_rows(cand, kk, cv_scr, ci_scr, 0, cid)
        cv, ci = cv_scr[...], ci_scr[...]
        a1, a2 = lax.shift_right_logical(ci, int(math.log2(kk))), ci & (kk - 1)
        i1 = jnp.zeros((kk, tt), jnp.int32)
        i2 = jnp.zeros((kk, tt), jnp.int32)
        for r in range(kk):
            i1 = i1 + jnp.where(a1 == r, si1[r:r + 1, :], 0)
            i2 = i2 + jnp.where(a2 == r, si2[r:r + 1, :], 0)
        i1p_scr[h * kk:(h + 1) * kk, :] = i1
        i2p_scr[h * kk:(h + 1) * kk, :] = i2
        e = jnp.exp(cv - jnp.max(cv, 0, keepdims=True))
        gp_scr[h * kk:(h + 1) * kk, :] = e / jnp.sum(e, 0, keepdims=True)
    i1_ref[...] = i1p_scr[...].T
    i2_ref[...] = i2p_scr[...].T
    gate_ref[...] = gp_scr[...].T


def _pk_topk(q, keys_bf16):
    m = q.shape[0]
    tt = PK_TOK
    hk = PK_HEADS * PK_TOPK
    assert hk == tt
    return pl.pallas_call(
        _pk_topk_body,
        grid=(m // tt,),
        in_specs=[pl.BlockSpec((tt, q.shape[1]), lambda i: (i, 0)),
                  pl.BlockSpec(keys_bf16.shape, lambda i: (0, 0, 0, 0))],
        out_specs=[pl.BlockSpec((tt, hk), lambda i: (i, 0))] * 3,
        out_shape=[jax.ShapeDtypeStruct((m, hk), jnp.int32), jax.ShapeDtypeStruct((m, hk), jnp.int32),
                   jax.ShapeDtypeStruct((m, hk), F32)],
        scratch_shapes=[pltpu.VMEM((2 * PK_TOPK, tt), F32), pltpu.VMEM((2 * PK_TOPK, tt), jnp.int32),
                        pltpu.VMEM((PK_TOPK, tt), F32), pltpu.VMEM((PK_TOPK, tt), jnp.int32),
                        pltpu.VMEM((hk, tt), jnp.int32), pltpu.VMEM((hk, tt), jnp.int32),
                        pltpu.VMEM((hk, tt), F32)],
        compiler_params=_cparams("parallel"),
        name="pk_topk",
    )(q, keys_bf16)


def _pk_score_body(x_ref, sc_ref, sh_ref, i1_ref, i2_ref, ut_ref, act_ref, hb_scr):
    e = pl.program_id(1)
    nk = PK_NKEYS
    per = ut_ref.shape[1] // nk

    @pl.when(e == 0)
    def _():
        hb_scr[...] = (x_ref[...] * (1.0 + sc_ref[...]) + sh_ref[...]).astype(BF16)
        act_ref[...] = jnp.zeros_like(act_ref)

    hb = hb_scr[...]
    i1 = i1_ref[...]
    i2 = i2_ref[...]
    act = act_ref[...]
    for kp in range(per // 2):
        s = _dot(hb, ut_ref[:, 2 * kp * nk:2 * (kp + 1) * nk])
        for k in (2 * kp, 2 * kp + 1):
            got = jnp.take_along_axis(s[:, (k % 2) * nk:(k % 2 + 1) * nk], i2, axis=1)
            act = jnp.where(i1 == e * per + k, got, act)
    act_ref[...] = act


def _pk_score(x, scale, shift, i1, i2, ut_bf16, rows_per_group):
    m, d = x.shape
    n_exp = ut_bf16.shape[1]
    tm = math.gcd(PK_TM_SCORE, m, rows_per_group)
    g = rows_per_group // tm
    assert n_exp % PK_EC == 0 and PK_EC % (2 * PK_NKEYS) == 0
    row = lambda i, e: (i, 0)
    vec = lambda i, e: (i // g, 0, 0)
    npk = i1.shape[1]
    return pl.pallas_call(
        _pk_score_body,
        grid=(m // tm, n_exp // PK_EC),
        in_specs=[pl.BlockSpec((tm, d), row), pl.BlockSpec((None, 1, d), vec), pl.BlockSpec((None, 1, d), vec),
                  pl.BlockSpec((tm, npk), row), pl.BlockSpec((tm, npk), row),
                  pl.BlockSpec((d, PK_EC), lambda i, e: (0, e))],
        out_specs=pl.BlockSpec((tm, npk), row),
        out_shape=jax.ShapeDtypeStruct((m, npk), F32),
        scratch_shapes=[pltpu.VMEM((tm, d), BF16)],
        compiler_params=_cparams("parallel", "arbitrary"),
        name="pk_score",
    )(x, scale, shift, i1, i2, ut_bf16)


def _pk_combine_body(x_ref, gt_ref, lg_ref, lb_ref, i1_ref, i2_ref, gate_ref, act_ref, v_ref,
                     o_ref, w_scr, grid_scr, acc_scr, *, alpha, n_chunks):
    e = pl.program_id(1)
    tm = x_ref.shape[0]
    nk = PK_NKEYS
    per = v_ref.shape[0] // nk

    @pl.when(e == 0)
    def _():
        w_scr[...] = jax.nn.gelu(act_ref[...]) * gate_ref[...]
        acc_scr[...] = jnp.zeros_like(acc_scr)
        sub = lax.broadcasted_iota(jnp.int32, (nk, nk), 0)
        key1 = jnp.where(sub < nk // 2, 2 * sub, 2 * sub - (nk - 1))

        def tok(tg, carry):
            for t0 in range(0, PK_UNROLL, 8):
                words = []
                for u in range(8):
                    t = tg * PK_UNROLL + t0 + u
                    wrow = w_scr[pl.ds(t, 1), :]
                    at = jnp.where(key1 == i1_ref[pl.ds(t, 1), :], wrow, 0.0).astype(BF16)
                    bt = (sub == i2_ref[pl.ds(t, 1), :]).astype(BF16)
                    g = lax.dot_general(at, bt, (((1,), (1,)), ((), ())), preferred_element_type=F32)
                    g = lax.bitcast_convert_type(g.astype(BF16).astype(F32), jnp.int32)
                    words.append(lax.shift_right_logical(g[:nk // 2], 16) | (g[nk // 2:] & jnp.int32(-65536)))
                first = pl.multiple_of(tg * PK_UNROLL + t0, 8)
                grid_scr[:, pl.ds(first, 8), :] = jnp.swapaxes(jnp.stack(words, 0), 0, 1)
            return carry

        lax.fori_loop(0, tm // PK_UNROLL, tok, 0)

    parts = []
    for kp in range(per // 2):
        wd = grid_scr[e * (per // 2) + kp]
        parts.append(lax.bitcast_convert_type(lax.shift_left(wd, 16), F32))
        parts.append(lax.bitcast_convert_type(wd & jnp.int32(-65536), F32))
    acc_scr[...] += _dot(jnp.concatenate(parts, -1).astype(BF16), v_ref[...])

    @pl.when(e == n_chunks - 1)
    def _():
        o_ref[...] = _ln(alpha * x_ref[...] + gt_ref[...] * acc_scr[...], lg_ref[...], lb_ref[...])


def _pk_combine(x, gate_vec, ln_g, ln_b, i1, i2, gate, act, v_bf16, rows_per_group, alpha):
    m, d = x.shape
    n_exp = v_bf16.shape[0]
    tm = math.gcd(PK_TM, m, rows_per_group)
    n_chunks = n_exp // PK_EC
    g = rows_per_group // tm
    assert n_exp % PK_EC == 0 and PK_EC % (2 * PK_NKEYS) == 0 and tm % PK_UNROLL == 0
    row = lambda i, e: (i, 0)
    vec = lambda i, e: (i // g, 0, 0)
    one = lambda i, e: (0, 0, 0)
    npk = i1.shape[1]
    return pl.pallas_call(
        functools.partial(_pk_combine_body, alpha=alpha, n_chunks=n_chunks),
        grid=(m // tm, n_chunks),
        in_specs=[pl.BlockSpec((tm, d), row), pl.BlockSpec((None, 1, d), vec),
                  pl.BlockSpec((None, 1, d), one), pl.BlockSpec((None, 1, d), one),
                  pl.BlockSpec((tm, npk), row), pl.BlockSpec((tm, npk), row), pl.BlockSpec((tm, npk), row),
                  pl.BlockSpec((tm, npk), row),
                  pl.BlockSpec((PK_EC, d), lambda i, e: (e, 0))],
        out_specs=pl.BlockSpec((tm, d), row),
        out_shape=jax.ShapeDtypeStruct((m, d), F32),
        scratch_shapes=[pltpu.VMEM((tm, npk), F32), pltpu.VMEM((PK_NKEYS // 2, tm, PK_NKEYS), jnp.int32),
                        pltpu.VMEM((tm, d), F32)],
        compiler_params=_cparams("parallel", "arbitrary"),
        name="pk_combine",
    )(x, gate_vec, ln_g, ln_b, i1, i2, gate, act, v_bf16)


def _peer_ln(x, scale, shift, gate_vec, ln_g, ln_b, rows_per_group, alpha, wq_bf16, keys_bf16, ut_bf16, v_bf16):
    q = _mm(x, wq_bf16, mod=(scale, shift), rows_per_group=rows_per_group)
    i1, i2, gate = _pk_topk(q, keys_bf16)
    act = _pk_score(x, scale, shift, i1, i2, ut_bf16, rows_per_group)
    return _pk_combine(x, gate_vec, ln_g, ln_b, i1, i2, gate, act, v_bf16, rows_per_group, alpha)


def kernel(x, c, ctx, c_ctx, ada_w, ada_b, ln_g, ln_b, pk_wq, pk_keys, pk_u, pk_v, hy_in_w, hy_in_b, hy_conv, hy_f_w1, hy_f_b1, hy_f_w2, hy_f_b2, hy_f_w3, hy_skip, hy_out_w, hy_out_b, gd_in_w, gd_conv, gd_a_log, gd_dt_bias, gd_norm_g, gd_out_w, fn_out_w, fn_out_b):
    p = dict(hy_in_w=hy_in_w, hy_in_b=hy_in_b, hy_conv=hy_conv, hy_f_w1=hy_f_w1, hy_f_b1=hy_f_b1,
             hy_f_w2=hy_f_w2, hy_f_b2=hy_f_b2, hy_f_w3=hy_f_w3, hy_skip=hy_skip, hy_out_w=hy_out_w,
             hy_out_b=hy_out_b, gd_in_w=gd_in_w, gd_conv=gd_conv, gd_a_log=gd_a_log, gd_dt_bias=gd_dt_bias,
             fn_out_w=fn_out_w, fn_out_b=fn_out_b)
    b, length, d = x.shape
    lc = ctx.shape[1]
    depth = ada_w.shape[0]
    alpha = (2 * depth) ** 0.25
    xl = _pos_add(x)
    xc = ctx.reshape(b * lc, d)
    gdn_layers = [i for i in range(depth) if i % N_MIXERS == 1]
    ctx_until = gdn_layers[-1] if gdn_layers else -1
    cond = jnp.concatenate([c, c_ctx[None], jnp.zeros((8 - b - 1, d), F32)], 0)
    for i in range(depth):
        kind, j = i % N_MIXERS, i // N_MIXERS
        ctx_in, ctx_out = i <= ctx_until, i < ctx_until
        mod = _mm(cond, ada_w[i].astype(BF16), bias=ada_b[i], silu_in=True, tn=2048)
        mod = mod.reshape(8, N_MOD, 1, d)
        ml = [mod[:b, t] for t in range(N_MOD)]
        mc = [mod[b:b + 1, t] for t in range(N_MOD)]
        lg = [ln_g[i, t].reshape(1, 1, d) for t in range(2)]
        lb = [ln_b[i, t].reshape(1, 1, d) for t in range(2)]
        yc = None
        if kind == 0:
            yl = _hyena(xl, ml[1], ml[0], length, length, p, j)
            if ctx_out:
                yc = _hyena(xc, mc[1], mc[0], b * lc, lc, p, j)
        elif kind == 1:
            qc, kc, vc, gbc, gbtc, uc = _gdn_inputs(xc, mc[1], mc[0], b * lc, b, lc, p, j)
            ql, kl, vl, gbl, gbtl, ul = _gdn_inputs(xl, ml[1], ml[0], length, b, length, p, j)
            s0 = jnp.zeros((2, b, GD_HEADS, GD_DK, GD_DK), F32)
            oc, s_ctx = _gdn_scan(qc, kc, vc, gbc, gbtc, s0)
            ol, _ = _gdn_scan(ql, kl, vl, gbl, gbtl, s_ctx)
            yl = _gdn_out(ol.reshape(2, b * length, -1), ul, gd_norm_g[j], gd_out_w[j])
            if ctx_out:
                yc = _gdn_out(oc.reshape(2, b * lc, -1), uc, gd_norm_g[j], gd_out_w[j])
        else:
            yl = _fnet(xl, ml[1], ml[0], length, b, length, p, j)
            if ctx_out:
                yc = _fnet(xc, mc[1], mc[0], b * lc, b, lc, p, j)
        wq = pk_wq[i].astype(BF16)
        keys = pk_keys[i].astype(BF16)
        ut, vt = pk_u[i].astype(BF16).T, pk_v[i].astype(BF16)
        xl = _res_ln(xl, yl, ml[2], lg[0], lb[0], alpha, length)
        xl = _peer_ln(xl, ml[4], ml[3], ml[5], lg[1], lb[1], length, alpha, wq, keys, ut, vt)
        if ctx_out:
            xc = _res_ln(xc, yc, mc[2], lg[0], lb[0], alpha, b * lc)
            xc = _peer_ln(xc, mc[4], mc[3], mc[5], lg[1], lb[1], b * lc, alpha, wq, keys, ut, vt)
    return xl.reshape(b, length, d)
```

```python
import functools
import math

import numpy as np
import jax
import jax.numpy as jnp
from jax import lax
from jax.experimental import pallas as pl
from jax.experimental.pallas import tpu as pltpu

F32 = jnp.float32
BF16 = jnp.bfloat16

GRID_W = 64
N_MIXERS = 3
N_MOD = 6
LN_EPS = 1e-5
HY_EMB = 33
HY_BANDS = (HY_EMB - 1) // 2
HY_SHIFT = 0.05
HY_TARGET = 1e-2
HY_MIN_DECAY = math.log(HY_TARGET) / 1.5
HY_MAX_DECAY = math.log(HY_TARGET) / 0.3
GD_HEADS = 8
GD_DK = 128
GD_CHUNK = 64
FN_GROUPS = 4
PK_HEADS = 8
PK_NKEYS = 128
PK_DH = 128
PK_TOPK = 16
PK_TOK = 128
PK_TM_SCORE = 1024
PK_TM = 512
PK_EC = 4096
PK_UNROLL = 32

VMEM_LIMIT_BYTES = 56 * 1024 * 1024


def _cparams(*sem):
    return pltpu.CompilerParams(dimension_semantics=sem, vmem_limit_bytes=VMEM_LIMIT_BYTES)


def _dot(a, b):
    return jnp.dot(a, b, preferred_element_type=F32)


def _dot_hi(a, b):
    return jnp.dot(a, b, preferred_element_type=F32, precision=lax.Precision.HIGHEST)


def _split(a):
    hi = a.astype(BF16)
    lo = (a - hi.astype(F32)).astype(BF16)
    return hi, lo


def _dot3(a, b):
    ah, al = _split(a)
    bh, bl = _split(b)
    return _dot(ah, bh) + (_dot(ah, bl) + _dot(al, bh))


def _mm_body(*refs, has_mod, has_bias, silu_in):
    a_ref, w_ref = refs[0], refs[1]
    k = 2
    a = a_ref[...]
    if has_mod:
        a = a * (1.0 + refs[k][...]) + refs[k + 1][...]
        k += 2
    if silu_in:
        a = a * jax.nn.sigmoid(a)
    o = _dot(a.astype(BF16), w_ref[...])
    if has_bias:
        o = o + refs[k][...]
        k += 1
    refs[k][...] = o


def _mm(a, w_bf16, bias=None, mod=None, rows_per_group=None, silu_in=False, tm=1024, tn=None):
    m, k = a.shape
    n = w_bf16.shape[1]
    tm = math.gcd(tm, m, rows_per_group or m)
    if tn is None:
        tn = max(t for t in range(128, min(n, 2048) + 1, 128) if n % t == 0)
    assert m % tm == 0 and n % tn == 0
    ins = [a, w_bf16]
    specs = [pl.BlockSpec((tm, k), lambda j, i: (i, 0)), pl.BlockSpec((k, tn), lambda j, i: (0, j))]
    if mod is not None:
        assert rows_per_group % tm == 0
        g = rows_per_group // tm
        for v in mod:
            ins.append(v)
            specs.append(pl.BlockSpec((None, 1, k), lambda j, i: (i // g, 0, 0)))
    if bias is not None:
        ins.append(bias.reshape(1, n))
        specs.append(pl.BlockSpec((1, tn), lambda j, i: (0, j)))
    return pl.pallas_call(
        functools.partial(_mm_body, has_mod=mod is not None, has_bias=bias is not None, silu_in=silu_in),
        grid=(n // tn, m // tm),
        in_specs=specs,
        out_specs=pl.BlockSpec((tm, tn), lambda j, i: (i, j)),
        out_shape=jax.ShapeDtypeStruct((m, n), F32),
        compiler_params=_cparams("parallel", "parallel"),
        name="mm",
    )(*ins)


def _rowwise(fn, rows, vecs, out_cols, rows_per_group=None, tm=512, name="rowwise"):
    m = rows[0].shape[0]
    tm = math.gcd(tm, m, rows_per_group or m)
    assert m % tm == 0
    n_r, n_v, n_o = len(rows), len(vecs), len(out_cols)

    def body(*refs):
        outs = fn(*[r[...] for r in refs[:n_r + n_v]])
        for o_ref, o in zip(refs[n_r + n_v:], outs):
            o_ref[...] = o

    specs = [pl.BlockSpec((tm, r.shape[1]), lambda i: (i, 0)) for r in rows]
    for v in vecs:
        if v.shape[0] == 1:
            specs.append(pl.BlockSpec((None, 1, v.shape[2]), lambda i: (0, 0, 0)))
        else:
            assert rows_per_group % tm == 0
            g = rows_per_group // tm
            specs.append(pl.BlockSpec((None, 1, v.shape[2]), lambda i, g=g: (i // g, 0, 0)))
    return pl.pallas_call(
        body,
        grid=(m // tm,),
        in_specs=specs,
        out_specs=[pl.BlockSpec((tm, c), lambda i: (i, 0)) for c in out_cols],
        out_shape=[jax.ShapeDtypeStruct((m, c), F32) for c in out_cols],
        compiler_params=_cparams("parallel"),
        name=name,
    )(*rows, *vecs)


def _ln(v, g, b):
    mu = jnp.mean(v, -1, keepdims=True)
    d = v - mu
    var = jnp.mean(d * d, -1, keepdims=True)
    return d * lax.rsqrt(var + LN_EPS) * g + b


def _res_ln(x, y, gate, ln_g, ln_b, alpha, rows_per_group):
    fn = lambda xt, yt, gt, lg, lb: (_ln(alpha * xt + gt * yt, lg, lb),)
    return _rowwise(fn, [x, y], [gate, ln_g, ln_b], [x.shape[1]], rows_per_group, name="res_ln")[0]


def _pos_add_body(x_ref, er_ref, ec_ref, o_ref):
    half = er_ref.shape[-1]
    x = x_ref[...]
    er = jnp.broadcast_to(er_ref[...], x.shape[:2] + (half,))
    ec = jnp.broadcast_to(ec_ref[...][None], x.shape[:2] + (half,))
    o_ref[...] = x + jnp.concatenate([er, ec], -1)


def _pos_add(x):
    b, length, d = x.shape
    rows = length // GRID_W
    quarter = d // 4
    omega = 1.0 / (10000.0 ** (jnp.arange(quarter, dtype=F32) / quarter))
    er = jnp.arange(rows, dtype=F32)[:, None] * omega
    ec = jnp.arange(GRID_W, dtype=F32)[:, None] * omega
    emb_r = jnp.concatenate([jnp.sin(er), jnp.cos(er)], -1).reshape(rows, 1, d // 2)
    emb_c = jnp.concatenate([jnp.sin(ec), jnp.cos(ec)], -1)
    rt = 8
    x4 = x.reshape(b, rows, GRID_W, d)
    out = pl.pallas_call(
        _pos_add_body,
        grid=(b, rows // rt),
        in_specs=[pl.BlockSpec((None, rt, GRID_W, d), lambda i, j: (i, j, 0, 0)),
                  pl.BlockSpec((rt, 1, d // 2), lambda i, j: (j, 0, 0)),
                  pl.BlockSpec((GRID_W, d // 2), lambda i, j: (0, 0))],
        out_specs=pl.BlockSpec((None, rt, GRID_W, d), lambda i, j: (i, j, 0, 0)),
        out_shape=jax.ShapeDtypeStruct(x4.shape, F32),
        compiler_params=_cparams("parallel", "parallel"),
        name="pos_add",
    )(x4, emb_r, emb_c)
    return out.reshape(b * length, d)


def _conv3_body(*refs, n_parts, tm, seq_len, post, n_vec):
    i = pl.program_id(0)
    first = (i * tm) % seq_len == 0
    last = ((i + 1) * tm) % seq_len == 0
    row = lax.broadcasted_iota(jnp.int32, (tm, 1), 0)
    parts = []
    for p in range(n_parts):
        main_ref, prev_ref, next_ref, w_ref = refs[4 * p:4 * p + 4]
        u = main_ref[...]
        w = w_ref[...]
        prev = jnp.where(first, 0.0, prev_ref[7:8, :])
        nxt = jnp.where(last, 0.0, next_ref[0:1, :])
        up = jnp.where(row == 0, prev, pltpu.roll(u, 1, 0))
        dn = jnp.where(row == tm - 1, nxt, pltpu.roll(u, tm - 1, 0))
        parts.append(up * w[0:1, :] + u * w[1:2, :] + dn * w[2:3, :])
    k = 4 * n_parts
    vecs = [refs[k + j][...] for j in range(n_vec)]
    outs = post(*parts, *vecs)
    for o_ref, o in zip(refs[k + n_vec:], outs):
        o_ref[...] = o


def _conv3(u, w, col_parts, tc, seq_len, post, n_out, vecs=(), tm=256, name="conv3"):
    m = u.shape[0]
    tm = min(tm, seq_len)
    assert seq_len % tm == 0 and m % tm == 0 and tm % 8 == 0
    width = col_parts[1] - col_parts[0] if len(col_parts) > 1 else tc
    ncol = width // tc
    t8 = tm // 8
    nb8 = m // 8
    ins, specs = [], []
    for c0 in col_parts:
        cb = c0 // tc
        ins += [u, u, u, w]
        specs += [
            pl.BlockSpec((tm, tc), lambda i, j, cb=cb: (i, cb + j)),
            pl.BlockSpec((8, tc), lambda i, j, cb=cb: (jnp.maximum(i * t8 - 1, 0), cb + j)),
            pl.BlockSpec((8, tc), lambda i, j, cb=cb: (jnp.minimum((i + 1) * t8, nb8 - 1), cb + j)),
            pl.BlockSpec((3, tc), lambda i, j, cb=cb: (0, cb + j)),
        ]
    for v in vecs:
        ins.append(v)
        specs.append(pl.BlockSpec((1, tc), lambda i, j: (0, j)))
    return pl.pallas_call(
        functools.partial(_conv3_body, n_parts=len(col_parts), tm=tm, seq_len=seq_len, post=post,
                          n_vec=len(vecs)),
        grid=(m // tm, ncol),
        in_specs=specs,
        out_specs=[pl.BlockSpec((tm, tc), lambda i, j: (i, j)) for _ in range(n_out)],
        out_shape=[jax.ShapeDtypeStruct((m, width), F32) for _ in range(n_out)],
        compiler_params=_cparams("parallel", "parallel"),
        name=name,
    )(*ins)


def _lmm_body(*refs, n_x):
    w_ref, o_ref = refs[0], refs[1 + n_x]
    tc = o_ref.shape[-1]
    xs = [r[...].reshape(-1, tc) for r in refs[1:1 + n_x]]
    x = xs[0] if n_x == 1 else jnp.concatenate(xs, 0)
    w = w_ref[...]
    w = w.reshape(w.shape[-2], w.shape[-1])
    o_ref[...] = _dot(w, x.astype(BF16)).reshape(o_ref.shape).astype(o_ref.dtype)


def _lmm(w, w_spec, xs, x_specs, out_shape, out_spec, grid, name, out_dtype=F32):
    return pl.pallas_call(
        functools.partial(_lmm_body, n_x=len(xs)),
        grid=grid,
        in_specs=[w_spec] + list(x_specs),
        out_specs=out_spec,
        out_shape=jax.ShapeDtypeStruct(out_shape, out_dtype),
        compiler_params=_cparams(*(["parallel"] * len(grid))),
        name=name,
    )(w, *xs)


def _spec_mul_body(w1_ref, w2_ref, x_ref, h_ref, o_ref):
    tc = o_ref.shape[-1]
    x = x_ref[...].reshape(-1, tc)
    z = _dot(w1_ref[...], x.astype(BF16))
    half = z.shape[0] // 2
    zr, zi = z[:half], z[half:]
    hr, hi = h_ref[0], h_ref[1]
    y = jnp.concatenate([zr * hr - zi * hi, zr * hi + zi * hr], 0)
    o_ref[...] = _dot(w2_ref[...], y.astype(BF16)).reshape(o_ref.shape).astype(o_ref.dtype)


def _cplx_mat(ang):
    c, s = jnp.cos(ang), jnp.sin(ang)
    return jnp.concatenate([jnp.concatenate([c, -s], -1), jnp.concatenate([s, c], -1)], -2)


def _phase(k, n):
    return (k % n).astype(F32) * (2.0 * math.pi / n)


def _outer_mats(n1, n2, n_in, sign):
    n = n1 * n2
    s1 = jnp.arange(n1, dtype=jnp.int32)[:, None, None]
    f2 = jnp.arange(n2, dtype=jnp.int32)[None, :, None]
    s2 = jnp.arange(n_in, dtype=jnp.int32)[None, None, :]
    return sign * _phase(f2 * (s1 + n1 * s2), n)


def _inner_phase(n1, sign):
    a = jnp.arange(n1, dtype=jnp.int32)
    return sign * _phase(a[:, None] * a[None, :], n1)


def _split_len(n):
    n1 = 1 << (int(math.log2(n)) // 2)
    return n1, n // n1


def _fft_conv_pair(z2, hspec, n1, n2, tcol=512):
    _, length, c = z2.shape
    n = 2 * length
    assert n1 * n2 == n
    h2 = n2 // 2
    m_in = _cplx_mat(_outer_mats(n1, n2, h2, -1.0)).astype(BF16)
    a = _lmm(m_in, pl.BlockSpec((1, 2 * n2, n2), lambda s: (s, 0, 0)),
             [z2.reshape(2, h2, n1 * c)], [pl.BlockSpec((2, h2, c), lambda s: (0, 0, s))],
             (2, n1, n2, c), pl.BlockSpec((2, 1, n2, c), lambda s: (0, s, 0, 0)), (n1,), "fft_in", BF16)
    w1 = _cplx_mat(_inner_phase(n1, -1.0)).astype(BF16)
    w2 = (_cplx_mat(_inner_phase(n1, 1.0)) * (1.0 / n)).astype(BF16)
    cols = n2 * c
    tcol = min(tcol, cols)
    b = pl.pallas_call(
        _spec_mul_body,
        grid=(cols // tcol,),
        in_specs=[pl.BlockSpec((2 * n1, 2 * n1), lambda j: (0, 0)),
                  pl.BlockSpec((2 * n1, 2 * n1), lambda j: (0, 0)),
                  pl.BlockSpec((2, n1, tcol), lambda j: (0, 0, j)),
                  pl.BlockSpec((2, n1, tcol), lambda j: (0, 0, j))],
        out_specs=pl.BlockSpec((2, n1, tcol), lambda j: (0, 0, j)),
        out_shape=jax.ShapeDtypeStruct((2, n1, cols), BF16),
        compiler_params=_cparams("parallel"),
        name="fft_mid",
    )(w1, w2, a.reshape(2, n1, cols), hspec)
    m_out = _cplx_mat(jnp.swapaxes(_outer_mats(n1, n2, h2, 1.0), 1, 2)).astype(BF16)
    y = _lmm(m_out, pl.BlockSpec((1, n2, 2 * n2), lambda s: (s, 0, 0)),
             [b.reshape(2, n1, n2, c)], [pl.BlockSpec((2, 1, n2, c), lambda s: (0, s, 0, 0))],
             (2, h2, n1 * c), pl.BlockSpec((2, h2, c), lambda s: (0, 0, s)), (n1,), "fft_out")
    return y.reshape(2, length, c)


def _fft_real_spectrum(f, n1, n2, tcol=512):
    n, c = f.shape
    if n2 == 1:
        ph = _inner_phase(n1, -1.0)
        w = jnp.concatenate([jnp.cos(ph), jnp.sin(ph)], 0).astype(BF16)
        tcol = min(tcol, c)
        return _lmm(w, pl.BlockSpec((2 * n1, n1), lambda j: (0, 0)),
                    [f], [pl.BlockSpec((n1, tcol), lambda j: (0, j))],
                    (2, n1, c), pl.BlockSpec((2, n1, tcol), lambda j: (0, 0, j)), (c // tcol,), "fft_spec1")
    ph = _outer_mats(n1, n2, n2, -1.0)
    m_in = jnp.concatenate([jnp.cos(ph), jnp.sin(ph)], 1).astype(BF16)
    a = _lmm(m_in, pl.BlockSpec((1, 2 * n2, n2), lambda s: (s, 0, 0)),
             [f.reshape(n2, n1 * c)], [pl.BlockSpec((n2, c), lambda s: (0, s))],
             (2, n1, n2, c), pl.BlockSpec((2, 1, n2, c), lambda s: (0, s, 0, 0)), (n1,), "fft_spec_in", BF16)
    w1 = _cplx_mat(_inner_phase(n1, -1.0)).astype(BF16)
    cols = n2 * c
    tcol = min(tcol, cols)
    h = _lmm(w1, pl.BlockSpec((2 * n1, 2 * n1), lambda j: (0, 0)),
             [a.reshape(2, n1, cols)], [pl.BlockSpec((2, n1, tcol), lambda j: (0, 0, j))],
             (2, n1, cols), pl.BlockSpec((2, n1, tcol), lambda j: (0, 0, j)), (cols // tcol,), "fft_spec_mid")
    return h


def _hy_filter_body(wt_ref, wc_ref, ws_ref, b1_ref, w2_ref, b2_ref, w3_ref, bands_ref, dl_ref, o_ref,
                    *, length, tr):
    d = o_ref.shape[-1]
    j = pl.program_id(0) * tr + lax.broadcasted_iota(jnp.int32, (tr, 1), 0)
    k = jnp.where(j < length, j, 2 * length - j)
    t = k.astype(F32) / length
    ang = 2.0 * jnp.pi * t * bands_ref[...]
    pre = t * wt_ref[...] + _dot_hi(jnp.cos(ang), wc_ref[...]) + _dot_hi(-jnp.sin(ang), ws_ref[...])
    hdn = jnp.sin(pre + b1_ref[...])
    hdn = jnp.sin(_dot_hi(hdn, w2_ref[...]) + b2_ref[...])
    hf = _dot3(hdn, w3_ref[...])
    win = jnp.exp(-t * dl_ref[...]) + HY_SHIFT
    h = jnp.where(j < length, hf[:, :d], hf[:, d:]) * win
    o_ref[...] = jnp.where(j == length, 0.0, h)


def _hy_filter(length, f_w1, f_b1, f_w2, f_b2, f_w3):
    d = f_w3.shape[1] // 2
    ffn = f_w2.shape[0]
    tr = min(512, length)
    bands = jnp.linspace(1e-4, HY_BANDS - 1, HY_BANDS, dtype=F32).reshape(1, HY_BANDS)
    deltas = jnp.abs(jnp.linspace(HY_MIN_DECAY, HY_MAX_DECAY, d, dtype=F32)).reshape(1, d)
    ins = [f_w1[0:1], f_w1[1:1 + HY_BANDS], f_w1[1 + HY_BANDS:], f_b1.reshape(1, ffn), f_w2,
           f_b2.reshape(1, ffn), f_w3, bands, deltas]
    return pl.pallas_call(
        functools.partial(_hy_filter_body, length=length, tr=tr),
        grid=(2 * length // tr,),
        in_specs=[pl.BlockSpec(a.shape, lambda i: (0, 0)) for a in ins],
        out_specs=pl.BlockSpec((tr, d), lambda i: (i, 0)),
        out_shape=jax.ShapeDtypeStruct((2 * length, d), F32),
        compiler_params=_cparams("parallel"),
        name="hy_filter",
    )(*ins)


def _hyena(x, scale, shift, rows_per_group, length, p, j):
    m, d = x.shape
    assert m == 2 * length
    u = _mm(x, p["hy_in_w"][j].astype(BF16), bias=p["hy_in_b"][j], mod=(scale, shift),
            rows_per_group=rows_per_group)
    post = lambda x0, x1, v: (x0, v * x1)
    x0c, z = _conv3(u, p["hy_conv"][j], [0, d, 2 * d], 512, length, post, 2, name="hy_conv")
    filt = _hy_filter(length, p["hy_f_w1"][j], p["hy_f_b1"][j], p["hy_f_w2"][j], p["hy_f_b2"][j],
                      p["hy_f_w3"][j])
    n = 2 * length
    n1, n2 = (n, 1) if n <= 1024 else _split_len(n)
    hspec = _fft_real_spectrum(filt, n1, n2)
    if n2 == 1:
        z2 = jnp.pad(z.reshape(2, length, d), ((0, 0), (0, length), (0, 0)))
        y = _fft_conv_pair_single(z2, hspec, n)[:, :length]
    else:
        y = _fft_conv_pair(z.reshape(2, length, d), hspec, n1, n2)
    fn = lambda yt, zt, x0t, sk: ((yt + sk * zt) * x0t,)
    g = _rowwise(fn, [y.reshape(m, d), z, x0c], [p["hy_skip"][j].reshape(1, 1, d)], [d], name="hy_gate")[0]
    return _mm(g, p["hy_out_w"][j].astype(BF16), bias=p["hy_out_b"][j])


def _fft_conv_pair_single(z2, hspec, n, tcol=512):
    c = z2.shape[-1]
    w1 = _cplx_mat(_inner_phase(n, -1.0)).astype(BF16)
    w2 = (_cplx_mat(_inner_phase(n, 1.0)) * (1.0 / n)).astype(BF16)
    tcol = min(tcol, c)
    return pl.pallas_call(
        _spec_mul_body,
        grid=(c // tcol,),
        in_specs=[pl.BlockSpec((2 * n, 2 * n), lambda j: (0, 0)),
                  pl.BlockSpec((2 * n, 2 * n), lambda j: (0, 0)),
                  pl.BlockSpec((2, n, tcol), lambda j: (0, 0, j)),
                  pl.BlockSpec((2, n, tcol), lambda j: (0, 0, j))],
        out_specs=pl.BlockSpec((2, n, tcol), lambda j: (0, 0, j)),
        out_shape=jax.ShapeDtypeStruct((2, n, c), F32),
        compiler_params=_cparams("parallel"),
        name="fft_mid1",
    )(w1, w2, z2, hspec)


def _fnet(x, scale, shift, rows_per_group, batch, length, p, j):
    m, d = x.shape
    gc = d // FN_GROUPS
    ph = _inner_phase(gc, -1.0)
    eye = jnp.eye(FN_GROUPS, dtype=F32)
    w_c = jnp.concatenate([jnp.kron(eye, jnp.cos(ph)), jnp.kron(eye, jnp.sin(ph))], 1).astype(BF16)
    w = _mm(x, w_c, mod=(scale, shift), rows_per_group=rows_per_group)
    if length <= 1024:
        n1, n2 = length, 1
    else:
        n1, n2 = _split_len(length)
    norm = 1.0 / math.sqrt(length * gc)
    if n2 == 1:
        ph1 = _inner_phase(n1, -1.0)
        wr = (jnp.concatenate([jnp.cos(ph1), -jnp.sin(ph1)], 1) * norm).astype(BF16)
        y = _lmm(wr, pl.BlockSpec((n1, 2 * n1), lambda b, c: (0, 0)),
                 [w.reshape(batch, n1, 2 * d)] * 2,
                 [pl.BlockSpec((None, n1, d), lambda b, c: (b, 0, 0)),
                  pl.BlockSpec((None, n1, d), lambda b, c: (b, 0, 1))],
                 (batch, n1, d), pl.BlockSpec((None, n1, d), lambda b, c: (b, 0, 0)), (batch, 1), "fn_pos1")
        y = y.reshape(m, d)
    else:
        m_in = _cplx_mat(_outer_mats(n1, n2, n2, -1.0)).astype(BF16)
        wv = w.reshape(batch, n2, n1 * 2 * d)
        a = _lmm(m_in, pl.BlockSpec((1, 2 * n2, 2 * n2), lambda b, s: (s, 0, 0)),
                 [wv, wv],
                 [pl.BlockSpec((None, n2, d), lambda b, s: (b, 0, 2 * s)),
                  pl.BlockSpec((None, n2, d), lambda b, s: (b, 0, 2 * s + 1))],
                 (batch, 2, n1, n2, d), pl.BlockSpec((None, 2, 1, n2, d), lambda b, s: (b, 0, s, 0, 0)),
                 (batch, n1), "fn_pos_in", BF16)
        ph1 = _inner_phase(n1, -1.0)
        wr = (jnp.concatenate([jnp.cos(ph1), -jnp.sin(ph1)], 1) * norm).astype(BF16)
        cols = n2 * d
        tcol = 1024
        y = _lmm(wr, pl.BlockSpec((n1, 2 * n1), lambda b, c: (0, 0)),
                 [a.reshape(batch, 2, n1, cols)],
                 [pl.BlockSpec((None, 2, n1, tcol), lambda b, c: (b, 0, 0, c))],
                 (batch, n1, cols), pl.BlockSpec((None, n1, tcol), lambda b, c: (b, 0, c)),
                 (batch, cols // tcol), "fn_pos_mid")
        y = y.reshape(m, d)
    return _mm(y, p["fn_out_w"][j].astype(BF16), bias=p["fn_out_b"][j])


def _head_l2(t, extra):
    outs = []
    for h in range(t.shape[1] // GD_DK):
        th = t[:, h * GD_DK:(h + 1) * GD_DK]
        outs.append(th * (lax.rsqrt(jnp.sum(th * th, -1, keepdims=True) + 1e-6) * extra))
    return jnp.concatenate(outs, -1)


def _silu(v):
    return v * jax.nn.sigmoid(v)


def _gdn_scan_body(q_ref, k_ref, v_ref, gb_ref, gbt_ref, s0_ref, o_ref, sfin_ref, s_scr, *, n_chunks):
    direction = pl.program_id(0)
    c = pl.program_id(1)
    cs = GD_CHUNK
    nb = q_ref.shape[0]

    @pl.when(c == 0)
    def _():
        s_scr[...] = s0_ref[...]

    ri = lax.broadcasted_iota(jnp.int32, (cs, cs), 0)
    ci = lax.broadcasted_iota(jnp.int32, (cs, cs), 1)
    lag = (ri - ci) * (1 - 2 * direction)
    incl = lag >= 0
    strict = lag > 0
    tri = incl.astype(F32)
    tri_t = (lag <= 0).astype(F32)
    eye = (ri == ci).astype(F32)
    pair_masks = []
    for lvl in range(int(math.log2(cs))):
        rb, cb = lax.shift_right_logical(ri, lvl), lax.shift_right_logical(ci, lvl)
        pair_masks.append((jnp.abs(rb - cb) == 1) & ((jnp.minimum(rb, cb) & 1) == 0))
    gb = [gb_ref[b] for b in range(nb)]
    gc_cols = [_dot_hi(tri, gb[b]) for b in range(nb)]
    gc_rows = [_dot_hi(gbt_ref[b], tri_t) for b in range(nb)]
    tot = [jnp.sum(gb[b], 0, keepdims=True) for b in range(nb)]
    ch = [(b, h) for b in range(nb) for h in range(GD_HEADS)]
    hs = range(len(ch))
    nt = (((1,), (1,)), ((), ()))
    tn = (((0,), (0,)), ((), ()))
    sl = [slice(h * GD_DK, (h + 1) * GD_DK) for _, h in ch]
    k = [k_ref[b, :, sl[i]] for i, (b, h) in enumerate(ch)]
    gcol = [gc_cols[b][:, h:h + 1] for b, h in ch]
    beta = [gb[b][:, GD_HEADS + h:GD_HEADS + h + 1] for b, h in ch]
    gtot = [tot[b][:, h:h + 1] for b, h in ch]
    decay = [jnp.exp(jnp.where(incl, gcol[i] - gc_rows[b][h:h + 1, :], -jnp.inf)) for i, (b, h) in enumerate(ch)]
    eg = [jnp.exp(gcol[h]) for h in hs]
    kb = [k[h] * beta[h] for h in hs]
    kbf = [k[h].astype(BF16) for h in hs]
    kk = [lax.dot_general(kb[h].astype(BF16), kbf[h], nt, preferred_element_type=F32) for h in hs]
    a = [jnp.where(strict, kk[h] * decay[h], 0.0) for h in hs]
    inv = [eye - jnp.where(pair_masks[0], a[h], 0.0) for h in hs]
    for pm in pair_masks[1:]:
        tn_s = [_dot3(inv[h], jnp.where(pm, a[h], 0.0)) for h in hs]
        tnt = [_dot3(tn_s[h], inv[h]) for h in hs]
        inv = [inv[h] - tnt[h] for h in hs]
    rhs = [jnp.concatenate([v_ref[b, :, sl[i]] * beta[i], kb[i] * eg[i]], -1) for i, (b, h) in enumerate(ch)]
    sol = [_dot3(inv[h], rhs[h]) for h in hs]
    q = [q_ref[b, :, sl[i]] for i, (b, h) in enumerate(ch)]
    qk = [lax.dot_general(q[h].astype(BF16), kbf[h], nt, preferred_element_type=F32) for h in hs]
    sb = [s_scr[b, h].astype(BF16) for b, h in ch]
    ws = [_dot(sol[h][:, GD_DK:].astype(BF16), sb[h]) for h in hs]
    qs = [_dot((q[h] * eg[h]).astype(BF16), sb[h]) for h in hs]
    vnb = [(sol[h][:, :GD_DK] - ws[h]).astype(BF16) for h in hs]
    av = [_dot((qk[h] * decay[h]).astype(BF16), vnb[h]) for h in hs]
    kv = [lax.dot_general((k[h] * jnp.exp(gtot[h] - gcol[h])).astype(BF16), vnb[h], tn,
                          preferred_element_type=F32) for h in hs]
    for i, (b, h) in enumerate(ch):
        s_scr[b, h] = s_scr[b, h] * jnp.exp(gtot[i]) + kv[i]
    for b in range(nb):
        o_ref[b] = jnp.concatenate([qs[i] + av[i] for i, (bb, _) in enumerate(ch) if bb == b], -1)

    @pl.when(c == n_chunks - 1)
    def _():
        sfin_ref[...] = s_scr[...]


def _gdn_scan(q, k, v, gb, gbt, s0):
    b, length, w = q.shape
    n_chunks = length // GD_CHUNK
    cidx = lambda d, c: c + d * (n_chunks - 1 - 2 * c)
    seq = lambda d, c: (0, cidx(d, c), 0)
    state = pl.BlockSpec((None, b, GD_HEADS, GD_DK, GD_DK), lambda d, c: (d, 0, 0, 0, 0))
    return pl.pallas_call(
        functools.partial(_gdn_scan_body, n_chunks=n_chunks),
        grid=(2, n_chunks),
        in_specs=[pl.BlockSpec((b, GD_CHUNK, w), seq),
                  pl.BlockSpec((b, GD_CHUNK, w), seq),
                  pl.BlockSpec((b, GD_CHUNK, w), seq),
                  pl.BlockSpec((None, b, GD_CHUNK, 128), lambda d, c: (d, 0, cidx(d, c), 0)),
                  pl.BlockSpec((None, b, None, 16, GD_CHUNK), lambda d, c: (d, 0, cidx(d, c), 0, 0)),
                  state],
        out_specs=[pl.BlockSpec((None, b, GD_CHUNK, w), lambda d, c: (d, 0, cidx(d, c), 0)), state],
        out_shape=[jax.ShapeDtypeStruct((2, b, length, w), F32),
                   jax.ShapeDtypeStruct((2, b, GD_HEADS, GD_DK, GD_DK), F32)],
        scratch_shapes=[pltpu.VMEM((b, GD_HEADS, GD_DK, GD_DK), F32)],
        compiler_params=_cparams("parallel", "arbitrary"),
        name="gdn_scan",
    )(q, k, v, gb, gbt, s0)


def _gdn_inputs(x, scale, shift, rows_per_group, batch, length, p, j):
    m, d = x.shape
    wd = GD_HEADS * GD_DK
    in_w = p["gd_in_w"][j]
    u = _mm(x, in_w[:, :4 * wd].astype(BF16), mod=(scale, shift), rows_per_group=rows_per_group)
    w_ab = jnp.pad(in_w[:, 4 * wd:], ((0, 0), (0, 128 - 4 * GD_HEADS))).astype(BF16)
    ab = _mm(x, w_ab, mod=(scale, shift), rows_per_group=rows_per_group)
    qscale = GD_DK ** -0.5
    post = lambda qc, kc, vc: (_head_l2(_silu(qc), qscale), _head_l2(_silu(kc), 1.0), _silu(vc))
    q, k, v = _conv3(u, p["gd_conv"][j], [0, wd, 2 * wd], 512, length, post, 3, name="gd_conv")
    nh = GD_HEADS
    a_par = jnp.zeros((1, 1, 128), F32).at[0, 0, :2 * nh].set(-jnp.exp(p["gd_a_log"][j]).reshape(-1))
    dt_par = jnp.zeros((1, 1, 128), F32).at[0, 0, :2 * nh].set(p["gd_dt_bias"][j].reshape(-1))

    def gate_fn(abt, an, dtb):
        pre = abt + dtb
        sp = jnp.maximum(pre, 0.0) + jnp.log(1.0 + jnp.exp(-jnp.abs(pre)))
        lane = lax.broadcasted_iota(jnp.int32, abt.shape, 1)
        return (jnp.where(lane < 2 * nh, an * sp, jax.nn.sigmoid(abt)),)

    gall = _rowwise(gate_fn, [ab], [a_par, dt_par], [128], name="gd_gate")[0]
    pad = jnp.zeros((m, 128 - 2 * nh), F32)
    gb = jnp.stack([jnp.concatenate([gall[:, dr * nh:(dr + 1) * nh],
                                     gall[:, (2 + dr) * nh:(3 + dr) * nh], pad], -1) for dr in range(2)])
    gb = gb.reshape(2, batch, length, 128)
    gbt = jnp.swapaxes(gb[..., :2 * nh].reshape(2, batch, length // GD_CHUNK, GD_CHUNK, 2 * nh), -1, -2)
    rs = lambda t: t.reshape(batch, length, wd)
    return rs(q), rs(k), rs(v), gb, gbt, u


def _gdn_out(o2, u, norm_g, out_w):
    m, wd = o2.shape[1], o2.shape[2]
    ng = jnp.tile(norm_g, wd // norm_g.shape[0]).reshape(1, 1, wd)

    def body(of_ref, ob_ref, z_ref, ng_ref, o_ref):
        o = of_ref[...] + ob_ref[...]
        outs = []
        for h in range(wd // GD_DK):
            oh = o[:, h * GD_DK:(h + 1) * GD_DK]
            outs.append(oh * lax.rsqrt(jnp.mean(oh * oh, -1, keepdims=True) + 1e-6))
        o_ref[...] = jnp.concatenate(outs, -1) * ng_ref[...] * _silu(z_ref[...])

    tm = math.gcd(512, m)
    g = pl.pallas_call(
        body,
        grid=(m // tm,),
        in_specs=[pl.BlockSpec((None, tm, wd), lambda i: (0, i, 0)),
                  pl.BlockSpec((None, tm, wd), lambda i: (1, i, 0)),
                  pl.BlockSpec((tm, wd), lambda i: (i, 3)),
                  pl.BlockSpec((None, 1, wd), lambda i: (0, 0, 0))],
        out_specs=pl.BlockSpec((tm, wd), lambda i: (i, 0)),
        out_shape=jax.ShapeDtypeStruct((m, wd), F32),
        compiler_params=_cparams("parallel"),
        name="gd_norm",
    )(o2, o2, u, ng)
    return _mm(g, out_w.astype(BF16))


def _topk_rows(s, n_take, val_ref, idx_ref, base, rid=None):
    big = jnp.iinfo(jnp.int32).max
    rows_in_order = rid is None
    if rows_in_order:
        rid = lax.broadcasted_iota(jnp.int32, s.shape, 0)
    for t in range(n_take):
        mx = jnp.max(s, 0, keepdims=True)
        if rows_in_order:
            first = jnp.full((8, s.shape[1]), big, jnp.int32)
            for j in reversed(range(s.shape[0] // 8)):
                first = jnp.where(s[8 * j:8 * j + 8] == mx, rid[8 * j:8 * j + 8], first)
            am = jnp.min(first, 0, keepdims=True)
        else:
            am = jnp.min(jnp.where(s == mx, rid, big), 0, keepdims=True)
        val_ref[base + t:base + t + 1, :] = mx
        idx_ref[base + t:base + t + 1, :] = am
        s = jnp.where(rid == am, -jnp.inf, s)


def _pk_topk_body(q_ref, keys_ref, i1_ref, i2_ref, gate_ref, sv_scr, si_scr, cv_scr, ci_scr,
                  i1p_scr, i2p_scr, gp_scr):
    kk = PK_TOPK
    tt = q_ref.shape[0]
    for h in range(PK_HEADS):
        for p in range(2):
            qh = q_ref[:, (2 * h + p) * PK_DH:(2 * h + p + 1) * PK_DH].astype(BF16)
            st = lax.dot_general(keys_ref[h, p], qh, (((1,), (1,)), ((), ())),
                                 preferred_element_type=F32)
            _topk_rows(st, kk, sv_scr, si_scr, p * kk)
        sv1, sv2 = sv_scr[0:kk, :], sv_scr[kk:2 * kk, :]
        si1, si2 = si_scr[0:kk, :], si_scr[kk:2 * kk, :]
        r8 = lax.broadcasted_iota(jnp.int32, (8, tt), 0)
        cand = jnp.concatenate([sv1[0:8, :] + sv2[r2:r2 + 1, :] for r2 in range(8)]
                               + [sv1[8:kk, :] + sv2[0:1, :], sv1[0:1, :] + sv2[8:kk, :]], 0)
        cid = jnp.concatenate([r8 * kk + r2 for r2 in range(8)] + [(r8 + 8) * kk, r8 + 8], 0)
        _topk_rows(cand, kk, cv_scr, ci_scr, 0, cid)
        cv, ci = cv_scr[...], ci_scr[...]
        a1, a2 = lax.shift_right_logical(ci, int(math.log2(kk))), ci & (kk - 1)
        i1 = jnp.zeros((kk, tt), jnp.int32)
        i2 = jnp.zeros((kk, tt), jnp.int32)
        for r in range(kk):
            i1 = i1 + jnp.where(a1 == r, si1[r:r + 1, :], 0)
            i2 = i2 + jnp.where(a2 == r, si2[r:r + 1, :], 0)
        i1p_scr[h * kk:(h + 1) * kk, :] = i1
        i2p_scr[h * kk:(h + 1) * kk, :] = i2
        e = jnp.exp(cv - jnp.max(cv, 0, keepdims=True))
        gp_scr[h * kk:(h + 1) * kk, :] = e / jnp.sum(e, 0, keepdims=True)
    i1_ref[...] = i1p_scr[...].T
    i2_ref[...] = i2p_scr[...].T
    gate_ref[...] = gp_scr[...].T


def _pk_topk(q, keys_bf16):
    m = q.shape[0]
    tt = PK_TOK
    hk = PK_HEADS * PK_TOPK
    assert hk == tt
    return pl.pallas_call(
        _pk_topk_body,
        grid=(m // tt,),
        in_specs=[pl.BlockSpec((tt, q.shape[1]), lambda i: (i, 0)),
                  pl.BlockSpec(keys_bf16.shape, lambda i: (0, 0, 0, 0))],
        out_specs=[pl.BlockSpec((tt, hk), lambda i: (i, 0))] * 3,
        out_shape=[jax.ShapeDtypeStruct((m, hk), jnp.int32), jax.ShapeDtypeStruct((m, hk), jnp.int32),
                   jax.ShapeDtypeStruct((m, hk), F32)],
        scratch_shapes=[pltpu.VMEM((2 * PK_TOPK, tt), F32), pltpu.VMEM((2 * PK_TOPK, tt), jnp.int32),
                        pltpu.VMEM((PK_TOPK, tt), F32), pltpu.VMEM((PK_TOPK, tt), jnp.int32),
                        pltpu.VMEM((hk, tt), jnp.int32), pltpu.VMEM((hk, tt), jnp.int32),
                        pltpu.VMEM((hk, tt), F32)],
        compiler_params=_cparams("parallel"),
        name="pk_topk",
    )(q, keys_bf16)


def _pk_score_body(x_ref, sc_ref, sh_ref, i1_ref, i2_ref, ut_ref, act_ref, hb_scr):
    e = pl.program_id(1)
    nk = PK_NKEYS
    per = ut_ref.shape[1] // nk

    @pl.when(e == 0)
    def _():
        hb_scr[...] = (x_ref[...] * (1.0 + sc_ref[...]) + sh_ref[...]).astype(BF16)
        act_ref[...] = jnp.zeros_like(act_ref)

    hb = hb_scr[...]
    i1 = i1_ref[...]
    i2 = i2_ref[...]
    act = act_ref[...]
    for kp in range(per // 2):
        s = _dot(hb, ut_ref[:, 2 * kp * nk:2 * (kp + 1) * nk])
        for k in (2 * kp, 2 * kp + 1):
            got = jnp.take_along_axis(s[:, (k % 2) * nk:(k % 2 + 1) * nk], i2, axis=1)
            act = jnp.where(i1 == e * per + k, got, act)
    act_ref[...] = act


def _pk_score(x, scale, shift, i1, i2, ut_bf16, rows_per_group):
    m, d = x.shape
    n_exp = ut_bf16.shape[1]
    tm = math.gcd(PK_TM_SCORE, m, rows_per_group)
    g = rows_per_group // tm
    assert n_exp % PK_EC == 0 and PK_EC % (2 * PK_NKEYS) == 0
    row = lambda i, e: (i, 0)
    vec = lambda i, e: (i // g, 0, 0)
    npk = i1.shape[1]
    return pl.pallas_call(
        _pk_score_body,
        grid=(m // tm, n_exp // PK_EC),
        in_specs=[pl.BlockSpec((tm, d), row), pl.BlockSpec((None, 1, d), vec), pl.BlockSpec((None, 1, d), vec),
                  pl.BlockSpec((tm, npk), row), pl.BlockSpec((tm, npk), row),
                  pl.BlockSpec((d, PK_EC), lambda i, e: (0, e))],
        out_specs=pl.BlockSpec((tm, npk), row),
        out_shape=jax.ShapeDtypeStruct((m, npk), F32),
        scratch_shapes=[pltpu.VMEM((tm, d), BF16)],
        compiler_params=_cparams("parallel", "arbitrary"),
        name="pk_score",
    )(x, scale, shift, i1, i2, ut_bf16)


def _pk_combine_body(x_ref, gt_ref, lg_ref, lb_ref, i1_ref, i2_ref, gate_ref, act_ref, v_ref,
                     o_ref, w_scr, grid_scr, acc_scr, *, alpha, n_chunks):
    e = pl.program_id(1)
    tm = x_ref.shape[0]
    nk = PK_NKEYS
    per = v_ref.shape[0] // nk

    @pl.when(e == 0)
    def _():
        w_scr[...] = jax.nn.gelu(act_ref[...]) * gate_ref[...]
        acc_scr[...] = jnp.zeros_like(acc_scr)
        sub = lax.broadcasted_iota(jnp.int32, (nk, nk), 0)
        key1 = jnp.where(sub < nk // 2, 2 * sub, 2 * sub - (nk - 1))

        def tok(tg, carry):
            for t0 in range(0, PK_UNROLL, 8):
                words = []
                for u in range(8):
                    t = tg * PK_UNROLL + t0 + u
                    wrow = w_scr[pl.ds(t, 1), :]
                    at = jnp.where(key1 == i1_ref[pl.ds(t, 1), :], wrow, 0.0).astype(BF16)
                    bt = (sub == i2_ref[pl.ds(t, 1), :]).astype(BF16)
                    g = lax.dot_general(at, bt, (((1,), (1,)), ((), ())), preferred_element_type=F32)
                    g = lax.bitcast_convert_type(g.astype(BF16).astype(F32), jnp.int32)
                    words.append(lax.shift_right_logical(g[:nk // 2], 16) | (g[nk // 2:] & jnp.int32(-65536)))
                first = pl.multiple_of(tg * PK_UNROLL + t0, 8)
                grid_scr[:, pl.ds(first, 8), :] = jnp.swapaxes(jnp.stack(words, 0), 0, 1)
            return carry

        lax.fori_loop(0, tm // PK_UNROLL, tok, 0)

    parts = []
    for kp in range(per // 2):
        wd = grid_scr[e * (per // 2) + kp]
        parts.append(lax.bitcast_convert_type(lax.shift_left(wd, 16), F32).astype(BF16))
        parts.append(lax.bitcast_convert_type(wd & jnp.int32(-65536), F32).astype(BF16))
    acc_scr[...] += _dot(jnp.concatenate(parts, -1), v_ref[...])

    @pl.when(e == n_chunks - 1)
    def _():
        o_ref[...] = _ln(alpha * x_ref[...] + gt_ref[...] * acc_scr[...], lg_ref[...], lb_ref[...])


def _pk_combine(x, gate_vec, ln_g, ln_b, i1, i2, gate, act, v_bf16, rows_per_group, alpha):
    m, d = x.shape
    n_exp = v_bf16.shape[0]
    tm = math.gcd(PK_TM, m, rows_per_group)
    n_chunks = n_exp // PK_EC
    g = rows_per_group // tm
    assert n_exp % PK_EC == 0 and PK_EC % (2 * PK_NKEYS) == 0 and tm % PK_UNROLL == 0
    row = lambda i, e: (i, 0)
    vec = lambda i, e: (i // g, 0, 0)
    one = lambda i, e: (0, 0, 0)
    npk = i1.shape[1]
    return pl.pallas_call(
        functools.partial(_pk_combine_body, alpha=alpha, n_chunks=n_chunks),
        grid=(m // tm, n_chunks),
        in_specs=[pl.BlockSpec((tm, d), row), pl.BlockSpec((None, 1, d), vec),
                  pl.BlockSpec((None, 1, d), one), pl.BlockSpec((None, 1, d), one),
                  pl.BlockSpec((tm, npk), row), pl.BlockSpec((tm, npk), row), pl.BlockSpec((tm, npk), row),
                  pl.BlockSpec((tm, npk), row),
                  pl.BlockSpec((PK_EC, d), lambda i, e: (e, 0))],
        out_specs=pl.BlockSpec((tm, d), row),
        out_shape=jax.ShapeDtypeStruct((m, d), F32),
        scratch_shapes=[pltpu.VMEM((tm, npk), F32), pltpu.VMEM((PK_NKEYS // 2, tm, PK_NKEYS), jnp.int32),
                        pltpu.VMEM((tm, d), F32)],
        compiler_params=_cparams("parallel", "arbitrary"),
        name="pk_combine",
    )(x, gate_vec, ln_g, ln_b, i1, i2, gate, act, v_bf16)


def _peer_ln(x, scale, shift, gate_vec, ln_g, ln_b, rows_per_group, alpha, wq_bf16, keys_bf16, ut_bf16, v_bf16):
    q = _mm(x, wq_bf16, mod=(scale, shift), rows_per_group=rows_per_group)
    i1, i2, gate = _pk_topk(q, keys_bf16)
    act = _pk_score(x, scale, shift, i1, i2, ut_bf16, rows_per_group)
    return _pk_combine(x, gate_vec, ln_g, ln_b, i1, i2, gate, act, v_bf16, rows_per_group, alpha)


def kernel(x, c, ctx, c_ctx, ada_w, ada_b, ln_g, ln_b, pk_wq, pk_keys, pk_u, pk_v, hy_in_w, hy_in_b, hy_conv, hy_f_w1, hy_f_b1, hy_f_w2, hy_f_b2, hy_f_w3, hy_skip, hy_out_w, hy_out_b, gd_in_w, gd_conv, gd_a_log, gd_dt_bias, gd_norm_g, gd_out_w, fn_out_w, fn_out_b):
    p = dict(hy_in_w=hy_in_w, hy_in_b=hy_in_b, hy_conv=hy_conv, hy_f_w1=hy_f_w1, hy_f_b1=hy_f_b1,
             hy_f_w2=hy_f_w2, hy_f_b2=hy_f_b2, hy_f_w3=hy_f_w3, hy_skip=hy_skip, hy_out_w=hy_out_w,
             hy_out_b=hy_out_b, gd_in_w=gd_in_w, gd_conv=gd_conv, gd_a_log=gd_a_log, gd_dt_bias=gd_dt_bias,
             fn_out_w=fn_out_w, fn_out_b=fn_out_b)
    b, length, d = x.shape
    lc = ctx.shape[1]
    depth = ada_w.shape[0]
    alpha = (2 * depth) ** 0.25
    xl = _pos_add(x)
    xc = ctx.reshape(b * lc, d)
    gdn_layers = [i for i in range(depth) if i % N_MIXERS == 1]
    ctx_until = gdn_layers[-1] if gdn_layers else -1
    cond = jnp.concatenate([c, c_ctx[None], jnp.zeros((8 - b - 1, d), F32)], 0)
    for i in range(depth):
        kind, j = i % N_MIXERS, i // N_MIXERS
        ctx_in, ctx_out = i <= ctx_until, i < ctx_until
        mod = _mm(cond, ada_w[i].astype(BF16), bias=ada_b[i], silu_in=True, tn=2048)
        mod = mod.reshape(8, N_MOD, 1, d)
        ml = [mod[:b, t] for t in range(N_MOD)]
        mc = [mod[b:b + 1, t] for t in range(N_MOD)]
        lg = [ln_g[i, t].reshape(1, 1, d) for t in range(2)]
        lb = [ln_b[i, t].reshape(1, 1, d) for t in range(2)]
        yc = None
        if kind == 0:
            yl = _hyena(xl, ml[1], ml[0], length, length, p, j)
            if ctx_out:
                yc = _hyena(xc, mc[1], mc[0], b * lc, lc, p, j)
        elif kind == 1:
            qc, kc, vc, gbc, gbtc, uc = _gdn_inputs(xc, mc[1], mc[0], b * lc, b, lc, p, j)
            ql, kl, vl, gbl, gbtl, ul = _gdn_inputs(xl, ml[1], ml[0], length, b, length, p, j)
            s0 = jnp.zeros((2, b, GD_HEADS, GD_DK, GD_DK), F32)
            oc, s_ctx = _gdn_scan(qc, kc, vc, gbc, gbtc, s0)
            ol, _ = _gdn_scan(ql, kl, vl, gbl, gbtl, s_ctx)
            yl = _gdn_out(ol.reshape(2, b * length, -1), ul, gd_norm_g[j], gd_out_w[j])
            if ctx_out:
                yc = _gdn_out(oc.reshape(2, b * lc, -1), uc, gd_norm_g[j], gd_out_w[j])
        else:
            yl = _fnet(xl, ml[1], ml[0], length, b, length, p, j)
            if ctx_out:
                yc = _fnet(xc, mc[1], mc[0], b * lc, b, lc, p, j)
        wq = pk_wq[i].astype(BF16)
        keys = pk_keys[i].astype(BF16)
        ut, vt = pk_u[i].astype(BF16).T, pk_v[i].astype(BF16)
        xl = _res_ln(xl, yl, ml[2], lg[0], lb[0], alpha, length)
        xl = _peer_ln(xl, ml[4], ml[3], ml[5], lg[1], lb[1], length, alpha, wq, keys, ut, vt)
        if ctx_out:
            xc = _res_ln(xc, yc, mc[2], lg[0], lb[0], alpha, b * lc)
            xc = _peer_ln(xc, mc[4], mc[3], mc[5], lg[1], lb[1], b * lc, alpha, wq, keys, ut, vt)
    return xl.reshape(b, length, d)
```

```python
import functools
import math

import numpy as np
import jax
import jax.numpy as jnp
from jax import lax
from jax.experimental import pallas as pl
from jax.experimental.pallas import tpu as pltpu

F32 = jnp.float32
BF16 = jnp.bfloat16

GRID_W = 64
N_MIXERS = 3
N_MOD = 6
LN_EPS = 1e-5
HY_EMB = 33
HY_BANDS = (HY_EMB - 1) // 2
HY_SHIFT = 0.05
HY_TARGET = 1e-2
HY_MIN_DECAY = math.log(HY_TARGET) / 1.5
HY_MAX_DECAY = math.log(HY_TARGET) / 0.3
GD_HEADS = 8
GD_DK = 128
GD_CHUNK = 64
FN_GROUPS = 4
PK_HEADS = 8
PK_NKEYS = 128
PK_DH = 128
PK_TOPK = 16
PK_TOK = 128
PK_TM_SCORE = 1024
PK_TM = 512
PK_EC = 4096
PK_UNROLL = 32

VMEM_LIMIT_BYTES = 56 * 1024 * 1024


def _cparams(*sem):
    return pltpu.CompilerParams(dimension_semantics=sem, vmem_limit_bytes=VMEM_LIMIT_BYTES)


def _dot(a, b):
    return jnp.dot(a, b, preferred_element_type=F32)


def _dot_hi(a, b):
    return jnp.dot(a, b, preferred_element_type=F32, precision=lax.Precision.HIGHEST)


def _split(a):
    hi = a.astype(BF16)
    lo = (a - hi.astype(F32)).astype(BF16)
    return hi, lo


def _dot3(a, b):
    ah, al = _split(a)
    bh, bl = _split(b)
    return _dot(ah, bh) + (_dot(ah, bl) + _dot(al, bh))


def _mm_body(*refs, has_mod, has_bias, silu_in):
    a_ref, w_ref = refs[0], refs[1]
    k = 2
    a = a_ref[...]
    if has_mod:
        a = a * (1.0 + refs[k][...]) + refs[k + 1][...]
        k += 2
    if silu_in:
        a = a * jax.nn.sigmoid(a)
    o = _dot(a.astype(BF16), w_ref[...])
    if has_bias:
        o = o + refs[k][...]
        k += 1
    refs[k][...] = o


def _mm(a, w_bf16, bias=None, mod=None, rows_per_group=None, silu_in=False, tm=1024, tn=None):
    m, k = a.shape
    n = w_bf16.shape[1]
    tm = math.gcd(tm, m, rows_per_group or m)
    if tn is None:
        tn = max(t for t in range(128, min(n, 2048) + 1, 128) if n % t == 0)
    assert m % tm == 0 and n % tn == 0
    ins = [a, w_bf16]
    specs = [pl.BlockSpec((tm, k), lambda j, i: (i, 0)), pl.BlockSpec((k, tn), lambda j, i: (0, j))]
    if mod is not None:
        assert rows_per_group % tm == 0
        g = rows_per_group // tm
        for v in mod:
            ins.append(v)
            specs.append(pl.BlockSpec((None, 1, k), lambda j, i: (i // g, 0, 0)))
    if bias is not None:
        ins.append(bias.reshape(1, n))
        specs.append(pl.BlockSpec((1, tn), lambda j, i: (0, j)))
    return pl.pallas_call(
        functools.partial(_mm_body, has_mod=mod is not None, has_bias=bias is not None, silu_in=silu_in),
        grid=(n // tn, m // tm),
        in_specs=specs,
        out_specs=pl.BlockSpec((tm, tn), lambda j, i: (i, j)),
        out_shape=jax.ShapeDtypeStruct((m, n), F32),
        compiler_params=_cparams("parallel", "parallel"),
        name="mm",
    )(*ins)


def _rowwise(fn, rows, vecs, out_cols, rows_per_group=None, tm=512, name="rowwise"):
    m = rows[0].shape[0]
    tm = math.gcd(tm, m, rows_per_group or m)
    assert m % tm == 0
    n_r, n_v, n_o = len(rows), len(vecs), len(out_cols)

    def body(*refs):
        outs = fn(*[r[...] for r in refs[:n_r + n_v]])
        for o_ref, o in zip(refs[n_r + n_v:], outs):
            o_ref[...] = o

    specs = [pl.BlockSpec((tm, r.shape[1]), lambda i: (i, 0)) for r in rows]
    for v in vecs:
        if v.shape[0] == 1:
            specs.append(pl.BlockSpec((None, 1, v.shape[2]), lambda i: (0, 0, 0)))
        else:
            assert rows_per_group % tm == 0
            g = rows_per_group // tm
            specs.append(pl.BlockSpec((None, 1, v.shape[2]), lambda i, g=g: (i // g, 0, 0)))
    return pl.pallas_call(
        body,
        grid=(m // tm,),
        in_specs=specs,
        out_specs=[pl.BlockSpec((tm, c), lambda i: (i, 0)) for c in out_cols],
        out_shape=[jax.ShapeDtypeStruct((m, c), F32) for c in out_cols],
        compiler_params=_cparams("parallel"),
        name=name,
    )(*rows, *vecs)


def _ln(v, g, b):
    mu = jnp.mean(v, -1, keepdims=True)
    d = v - mu
    var = jnp.mean(d * d, -1, keepdims=True)
    return d * lax.rsqrt(var + LN_EPS) * g + b


def _res_ln(x, y, gate, ln_g, ln_b, alpha, rows_per_group):
    fn = lambda xt, yt, gt, lg, lb: (_ln(alpha * xt + gt * yt, lg, lb),)
    return _rowwise(fn, [x, y], [gate, ln_g, ln_b], [x.shape[1]], rows_per_group, name="res_ln")[0]


def _pos_add_body(x_ref, er_ref, ec_ref, o_ref):
    half = er_ref.shape[-1]
    x = x_ref[...]
    er = jnp.broadcast_to(er_ref[...], x.shape[:2] + (half,))
    ec = jnp.broadcast_to(ec_ref[...][None], x.shape[:2] + (half,))
    o_ref[...] = x + jnp.concatenate([er, ec], -1)


def _pos_add(x):
    b, length, d = x.shape
    rows = length // GRID_W
    quarter = d // 4
    omega = 1.0 / (10000.0 ** (jnp.arange(quarter, dtype=F32) / quarter))
    er = jnp.arange(rows, dtype=F32)[:, None] * omega
    ec = jnp.arange(GRID_W, dtype=F32)[:, None] * omega
    emb_r = jnp.concatenate([jnp.sin(er), jnp.cos(er)], -1).reshape(rows, 1, d // 2)
    emb_c = jnp.concatenate([jnp.sin(ec), jnp.cos(ec)], -1)
    rt = 8
    x4 = x.reshape(b, rows, GRID_W, d)
    out = pl.pallas_call(
        _pos_add_body,
        grid=(b, rows // rt),
        in_specs=[pl.BlockSpec((None, rt, GRID_W, d), lambda i, j: (i, j, 0, 0)),
                  pl.BlockSpec((rt, 1, d // 2), lambda i, j: (j, 0, 0)),
                  pl.BlockSpec((GRID_W, d // 2), lambda i, j: (0, 0))],
        out_specs=pl.BlockSpec((None, rt, GRID_W, d), lambda i, j: (i, j, 0, 0)),
        out_shape=jax.ShapeDtypeStruct(x4.shape, F32),
        compiler_params=_cparams("parallel", "parallel"),
        name="pos_add",
    )(x4, emb_r, emb_c)
    return out.reshape(b * length, d)


def _conv3_body(*refs, n_parts, tm, seq_len, post, n_vec):
    i = pl.program_id(0)
    first = (i * tm) % seq_len == 0
    last = ((i + 1) * tm) % seq_len == 0
    row = lax.broadcasted_iota(jnp.int32, (tm, 1), 0)
    parts = []
    for p in range(n_parts):
        main_ref, prev_ref, next_ref, w_ref = refs[4 * p:4 * p + 4]
        u = main_ref[...]
        w = w_ref[...]
        prev = jnp.where(first, 0.0, prev_ref[7:8, :])
        nxt = jnp.where(last, 0.0, next_ref[0:1, :])
        up = jnp.where(row == 0, prev, pltpu.roll(u, 1, 0))
        dn = jnp.where(row == tm - 1, nxt, pltpu.roll(u, tm - 1, 0))
        parts.append(up * w[0:1, :] + u * w[1:2, :] + dn * w[2:3, :])
    k = 4 * n_parts
    vecs = [refs[k + j][...] for j in range(n_vec)]
    outs = post(*parts, *vecs)
    for o_ref, o in zip(refs[k + n_vec:], outs):
        o_ref[...] = o


def _conv3(u, w, col_parts, tc, seq_len, post, n_out, vecs=(), tm=256, name="conv3"):
    m = u.shape[0]
    tm = min(tm, seq_len)
    assert seq_len % tm == 0 and m % tm == 0 and tm % 8 == 0
    width = col_parts[1] - col_parts[0] if len(col_parts) > 1 else tc
    ncol = width // tc
    t8 = tm // 8
    nb8 = m // 8
    ins, specs = [], []
    for c0 in col_parts:
        cb = c0 // tc
        ins += [u, u, u, w]
        specs += [
            pl.BlockSpec((tm, tc), lambda i, j, cb=cb: (i, cb + j)),
            pl.BlockSpec((8, tc), lambda i, j, cb=cb: (jnp.maximum(i * t8 - 1, 0), cb + j)),
            pl.BlockSpec((8, tc), lambda i, j, cb=cb: (jnp.minimum((i + 1) * t8, nb8 - 1), cb + j)),
            pl.BlockSpec((3, tc), lambda i, j, cb=cb: (0, cb + j)),
        ]
    for v in vecs:
        ins.append(v)
        specs.append(pl.BlockSpec((1, tc), lambda i, j: (0, j)))
    return pl.pallas_call(
        functools.partial(_conv3_body, n_parts=len(col_parts), tm=tm, seq_len=seq_len, post=post,
                          n_vec=len(vecs)),
        grid=(m // tm, ncol),
        in_specs=specs,
        out_specs=[pl.BlockSpec((tm, tc), lambda i, j: (i, j)) for _ in range(n_out)],
        out_shape=[jax.ShapeDtypeStruct((m, width), F32) for _ in range(n_out)],
        compiler_params=_cparams("parallel", "parallel"),
        name=name,
    )(*ins)


def _lmm_body(*refs, n_x):
    w_ref, o_ref = refs[0], refs[1 + n_x]
    tc = o_ref.shape[-1]
    xs = [r[...].reshape(-1, tc) for r in refs[1:1 + n_x]]
    x = xs[0] if n_x == 1 else jnp.concatenate(xs, 0)
    w = w_ref[...]
    w = w.reshape(w.shape[-2], w.shape[-1])
    o_ref[...] = _dot(w, x.astype(BF16)).reshape(o_ref.shape).astype(o_ref.dtype)


def _lmm(w, w_spec, xs, x_specs, out_shape, out_spec, grid, name, out_dtype=F32):
    return pl.pallas_call(
        functools.partial(_lmm_body, n_x=len(xs)),
        grid=grid,
        in_specs=[w_spec] + list(x_specs),
        out_specs=out_spec,
        out_shape=jax.ShapeDtypeStruct(out_shape, out_dtype),
        compiler_params=_cparams(*(["parallel"] * len(grid))),
        name=name,
    )(w, *xs)


def _spec_mul_body(w1_ref, w2_ref, x_ref, h_ref, o_ref):
    tc = o_ref.shape[-1]
    x = x_ref[...].reshape(-1, tc)
    z = _dot(w1_ref[...], x.astype(BF16))
    half = z.shape[0] // 2
    zr, zi = z[:half], z[half:]
    hr, hi = h_ref[0], h_ref[1]
    y = jnp.concatenate([zr * hr - zi * hi, zr * hi + zi * hr], 0)
    o_ref[...] = _dot(w2_ref[...], y.astype(BF16)).reshape(o_ref.shape).astype(o_ref.dtype)


def _cplx_mat(ang):
    c, s = jnp.cos(ang), jnp.sin(ang)
    return jnp.concatenate([jnp.concatenate([c, -s], -1), jnp.concatenate([s, c], -1)], -2)


def _phase(k, n):
    return (k % n).astype(F32) * (2.0 * math.pi / n)


def _outer_mats(n1, n2, n_in, sign):
    n = n1 * n2
    s1 = jnp.arange(n1, dtype=jnp.int32)[:, None, None]
    f2 = jnp.arange(n2, dtype=jnp.int32)[None, :, None]
    s2 = jnp.arange(n_in, dtype=jnp.int32)[None, None, :]
    return sign * _phase(f2 * (s1 + n1 * s2), n)


def _inner_phase(n1, sign):
    a = jnp.arange(n1, dtype=jnp.int32)
    return sign * _phase(a[:, None] * a[None, :], n1)


def _split_len(n):
    n1 = 1 << (int(math.log2(n)) // 2)
    return n1, n // n1


def _fft_conv_pair(z2, hspec, n1, n2, tcol=512):
    _, length, c = z2.shape
    n = 2 * length
    assert n1 * n2 == n
    h2 = n2 // 2
    m_in = _cplx_mat(_outer_mats(n1, n2, h2, -1.0)).astype(BF16)
    a = _lmm(m_in, pl.BlockSpec((1, 2 * n2, n2), lambda s: (s, 0, 0)),
             [z2.reshape(2, h2, n1 * c)], [pl.BlockSpec((2, h2, c), lambda s: (0, 0, s))],
             (2, n1, n2, c), pl.BlockSpec((2, 1, n2, c), lambda s: (0, s, 0, 0)), (n1,), "fft_in", BF16)
    w1 = _cplx_mat(_inner_phase(n1, -1.0)).astype(BF16)
    w2 = (_cplx_mat(_inner_phase(n1, 1.0)) * (1.0 / n)).astype(BF16)
    cols = n2 * c
    tcol = min(tcol, cols)
    b = pl.pallas_call(
        _spec_mul_body,
        grid=(cols // tcol,),
        in_specs=[pl.BlockSpec((2 * n1, 2 * n1), lambda j: (0, 0)),
                  pl.BlockSpec((2 * n1, 2 * n1), lambda j: (0, 0)),
                  pl.BlockSpec((2, n1, tcol), lambda j: (0, 0, j)),
                  pl.BlockSpec((2, n1, tcol), lambda j: (0, 0, j))],
        out_specs=pl.BlockSpec((2, n1, tcol), lambda j: (0, 0, j)),
        out_shape=jax.ShapeDtypeStruct((2, n1, cols), BF16),
        compiler_params=_cparams("parallel"),
        name="fft_mid",
    )(w1, w2, a.reshape(2, n1, cols), hspec)
    m_out = _cplx_mat(jnp.swapaxes(_outer_mats(n1, n2, h2, 1.0), 1, 2)).astype(BF16)
    y = _lmm(m_out, pl.BlockSpec((1, n2, 2 * n2), lambda s: (s, 0, 0)),
             [b.reshape(2, n1, n2, c)], [pl.BlockSpec((2, 1, n2, c), lambda s: (0, s, 0, 0))],
             (2, h2, n1 * c), pl.BlockSpec((2, h2, c), lambda s: (0, 0, s)), (n1,), "fft_out")
    return y.reshape(2, length, c)


def _fft_real_spectrum(f, n1, n2, tcol=512):
    n, c = f.shape
    if n2 == 1:
        ph = _inner_phase(n1, -1.0)
        w = jnp.concatenate([jnp.cos(ph), jnp.sin(ph)], 0).astype(BF16)
        tcol = min(tcol, c)
        return _lmm(w, pl.BlockSpec((2 * n1, n1), lambda j: (0, 0)),
                    [f], [pl.BlockSpec((n1, tcol), lambda j: (0, j))],
                    (2, n1, c), pl.BlockSpec((2, n1, tcol), lambda j: (0, 0, j)), (c // tcol,), "fft_spec1")
    ph = _outer_mats(n1, n2, n2, -1.0)
    m_in = jnp.concatenate([jnp.cos(ph), jnp.sin(ph)], 1).astype(BF16)
    a = _lmm(m_in, pl.BlockSpec((1, 2 * n2, n2), lambda s: (s, 0, 0)),
             [f.reshape(n2, n1 * c)], [pl.BlockSpec((n2, c), lambda s: (0, s))],
             (2, n1, n2, c), pl.BlockSpec((2, 1, n2, c), lambda s: (0, s, 0, 0)), (n1,), "fft_spec_in", BF16)
    w1 = _cplx_mat(_inner_phase(n1, -1.0)).astype(BF16)
    cols = n2 * c
    tcol = min(tcol, cols)
    h = _lmm(w1, pl.BlockSpec((2 * n1, 2 * n1), lambda j: (0, 0)),
             [a.reshape(2, n1, cols)], [pl.BlockSpec((2, n1, tcol), lambda j: (0, 0, j))],
             (2, n1, cols), pl.BlockSpec((2, n1, tcol), lambda j: (0, 0, j)), (cols // tcol,), "fft_spec_mid")
    return h


def _hy_filter_body(wt_ref, wc_ref, ws_ref, b1_ref, w2_ref, b2_ref, w3_ref, bands_ref, dl_ref, o_ref,
                    *, length, tr):
    d = o_ref.shape[-1]
    j = pl.program_id(0) * tr + lax.broadcasted_iota(jnp.int32, (tr, 1), 0)
    k = jnp.where(j < length, j, 2 * length - j)
    t = k.astype(F32) / length
    ang = 2.0 * jnp.pi * t * bands_ref[...]
    pre = t * wt_ref[...] + _dot_hi(jnp.cos(ang), wc_ref[...]) + _dot_hi(-jnp.sin(ang), ws_ref[...])
    hdn = jnp.sin(pre + b1_ref[...])
    hdn = jnp.sin(_dot_hi(hdn, w2_ref[...]) + b2_ref[...])
    hf = _dot3(hdn, w3_ref[...])
    win = jnp.exp(-t * dl_ref[...]) + HY_SHIFT
    h = jnp.where(j < length, hf[:, :d], hf[:, d:]) * win
    o_ref[...] = jnp.where(j == length, 0.0, h)


def _hy_filter(length, f_w1, f_b1, f_w2, f_b2, f_w3):
    d = f_w3.shape[1] // 2
    ffn = f_w2.shape[0]
    tr = min(512, length)
    bands = jnp.linspace(1e-4, HY_BANDS - 1, HY_BANDS, dtype=F32).reshape(1, HY_BANDS)
    deltas = jnp.abs(jnp.linspace(HY_MIN_DECAY, HY_MAX_DECAY, d, dtype=F32)).reshape(1, d)
    ins = [f_w1[0:1], f_w1[1:1 + HY_BANDS], f_w1[1 + HY_BANDS:], f_b1.reshape(1, ffn), f_w2,
           f_b2.reshape(1, ffn), f_w3, bands, deltas]
    return pl.pallas_call(
        functools.partial(_hy_filter_body, length=length, tr=tr),
        grid=(2 * length // tr,),
        in_specs=[pl.BlockSpec(a.shape, lambda i: (0, 0)) for a in ins],
        out_specs=pl.BlockSpec((tr, d), lambda i: (i, 0)),
        out_shape=jax.ShapeDtypeStruct((2 * length, d), F32),
        compiler_params=_cparams("parallel"),
        name="hy_filter",
    )(*ins)


def _hyena(x, scale, shift, rows_per_group, length, p, j):
    m, d = x.shape
    assert m == 2 * length
    u = _mm(x, p["hy_in_w"][j].astype(BF16), bias=p["hy_in_b"][j], mod=(scale, shift),
            rows_per_group=rows_per_group)
    post = lambda x0, x1, v: (x0, v * x1)
    x0c, z = _conv3(u, p["hy_conv"][j], [0, d, 2 * d], 512, length, post, 2, name="hy_conv")
    filt = _hy_filter(length, p["hy_f_w1"][j], p["hy_f_b1"][j], p["hy_f_w2"][j], p["hy_f_b2"][j],
                      p["hy_f_w3"][j])
    n = 2 * length
    n1, n2 = (n, 1) if n <= 1024 else _split_len(n)
    hspec = _fft_real_spectrum(filt, n1, n2)
    if n2 == 1:
        z2 = jnp.pad(z.reshape(2, length, d), ((0, 0), (0, length), (0, 0)))
        y = _fft_conv_pair_single(z2, hspec, n)[:, :length]
    else:
        y = _fft_conv_pair(z.reshape(2, length, d), hspec, n1, n2)
    fn = lambda yt, zt, x0t, sk: ((yt + sk * zt) * x0t,)
    g = _rowwise(fn, [y.reshape(m, d), z, x0c], [p["hy_skip"][j].reshape(1, 1, d)], [d], name="hy_gate")[0]
    return _mm(g, p["hy_out_w"][j].astype(BF16), bias=p["hy_out_b"][j])


def _fft_conv_pair_single(z2, hspec, n, tcol=512):
    c = z2.shape[-1]
    w1 = _cplx_mat(_inner_phase(n, -1.0)).astype(BF16)
    w2 = (_cplx_mat(_inner_phase(n, 1.0)) * (1.0 / n)).astype(BF16)
    tcol = min(tcol, c)
    return pl.pallas_call(
        _spec_mul_body,
        grid=(c // tcol,),
        in_specs=[pl.BlockSpec((2 * n, 2 * n), lambda j: (0, 0)),
                  pl.BlockSpec((2 * n, 2 * n), lambda j: (0, 0)),
                  pl.BlockSpec((2, n, tcol), lambda j: (0, 0, j)),
                  pl.BlockSpec((2, n, tcol), lambda j: (0, 0, j))],
        out_specs=pl.BlockSpec((2, n, tcol), lambda j: (0, 0, j)),
        out_shape=jax.ShapeDtypeStruct((2, n, c), F32),
        compiler_params=_cparams("parallel"),
        name="fft_mid1",
    )(w1, w2, z2, hspec)


def _fnet(x, scale, shift, rows_per_group, batch, length, p, j):
    m, d = x.shape
    gc = d // FN_GROUPS
    ph = _inner_phase(gc, -1.0)
    eye = jnp.eye(FN_GROUPS, dtype=F32)
    w_c = jnp.concatenate([jnp.kron(eye, jnp.cos(ph)), jnp.kron(eye, jnp.sin(ph))], 1).astype(BF16)
    w = _mm(x, w_c, mod=(scale, shift), rows_per_group=rows_per_group)
    if length <= 1024:
        n1, n2 = length, 1
    else:
        n1, n2 = _split_len(length)
    norm = 1.0 / math.sqrt(length * gc)
    if n2 == 1:
        ph1 = _inner_phase(n1, -1.0)
        wr = (jnp.concatenate([jnp.cos(ph1), -jnp.sin(ph1)], 1) * norm).astype(BF16)
        y = _lmm(wr, pl.BlockSpec((n1, 2 * n1), lambda b, c: (0, 0)),
                 [w.reshape(batch, n1, 2 * d)] * 2,
                 [pl.BlockSpec((None, n1, d), lambda b, c: (b, 0, 0)),
                  pl.BlockSpec((None, n1, d), lambda b, c: (b, 0, 1))],
                 (batch, n1, d), pl.BlockSpec((None, n1, d), lambda b, c: (b, 0, 0)), (batch, 1), "fn_pos1")
        y = y.reshape(m, d)
    else:
        m_in = _cplx_mat(_outer_mats(n1, n2, n2, -1.0)).astype(BF16)
        wv = w.reshape(batch, n2, n1 * 2 * d)
        a = _lmm(m_in, pl.BlockSpec((1, 2 * n2, 2 * n2), lambda b, s: (s, 0, 0)),
                 [wv, wv],
                 [pl.BlockSpec((None, n2, d), lambda b, s: (b, 0, 2 * s)),
                  pl.BlockSpec((None, n2, d), lambda b, s: (b, 0, 2 * s + 1))],
                 (batch, 2, n1, n2, d), pl.BlockSpec((None, 2, 1, n2, d), lambda b, s: (b, 0, s, 0, 0)),
                 (batch, n1), "fn_pos_in", BF16)
        ph1 = _inner_phase(n1, -1.0)
        wr = (jnp.concatenate([jnp.cos(ph1), -jnp.sin(ph1)], 1) * norm).astype(BF16)
        cols = n2 * d
        tcol = 1024
        y = _lmm(wr, pl.BlockSpec((n1, 2 * n1), lambda b, c: (0, 0)),
                 [a.reshape(batch, 2, n1, cols)],
                 [pl.BlockSpec((None, 2, n1, tcol), lambda b, c: (b, 0, 0, c))],
                 (batch, n1, cols), pl.BlockSpec((None, n1, tcol), lambda b, c: (b, 0, c)),
                 (batch, cols // tcol), "fn_pos_mid")
        y = y.reshape(m, d)
    return _mm(y, p["fn_out_w"][j].astype(BF16), bias=p["fn_out_b"][j])


def _head_l2(t, extra):
    outs = []
    for h in range(t.shape[1] // GD_DK):
        th = t[:, h * GD_DK:(h + 1) * GD_DK]
        outs.append(th * (lax.rsqrt(jnp.sum(th * th, -1, keepdims=True) + 1e-6) * extra))
    return jnp.concatenate(outs, -1)


def _silu(v):
    return v * jax.nn.sigmoid(v)


def _gdn_scan_body(q_ref, k_ref, v_ref, gb_ref, gbt_ref, s0_ref, o_ref, sfin_ref, s_scr, *, n_chunks):
    direction = pl.program_id(0)
    c = pl.program_id(1)
    cs = GD_CHUNK
    nb = q_ref.shape[0]

    @pl.when(c == 0)
    def _():
        s_scr[...] = s0_ref[...]

    ri = lax.broadcasted_iota(jnp.int32, (cs, cs), 0)
    ci = lax.broadcasted_iota(jnp.int32, (cs, cs), 1)
    lag = (ri - ci) * (1 - 2 * direction)
    incl = lag >= 0
    strict = lag > 0
    tri = incl.astype(F32)
    tri_t = (lag <= 0).astype(F32)
    eye = (ri == ci).astype(F32)
    pair_masks = []
    for lvl in range(int(math.log2(cs))):
        rb, cb = lax.shift_right_logical(ri, lvl), lax.shift_right_logical(ci, lvl)
        pair_masks.append((jnp.abs(rb - cb) == 1) & ((jnp.minimum(rb, cb) & 1) == 0))
    gb = [gb_ref[b] for b in range(nb)]
    gc_cols = [_dot_hi(tri, gb[b]) for b in range(nb)]
    gc_rows = [_dot_hi(gbt_ref[b], tri_t) for b in range(nb)]
    tot = [jnp.sum(gb[b], 0, keepdims=True) for b in range(nb)]
    ch = [(b, h) for b in range(nb) for h in range(GD_HEADS)]
    hs = range(len(ch))
    nt = (((1,), (1,)), ((), ()))
    tn = (((0,), (0,)), ((), ()))
    sl = [slice(h * GD_DK, (h + 1) * GD_DK) for _, h in ch]
    k = [k_ref[b, :, sl[i]] for i, (b, h) in enumerate(ch)]
    gcol = [gc_cols[b][:, h:h + 1] for b, h in ch]
    beta = [gb[b][:, GD_HEADS + h:GD_HEADS + h + 1] for b, h in ch]
    gtot = [tot[b][:, h:h + 1] for b, h in ch]
    decay = [jnp.exp(jnp.where(incl, gcol[i] - gc_rows[b][h:h + 1, :], -jnp.inf)) for i, (b, h) in enumerate(ch)]
    eg = [jnp.exp(gcol[h]) for h in hs]
    kb = [k[h] * beta[h] for h in hs]
    kbf = [k[h].astype(BF16) for h in hs]
    kk = [lax.dot_general(kb[h].astype(BF16), kbf[h], nt, preferred_element_type=F32) for h in hs]
    a = [jnp.where(strict, kk[h] * decay[h], 0.0) for h in hs]
    inv = [eye - jnp.where(pair_masks[0], a[h], 0.0) for h in hs]
    for pm in pair_masks[1:]:
        tn_s = [_dot3(inv[h], jnp.where(pm, a[h], 0.0)) for h in hs]
        tnt = [_dot3(tn_s[h], inv[h]) for h in hs]
        inv = [inv[h] - tnt[h] for h in hs]
    rhs = [jnp.concatenate([v_ref[b, :, sl[i]] * beta[i], kb[i] * eg[i]], -1) for i, (b, h) in enumerate(ch)]
    sol = [_dot3(inv[h], rhs[h]) for h in hs]
    q = [q_ref[b, :, sl[i]] for i, (b, h) in enumerate(ch)]
    qk = [lax.dot_general(q[h].astype(BF16), kbf[h], nt, preferred_element_type=F32) for h in hs]
    sb = [s_scr[b, h].astype(BF16) for b, h in ch]
    ws = [_dot(sol[h][:, GD_DK:].astype(BF16), sb[h]) for h in hs]
    qs = [_dot((q[h] * eg[h]).astype(BF16), sb[h]) for h in hs]
    vnb = [(sol[h][:, :GD_DK] - ws[h]).astype(BF16) for h in hs]
    av = [_dot((qk[h] * decay[h]).astype(BF16), vnb[h]) for h in hs]
    kv = [lax.dot_general((k[h] * jnp.exp(gtot[h] - gcol[h])).astype(BF16), vnb[h], tn,
                          preferred_element_type=F32) for h in hs]
    for i, (b, h) in enumerate(ch):
        s_scr[b, h] = s_scr[b, h] * jnp.exp(gtot[i]) + kv[i]
    for b in range(nb):
        o_ref[b] = jnp.concatenate([qs[i] + av[i] for i, (bb, _) in enumerate(ch) if bb == b], -1)

    @pl.when(c == n_chunks - 1)
    def _():
        sfin_ref[...] = s_scr[...]


def _gdn_scan(q, k, v, gb, gbt, s0):
    b, length, w = q.shape
    n_chunks = length // GD_CHUNK
    cidx = lambda d, c: c + d * (n_chunks - 1 - 2 * c)
    seq = lambda d, c: (0, cidx(d, c), 0)
    state = pl.BlockSpec((None, b, GD_HEADS, GD_DK, GD_DK), lambda d, c: (d, 0, 0, 0, 0))
    return pl.pallas_call(
        functools.partial(_gdn_scan_body, n_chunks=n_chunks),
        grid=(2, n_chunks),
        in_specs=[pl.BlockSpec((b, GD_CHUNK, w), seq),
                  pl.BlockSpec((b, GD_CHUNK, w), seq),
                  pl.BlockSpec((b, GD_CHUNK, w), seq),
                  pl.BlockSpec((None, b, GD_CHUNK, 128), lambda d, c: (d, 0, cidx(d, c), 0)),
                  pl.BlockSpec((None, b, None, 16, GD_CHUNK), lambda d, c: (d, 0, cidx(d, c), 0, 0)),
                  state],
        out_specs=[pl.BlockSpec((None, b, GD_CHUNK, w), lambda d, c: (d, 0, cidx(d, c), 0)), state],
        out_shape=[jax.ShapeDtypeStruct((2, b, length, w), F32),
                   jax.ShapeDtypeStruct((2, b, GD_HEADS, GD_DK, GD_DK), F32)],
        scratch_shapes=[pltpu.VMEM((b, GD_HEADS, GD_DK, GD_DK), F32)],
        compiler_params=_cparams("parallel", "arbitrary"),
        name="gdn_scan",
    )(q, k, v, gb, gbt, s0)


def _gdn_inputs(x, scale, shift, rows_per_group, batch, length, p, j):
    m, d = x.shape
    wd = GD_HEADS * GD_DK
    in_w = p["gd_in_w"][j]
    u = _mm(x, in_w[:, :4 * wd].astype(BF16), mod=(scale, shift), rows_per_group=rows_per_group)
    w_ab = jnp.pad(in_w[:, 4 * wd:], ((0, 0), (0, 128 - 4 * GD_HEADS))).astype(BF16)
    ab = _mm(x, w_ab, mod=(scale, shift), rows_per_group=rows_per_group)
    qscale = GD_DK ** -0.5
    post = lambda qc, kc, vc: (_head_l2(_silu(qc), qscale), _head_l2(_silu(kc), 1.0), _silu(vc))
    q, k, v = _conv3(u, p["gd_conv"][j], [0, wd, 2 * wd], 512, length, post, 3, name="gd_conv")
    nh = GD_HEADS
    a_par = jnp.zeros((1, 1, 128), F32).at[0, 0, :2 * nh].set(-jnp.exp(p["gd_a_log"][j]).reshape(-1))
    dt_par = jnp.zeros((1, 1, 128), F32).at[0, 0, :2 * nh].set(p["gd_dt_bias"][j].reshape(-1))

    def gate_fn(abt, an, dtb):
        pre = abt + dtb
        sp = jnp.maximum(pre, 0.0) + jnp.log(1.0 + jnp.exp(-jnp.abs(pre)))
        lane = lax.broadcasted_iota(jnp.int32, abt.shape, 1)
        return (jnp.where(lane < 2 * nh, an * sp, jax.nn.sigmoid(abt)),)

    gall = _rowwise(gate_fn, [ab], [a_par, dt_par], [128], name="gd_gate")[0]
    pad = jnp.zeros((m, 128 - 2 * nh), F32)
    gb = jnp.stack([jnp.concatenate([gall[:, dr * nh:(dr + 1) * nh],
                                     gall[:, (2 + dr) * nh:(3 + dr) * nh], pad], -1) for dr in range(2)])
    gb = gb.reshape(2, batch, length, 128)
    gbt = jnp.swapaxes(gb[..., :2 * nh].reshape(2, batch, length // GD_CHUNK, GD_CHUNK, 2 * nh), -1, -2)
    rs = lambda t: t.reshape(batch, length, wd)
    return rs(q), rs(k), rs(v), gb, gbt, u


def _gdn_out(o2, u, norm_g, out_w):
    m, wd = o2.shape[1], o2.shape[2]
    ng = jnp.tile(norm_g, wd // norm_g.shape[0]).reshape(1, 1, wd)

    def body(of_ref, ob_ref, z_ref, ng_ref, o_ref):
        o = of_ref[...] + ob_ref[...]
        outs = []
        for h in range(wd // GD_DK):
            oh = o[:, h * GD_DK:(h + 1) * GD_DK]
            outs.append(oh * lax.rsqrt(jnp.mean(oh * oh, -1, keepdims=True) + 1e-6))
        o_ref[...] = jnp.concatenate(outs, -1) * ng_ref[...] * _silu(z_ref[...])

    tm = math.gcd(512, m)
    g = pl.pallas_call(
        body,
        grid=(m // tm,),
        in_specs=[pl.BlockSpec((None, tm, wd), lambda i: (0, i, 0)),
                  pl.BlockSpec((None, tm, wd), lambda i: (1, i, 0)),
                  pl.BlockSpec((tm, wd), lambda i: (i, 3)),
                  pl.BlockSpec((None, 1, wd), lambda i: (0, 0, 0))],
        out_specs=pl.BlockSpec((tm, wd), lambda i: (i, 0)),
        out_shape=jax.ShapeDtypeStruct((m, wd), F32),
        compiler_params=_cparams("parallel"),
        name="gd_norm",
    )(o2, o2, u, ng)
    return _mm(g, out_w.astype(BF16))


def _topk_rows(s, n_take, val_ref, idx_ref, base, rid=None, ordered_groups=0):
    big = jnp.iinfo(jnp.int32).max
    groups = s.shape[0] // 8
    if rid is None:
        rid = lax.broadcasted_iota(jnp.int32, s.shape, 0)
        ordered_groups = groups
    grp = lambda a, j: a[8 * j:8 * j + 8]
    for t in range(n_take):
        mx = jnp.max(s, 0, keepdims=True)
        first = jnp.full((8, s.shape[1]), big, jnp.int32)
        for j in reversed(range(ordered_groups)):
            first = jnp.where(grp(s, j) == mx, grp(rid, j), first)
        for j in range(ordered_groups, groups):
            first = jnp.minimum(first, jnp.where(grp(s, j) == mx, grp(rid, j), big))
        am = jnp.min(first, 0, keepdims=True)
        val_ref[base + t:base + t + 1, :] = mx
        idx_ref[base + t:base + t + 1, :] = am
        s = jnp.where(rid == am, -jnp.inf, s)


def _pk_topk_body(q_ref, keys_ref, i1_ref, i2_ref, gate_ref, sv_scr, si_scr, cv_scr, ci_scr,
                  i1p_scr, i2p_scr, gp_scr):
    kk = PK_TOPK
    tt = q_ref.shape[0]
    for h in range(PK_HEADS):
        for p in range(2):
            qh = q_ref[:, (2 * h + p) * PK_DH:(2 * h + p + 1) * PK_DH].astype(BF16)
            st = lax.dot_general(keys_ref[h, p], qh, (((1,), (1,)), ((), ())),
                                 preferred_element_type=F32)
            _topk_rows(st, kk, sv_scr, si_scr, p * kk)
        sv1, sv2 = sv_scr[0:kk, :], sv_scr[kk:2 * kk, :]
        si1, si2 = si_scr[0:kk, :], si_scr[kk:2 * kk, :]
        r8 = lax.broadcasted_iota(jnp.int32, (8, tt), 0)
        cand = jnp.concatenate([sv1[0:8, :] + sv2[r2:r2 + 1, :] for r2 in range(8)]
                               + [sv1[8:kk, :] + sv2[0:1, :], sv1[0:1, :] + sv2[8:kk, :]], 0)
        cid = jnp.concatenate([r8 * kk + r2 for r2 in range(8)] + [(r8 + 8) * kk, r8 + 8], 0)
        _topk_rows(cand, kk, cv_scr, ci_scr, 0, cid, ordered_groups=9)
        cv, ci = cv_scr[...], ci_scr[...]
        a1, a2 = lax.shift_right_logical(ci, int(math.log2(kk))), ci & (kk - 1)
        i1 = jnp.zeros((kk, tt), jnp.int32)
        i2 = jnp.zeros((kk, tt), jnp.int32)
        for r in range(kk):
            i1 = i1 + jnp.where(a1 == r, si1[r:r + 1, :], 0)
            i2 = i2 + jnp.where(a2 == r, si2[r:r + 1, :], 0)
        i1p_scr[h * kk:(h + 1) * kk, :] = i1
        i2p_scr[h * kk:(h + 1) * kk, :] = i2
        e = jnp.exp(cv - jnp.max(cv, 0, keepdims=True))
        gp_scr[h * kk:(h + 1) * kk, :] = e / jnp.sum(e, 0, keepdims=True)
    i1_ref[...] = i1p_scr[...].T
    i2_ref[...] = i2p_scr[...].T
    gate_ref[...] = gp_scr[...].T


def _pk_topk(q, keys_bf16):
    m = q.shape[0]
    tt = PK_TOK
    hk = PK_HEADS * PK_TOPK
    assert hk == tt
    return pl.pallas_call(
        _pk_topk_body,
        grid=(m // tt,),
        in_specs=[pl.BlockSpec((tt, q.shape[1]), lambda i: (i, 0)),
                  pl.BlockSpec(keys_bf16.shape, lambda i: (0, 0, 0, 0))],
        out_specs=[pl.BlockSpec((tt, hk), lambda i: (i, 0))] * 3,
        out_shape=[jax.ShapeDtypeStruct((m, hk), jnp.int32), jax.ShapeDtypeStruct((m, hk), jnp.int32),
                   jax.ShapeDtypeStruct((m, hk), F32)],
        scratch_shapes=[pltpu.VMEM((2 * PK_TOPK, tt), F32), pltpu.VMEM((2 * PK_TOPK, tt), jnp.int32),
                        pltpu.VMEM((PK_TOPK, tt), F32), pltpu.VMEM((PK_TOPK, tt), jnp.int32),
                        pltpu.VMEM((hk, tt), jnp.int32), pltpu.VMEM((hk, tt), jnp.int32),
                        pltpu.VMEM((hk, tt), F32)],
        compiler_params=_cparams("parallel"),
        name="pk_topk",
    )(q, keys_bf16)


def _pk_score_body(x_ref, sc_ref, sh_ref, i1_ref, i2_ref, ut_ref, act_ref, hb_scr):
    e = pl.program_id(1)
    nk = PK_NKEYS
    per = ut_ref.shape[1] // nk

    @pl.when(e == 0)
    def _():
        hb_scr[...] = (x_ref[...] * (1.0 + sc_ref[...]) + sh_ref[...]).astype(BF16)
        act_ref[...] = jnp.zeros_like(act_ref)

    hb = hb_scr[...]
    i1 = i1_ref[...]
    i2 = i2_ref[...]
    act = act_ref[...]
    for kp in range(per // 2):
        s = _dot(hb, ut_ref[:, 2 * kp * nk:2 * (kp + 1) * nk])
        for k in (2 * kp, 2 * kp + 1):
            got = jnp.take_along_axis(s[:, (k % 2) * nk:(k % 2 + 1) * nk], i2, axis=1)
            act = jnp.where(i1 == e * per + k, got, act)
    act_ref[...] = act


def _pk_score(x, scale, shift, i1, i2, ut_bf16, rows_per_group):
    m, d = x.shape
    n_exp = ut_bf16.shape[1]
    tm = math.gcd(PK_TM_SCORE, m, rows_per_group)
    g = rows_per_group // tm
    assert n_exp % PK_EC == 0 and PK_EC % (2 * PK_NKEYS) == 0
    row = lambda i, e: (i, 0)
    vec = lambda i, e: (i // g, 0, 0)
    npk = i1.shape[1]
    return pl.pallas_call(
        _pk_score_body,
        grid=(m // tm, n_exp // PK_EC),
        in_specs=[pl.BlockSpec((tm, d), row), pl.BlockSpec((None, 1, d), vec), pl.BlockSpec((None, 1, d), vec),
                  pl.BlockSpec((tm, npk), row), pl.BlockSpec((tm, npk), row),
                  pl.BlockSpec((d, PK_EC), lambda i, e: (0, e))],
        out_specs=pl.BlockSpec((tm, npk), row),
        out_shape=jax.ShapeDtypeStruct((m, npk), F32),
        scratch_shapes=[pltpu.VMEM((tm, d), BF16)],
        compiler_params=_cparams("parallel", "arbitrary"),
        name="pk_score",
    )(x, scale, shift, i1, i2, ut_bf16)


def _pk_combine_body(x_ref, gt_ref, lg_ref, lb_ref, i1_ref, i2_ref, gate_ref, act_ref, v_ref,
                     o_ref, w_scr, grid_scr, acc_scr, *, alpha, n_chunks):
    e = pl.program_id(1)
    tm = x_ref.shape[0]
    nk = PK_NKEYS
    per = v_ref.shape[0] // nk

    @pl.when(e == 0)
    def _():
        w_scr[...] = jax.nn.gelu(act_ref[...]) * gate_ref[...]
        acc_scr[...] = jnp.zeros_like(acc_scr)
        sub = lax.broadcasted_iota(jnp.int32, (nk, nk), 0)
        key1 = jnp.where(sub < nk // 2, 2 * sub, 2 * sub - (nk - 1))

        def tok(tg, carry):
            for t0 in range(0, PK_UNROLL, 8):
                words = []
                for u in range(8):
                    t = tg * PK_UNROLL + t0 + u
                    wrow = w_scr[pl.ds(t, 1), :]
                    at = jnp.where(key1 == i1_ref[pl.ds(t, 1), :], wrow, 0.0).astype(BF16)
                    bt = (sub == i2_ref[pl.ds(t, 1), :]).astype(BF16)
                    g = lax.dot_general(at, bt, (((1,), (1,)), ((), ())), preferred_element_type=F32)
                    g = lax.bitcast_convert_type(g.astype(BF16).astype(F32), jnp.int32)
                    words.append(lax.shift_right_logical(g[:nk // 2], 16) | (g[nk // 2:] & jnp.int32(-65536)))
                first = pl.multiple_of(tg * PK_UNROLL + t0, 8)
                grid_scr[:, pl.ds(first, 8), :] = jnp.swapaxes(jnp.stack(words, 0), 0, 1)
            return carry

        lax.fori_loop(0, tm // PK_UNROLL, tok, 0)

    parts = []
    for kp in range(per // 2):
        wd = grid_scr[e * (per // 2) + kp]
        parts.append(lax.bitcast_convert_type(lax.shift_left(wd, 16), F32).astype(BF16))
        parts.append(lax.bitcast_convert_type(wd & jnp.int32(-65536), F32).astype(BF16))
    acc_scr[...] += _dot(jnp.concatenate(parts, -1), v_ref[...])

    @pl.when(e == n_chunks - 1)
    def _():
        o_ref[...] = _ln(alpha * x_ref[...] + gt_ref[...] * acc_scr[...], lg_ref[...], lb_ref[...])


def _pk_combine(x, gate_vec, ln_g, ln_b, i1, i2, gate, act, v_bf16, rows_per_group, alpha):
    m, d = x.shape
    n_exp = v_bf16.shape[0]
    tm = math.gcd(PK_TM, m, rows_per_group)
    n_chunks = n_exp // PK_EC
    g = rows_per_group // tm
    assert n_exp % PK_EC == 0 and PK_EC % (2 * PK_NKEYS) == 0 and tm % PK_UNROLL == 0
    row = lambda i, e: (i, 0)
    vec = lambda i, e: (i // g, 0, 0)
    one = lambda i, e: (0, 0, 0)
    npk = i1.shape[1]
    return pl.pallas_call(
        functools.partial(_pk_combine_body, alpha=alpha, n_chunks=n_chunks),
        grid=(m // tm, n_chunks),
        in_specs=[pl.BlockSpec((tm, d), row), pl.BlockSpec((None, 1, d), vec),
                  pl.BlockSpec((None, 1, d), one), pl.BlockSpec((None, 1, d), one),
                  pl.BlockSpec((tm, npk), row), pl.BlockSpec((tm, npk), row), pl.BlockSpec((tm, npk), row),
                  pl.BlockSpec((tm, npk), row),
                  pl.BlockSpec((PK_EC, d), lambda i, e: (e, 0))],
        out_specs=pl.BlockSpec((tm, d), row),
        out_shape=jax.ShapeDtypeStruct((m, d), F32),
        scratch_shapes=[pltpu.VMEM((tm, npk), F32), pltpu.VMEM((PK_NKEYS // 2, tm, PK_NKEYS), jnp.int32),
                        pltpu.VMEM((tm, d), F32)],
        compiler_params=_cparams("parallel", "arbitrary"),
        name="pk_combine",
    )(x, gate_vec, ln_g, ln_b, i1, i2, gate, act, v_bf16)


def _peer_ln(x, scale, shift, gate_vec, ln_g, ln_b, rows_per_group, alpha, wq_bf16, keys_bf16, ut_bf16, v_bf16):
    q = _mm(x, wq_bf16, mod=(scale, shift), rows_per_group=rows_per_group)
    i1, i2, gate = _pk_topk(q, keys_bf16)
    act = _pk_score(x, scale, shift, i1, i2, ut_bf16, rows_per_group)
    return _pk_combine(x, gate_vec, ln_g, ln_b, i1, i2, gate, act, v_bf16, rows_per_group, alpha)


def kernel(x, c, ctx, c_ctx, ada_w, ada_b, ln_g, ln_b, pk_wq, pk_keys, pk_u, pk_v, hy_in_w, hy_in_b, hy_conv, hy_f_w1, hy_f_b1, hy_f_w2, hy_f_b2, hy_f_w3, hy_skip, hy_out_w, hy_out_b, gd_in_w, gd_conv, gd_a_log, gd_dt_bias, gd_norm_g, gd_out_w, fn_out_w, fn_out_b):
    p = dict(hy_in_w=hy_in_w, hy_in_b=hy_in_b, hy_conv=hy_conv, hy_f_w1=hy_f_w1, hy_f_b1=hy_f_b1,
             hy_f_w2=hy_f_w2, hy_f_b2=hy_f_b2, hy_f_w3=hy_f_w3, hy_skip=hy_skip, hy_out_w=hy_out_w,
             hy_out_b=hy_out_b, gd_in_w=gd_in_w, gd_conv=gd_conv, gd_a_log=gd_a_log, gd_dt_bias=gd_dt_bias,
             fn_out_w=fn_out_w, fn_out_b=fn_out_b)
    b, length, d = x.shape
    lc = ctx.shape[1]
    depth = ada_w.shape[0]
    alpha = (2 * depth) ** 0.25
    xl = _pos_add(x)
    xc = ctx.reshape(b * lc, d)
    gdn_layers = [i for i in range(depth) if i % N_MIXERS == 1]
    ctx_until = gdn_layers[-1] if gdn_layers else -1
    cond = jnp.concatenate([c, c_ctx[None], jnp.zeros((8 - b - 1, d), F32)], 0)
    for i in range(depth):
        kind, j = i % N_MIXERS, i // N_MIXERS
        ctx_in, ctx_out = i <= ctx_until, i < ctx_until
        mod = _mm(cond, ada_w[i].astype(BF16), bias=ada_b[i], silu_in=True, tn=2048)
        mod = mod.reshape(8, N_MOD, 1, d)
        ml = [mod[:b, t] for t in range(N_MOD)]
        mc = [mod[b:b + 1, t] for t in range(N_MOD)]
        lg = [ln_g[i, t].reshape(1, 1, d) for t in range(2)]
        lb = [ln_b[i, t].reshape(1, 1, d) for t in range(2)]
        yc = None
        if kind == 0:
            yl = _hyena(xl, ml[1], ml[0], length, length, p, j)
            if ctx_out:
                yc = _hyena(xc, mc[1], mc[0], b * lc, lc, p, j)
        elif kind == 1:
            qc, kc, vc, gbc, gbtc, uc = _gdn_inputs(xc, mc[1], mc[0], b * lc, b, lc, p, j)
            ql, kl, vl, gbl, gbtl, ul = _gdn_inputs(xl, ml[1], ml[0], length, b, length, p, j)
            s0 = jnp.zeros((2, b, GD_HEADS, GD_DK, GD_DK), F32)
            oc, s_ctx = _gdn_scan(qc, kc, vc, gbc, gbtc, s0)
            ol, _ = _gdn_scan(ql, kl, vl, gbl, gbtl, s_ctx)
            yl = _gdn_out(ol.reshape(2, b * length, -1), ul, gd_norm_g[j], gd_out_w[j])
            if ctx_out:
                yc = _gdn_out(oc.reshape(2, b * lc, -1), uc, gd_norm_g[j], gd_out_w[j])
        else:
            yl = _fnet(xl, ml[1], ml[0], length, b, length, p, j)
            if ctx_out:
                yc = _fnet(xc, mc[1], mc[0], b * lc, b, lc, p, j)
        wq = pk_wq[i].astype(BF16)
        keys = pk_keys[i].astype(BF16)
        ut, vt = pk_u[i].astype(BF16).T, pk_v[i].astype(BF16)
        xl = _res_ln(xl, yl, ml[2], lg[0], lb[0], alpha, length)
        xl = _peer_ln(xl, ml[4], ml[3], ml[5], lg[1], lb[1], length, alpha, wq, keys, ut, vt)
        if ctx_out:
            xc = _res_ln(xc, yc, mc[2], lg[0], lb[0], alpha, b * lc)
            xc = _peer_ln(xc, mc[4], mc[3], mc[5], lg[1], lb[1], b * lc, alpha, wq, keys, ut, vt)
    return xl.reshape(b, length, d)
```
